```python
import jax, jax.numpy as jnp
from jax import lax
import numpy as np

D_MODEL = 2048
BATCH = 1
SEQ = 8192
DEPTH = 1

CHUNK = 64
ROPE_THETA = 500000.0
EPS = 1e-6
A_HEADS = 16
A_HEAD_DIM = 128
A_ROT = A_HEAD_DIM // 4
IDX_HEADS = 16
IDX_DIM = 64
IDX_ROT = IDX_DIM // 4
TOPK_MAX = 256
Q_BLOCK = 128
B_QK_HEADS = 16
B_V_HEADS = 32
B_HEAD_DIM = 128
CONV_WIDTH = 4
N_GROUPS = 8
EXPERTS_PER_GROUP = 16
N_EXPERTS = N_GROUPS * EXPERTS_PER_GROUP
TOP_K_IN_GROUP = 2
D_FF_EXPERT = 768

A_W = A_HEADS * A_HEAD_DIM
A_IQ = IDX_HEADS * IDX_DIM
B_QK = B_QK_HEADS * B_HEAD_DIM
B_V = B_V_HEADS * B_HEAD_DIM
SPLIT_SIZES = (A_W, A_W, A_W, A_IQ, IDX_DIM, IDX_HEADS,
               B_QK, B_QK, B_V, B_V, B_V_HEADS, B_V_HEADS,
               D_MODEL, D_MODEL)
D_IN = sum(SPLIT_SIZES)
CONV_CH = 2 * B_QK + B_V

kernel_name = "hybrid_dsa_gdn_hmoe_block"


def rms_norm(x, gain):
    xf = x.astype(jnp.float32)
    y = xf * lax.rsqrt(jnp.mean(xf * xf, axis=-1, keepdims=True) + EPS)
    return (y * gain.astype(jnp.float32)).astype(x.dtype)


def l2_norm(x):
    return x * lax.rsqrt(jnp.sum(x * x, axis=-1, keepdims=True) + EPS)


def partial_rope(x, positions, rot):
    half = rot // 2
    inv_freq = jnp.power(ROPE_THETA, -jnp.arange(half, dtype=jnp.float32) * 2.0 / rot)
    ang = positions.astype(jnp.float32)[..., None] * inv_freq
    cos = jnp.cos(ang)[:, :, None, :]
    sin = jnp.sin(ang)[:, :, None, :]
    xf = x.astype(jnp.float32)
    x1 = xf[..., :half]
    x2 = xf[..., half:rot]
    out = jnp.concatenate([x1 * cos - x2 * sin, x2 * cos + x1 * sin, xf[..., rot:]], axis=-1)
    return out.astype(x.dtype)


def split_columns(a, sizes):
    parts = []
    start = 0
    for s in sizes:
        parts.append(a[..., start:start + s])
        start += s
    return parts


def dsa_branch(q, k, v, q_idx, k_idx, w_idx, positions, q_norm, k_norm, idx_k_norm):
    B, T = q.shape[0], q.shape[1]
    q = partial_rope(rms_norm(q.reshape(B, T, A_HEADS, A_HEAD_DIM), q_norm), positions, A_ROT)
    k = partial_rope(rms_norm(k.reshape(B, T, A_HEADS, A_HEAD_DIM), k_norm), positions, A_ROT)
    v = v.reshape(B, T, A_HEADS, A_HEAD_DIM)
    q_idx = partial_rope(q_idx.reshape(B, T, IDX_HEADS, IDX_DIM), positions, IDX_ROT)
    k_idx = partial_rope(rms_norm(k_idx, idx_k_norm)[:, :, None, :], positions, IDX_ROT)[:, :, 0, :]
    w_idx = w_idx.astype(jnp.float32) * (IDX_HEADS ** -0.5 * IDX_DIM ** -0.5)
    top_k = min(TOPK_MAX, T // 4)
    n_blk = T // Q_BLOCK
    key_pos = jnp.arange(T, dtype=jnp.int32)
    scale = A_HEAD_DIM ** -0.5

    def blockify(a):
        return a.reshape((B, n_blk, Q_BLOCK) + a.shape[2:]).swapaxes(0, 1)

    def one_block(args):
        qb, qib, wb, tb = args
        limit = (tb // CHUNK + 1) * CHUNK
        s = jnp.einsum('bqhd,bsd->bqhs', qib, k_idx).astype(jnp.float32)
        score = jnp.einsum('bqh,bqhs->bqs', wb, jax.nn.relu(s))
        admissible = key_pos[None, :] < limit[:, None]
        score = jnp.where(admissible[None], score, -jnp.inf)
        _, sel = lax.top_k(score, top_k)
        valid = sel < limit[None, :, None]
        kg = jax.vmap(lambda kk, ii: kk[ii])(k, sel)
        vg = jax.vmap(lambda vv, ii: vv[ii])(v, sel)
        logits = jnp.einsum('bqhd,bqkhd->bqhk', qb, kg).astype(jnp.float32) * scale
        logits = jnp.where(valid[:, :, None, :], logits, -jnp.inf)
        p = jax.nn.softmax(logits, axis=-1).astype(vg.dtype)
        return jnp.einsum('bqhk,bqkhd->bqhd', p, vg)

    t_blocks = jnp.arange(T, dtype=jnp.int32).reshape(n_blk, Q_BLOCK)
    out = lax.map(one_block, (blockify(q), blockify(q_idx), blockify(w_idx), t_blocks))
    return out.swapaxes(0, 1).reshape(B, T, A_W)


def chunk_gated_delta_rule(q, k, v, g, beta):
    B, T, H, dk = q.shape
    dv = v.shape[-1]
    n = T // CHUNK

    def chunks(a):
        return a.reshape(B, n, CHUNK, H, a.shape[-1]).transpose(1, 0, 3, 2, 4)

    q, k, v = chunks(q), chunks(k), chunks(v)
    g = g.reshape(B, n, CHUNK, H).transpose(1, 0, 3, 2)
    beta = beta.reshape(B, n, CHUNK, H).transpose(1, 0, 3, 2)
    gc = jnp.cumsum(g, axis=-1)
    kb = k * beta[..., None]
    vb = v * beta[..., None]
    idx = jnp.arange(CHUNK)
    incl = idx[:, None] >= idx[None, :]
    strict = idx[:, None] > idx[None, :]
    decay = jnp.exp(jnp.where(incl, gc[..., :, None] - gc[..., None, :], -jnp.inf))
    a_mat = jnp.where(strict, jnp.einsum('nbhcd,nbhsd->nbhcs', kb, k) * decay, 0.0)
    eye = jnp.eye(CHUNK, dtype=jnp.float32)
    t_mat = lax.linalg.triangular_solve(a_mat + eye, jnp.broadcast_to(eye, a_mat.shape),
                                        left_side=True, lower=True, unit_diagonal=True)
    u = jnp.einsum('nbhcs,nbhsd->nbhcd', t_mat, vb)
    w = jnp.einsum('nbhcs,nbhsd->nbhcd', t_mat, kb * jnp.exp(gc)[..., None])
    attn = jnp.einsum('nbhcd,nbhsd->nbhcs', q, k) * decay

    def step(state, xs):
        q_n, k_n, u_n, w_n, gc_n, attn_n = xs
        v_new = u_n - jnp.einsum('bhcd,bhde->bhce', w_n, state)
        o = (jnp.einsum('bhcd,bhde->bhce', q_n * jnp.exp(gc_n)[..., None], state)
             + jnp.einsum('bhcs,bhse->bhce', attn_n, v_new))
        g_last = gc_n[..., -1]
        state = (state * jnp.exp(g_last)[..., None, None]
                 + jnp.einsum('bhcd,bhce->bhde', k_n * jnp.exp(g_last[..., None] - gc_n)[..., None], v_new))
        return state, o

    s0 = jnp.zeros((B, H, dk, dv), jnp.float32)
    _, o = lax.scan(step, s0, (q, k, u, w, gc, attn))
    return o.transpose(1, 0, 3, 2, 4).reshape(B, T, H, dv)


def gdn_branch(q, k, v, z, b, a, conv_w, a_log, dt_bias, out_norm):
    B, T = q.shape[0], q.shape[1]
    qkv = jnp.concatenate([q, k, v], axis=-1)
    qkv = lax.conv_general_dilated(qkv, conv_w.astype(qkv.dtype)[:, None, :], window_strides=(1,),
                                   padding=[(CONV_WIDTH - 1, 0)],
                                   dimension_numbers=('NWC', 'WIO', 'NWC'),
                                   feature_group_count=CONV_CH)
    qkv = jax.nn.silu(qkv).astype(jnp.float32)
    q, k, v = split_columns(qkv, (B_QK, B_QK, B_V))
    rep = B_V_HEADS // B_QK_HEADS
    q = jnp.repeat(l2_norm(q.reshape(B, T, B_QK_HEADS, B_HEAD_DIM)), rep, axis=2) * (B_HEAD_DIM ** -0.5)
    k = jnp.repeat(l2_norm(k.reshape(B, T, B_QK_HEADS, B_HEAD_DIM)), rep, axis=2)
    v = v.reshape(B, T, B_V_HEADS, B_HEAD_DIM)
    beta = jax.nn.sigmoid(b.astype(jnp.float32))
    g = -jnp.exp(a_log.astype(jnp.float32)) * jax.nn.softplus(a.astype(jnp.float32) + dt_bias.astype(jnp.float32))
    o = chunk_gated_delta_rule(q, k, v, g, beta)
    o = rms_norm(o, out_norm) * jax.nn.silu(z.reshape(B, T, B_V_HEADS, B_HEAD_DIM).astype(jnp.float32))
    return o.reshape(B, T, B_V).astype(z.dtype)


def hier_moe(h, w_router_group, b_router_group, w_router_expert, b_router_expert, w_gate, w_up, w_down):
    B, T, D = h.shape
    n_tok = B * T
    hf = h.reshape(n_tok, D)
    p_group = jax.nn.softmax((hf @ w_router_group).astype(jnp.float32) + b_router_group.astype(jnp.float32), axis=-1)
    g_top, g_idx = lax.top_k(p_group, 1)
    le = ((hf @ w_router_expert).astype(jnp.float32) + b_router_expert.astype(jnp.float32))
    le = le.reshape(n_tok, N_GROUPS, EXPERTS_PER_GROUP)
    le_g = jnp.take_along_axis(le, g_idx[:, :, None], axis=1)[:, 0]
    p_exp = jax.nn.softmax(le_g, axis=-1)
    e_top, e_loc = lax.top_k(p_exp, TOP_K_IN_GROUP)
    weights = g_top * e_top / jnp.sum(e_top, axis=-1, keepdims=True)
    expert_ids = (g_idx * EXPERTS_PER_GROUP + e_loc).reshape(-1)
    flat_w = weights.reshape(-1)
    order = jnp.argsort(expert_ids)
    tok = order // TOP_K_IN_GROUP
    xs = hf[tok]
    sizes = jnp.bincount(expert_ids, length=N_EXPERTS).astype(jnp.int32)
    act = jax.nn.silu(lax.ragged_dot(xs, w_gate, sizes)) * lax.ragged_dot(xs, w_up, sizes)
    out = lax.ragged_dot(act, w_down, sizes) * flat_w[order][:, None].astype(xs.dtype)
    y = jnp.zeros((n_tok, D), out.dtype).at[tok].add(out)
    return y.reshape(B, T, D).astype(h.dtype)


def setup_inputs(seed: int = 0) -> dict:
    key = jax.random.key(seed)
    ks = jax.random.split(key, 24)
    f32 = jnp.float32

    def dense(k, fan_in, shape):
        return jax.random.normal(k, shape, f32) * (fan_in ** -0.5)

    def gain(k, shape):
        return 1.0 + 0.02 * jax.random.normal(k, shape, f32)

    x = jax.random.normal(ks[0], (BATCH, SEQ, D_MODEL), f32)
    offset = jax.random.randint(ks[1], (BATCH, 1), 0, 64, dtype=jnp.int32) * CHUNK
    positions = (offset + jnp.arange(SEQ, dtype=jnp.int32)[None, :]).astype(jnp.int32)
    dt = jnp.exp(jax.random.uniform(ks[9], (DEPTH, B_V_HEADS), f32) * (np.log(0.1) - np.log(1e-3)) + np.log(1e-3))
    return {
        "x": x,
        "positions": positions,
        "attn_norm": gain(ks[2], (DEPTH, D_MODEL)),
        "w_in": dense(ks[3], D_MODEL, (DEPTH, D_MODEL, D_IN)),
        "q_norm": gain(ks[4], (DEPTH, A_HEAD_DIM)),
        "k_norm": gain(ks[5], (DEPTH, A_HEAD_DIM)),
        "idx_k_norm": gain(ks[6], (DEPTH, IDX_DIM)),
        "conv_w": dense(ks[7], CONV_WIDTH, (DEPTH, CONV_WIDTH, CONV_CH)),
        "a_log": jnp.log(jax.random.uniform(ks[8], (DEPTH, B_V_HEADS), f32, 1.0, 16.0)),
        "dt_bias": dt + jnp.log(-jnp.expm1(-dt)),
        "gdn_out_norm": gain(ks[10], (DEPTH, B_HEAD_DIM)),
        "w_o_a": dense(ks[11], A_W, (DEPTH, A_W, D_MODEL)),
        "w_o_b": dense(ks[12], B_V, (DEPTH, B_V, D_MODEL)),
        "w_out": dense(ks[13], D_MODEL, (DEPTH, D_MODEL, D_MODEL)),
        "ffn_norm": gain(ks[14], (DEPTH, D_MODEL)),
        "w_router_group": dense(ks[15], D_MODEL, (DEPTH, D_MODEL, N_GROUPS)),
        "b_router_group": 0.01 * jax.random.normal(ks[16], (DEPTH, N_GROUPS), f32),
        "w_router_expert": dense(ks[17], D_MODEL, (DEPTH, D_MODEL, N_EXPERTS)),
        "b_router_expert": 0.01 * jax.random.normal(ks[18], (DEPTH, N_EXPERTS), f32),
        "w_gate": dense(ks[19], D_MODEL, (DEPTH, N_EXPERTS, D_MODEL, D_FF_EXPERT)),
        "w_up": dense(ks[20], D_MODEL, (DEPTH, N_EXPERTS, D_MODEL, D_FF_EXPERT)),
        "w_down": dense(ks[21], D_FF_EXPERT, (DEPTH, N_EXPERTS, D_FF_EXPERT, D_MODEL)),
    }


def reference(x, positions, attn_norm, w_in, q_norm, k_norm, idx_k_norm, conv_w, a_log, dt_bias,
              gdn_out_norm, w_o_a, w_o_b, w_out, ffn_norm, w_router_group, b_router_group,
              w_router_expert, b_router_expert, w_gate, w_up, w_down):
    for layer in range(DEPTH):
        h = rms_norm(x, attn_norm[layer])
        proj = jnp.einsum('btd,de->bte', h, w_in[layer])
        (a_q, a_k, a_v, a_iq, a_ik, a_iw, b_q, b_k, b_v, b_z, b_b, b_a,
         gate_a, gate_b) = split_columns(proj, SPLIT_SIZES)
        out_a = dsa_branch(a_q, a_k, a_v, a_iq, a_ik, a_iw, positions,
                           q_norm[layer], k_norm[layer], idx_k_norm[layer])
        out_b = gdn_branch(b_q, b_k, b_v, b_z, b_b, b_a, conv_w[layer], a_log[layer],
                           dt_bias[layer], gdn_out_norm[layer])
        mix = (jax.nn.sigmoid(gate_a) * (out_a @ w_o_a[layer])
               + jax.nn.sigmoid(gate_b) * (out_b @ w_o_b[layer]))
        x = x + mix @ w_out[layer]
        h2 = rms_norm(x, ffn_norm[layer])
        x = x + hier_moe(h2, w_router_group[layer], b_router_group[layer], w_router_expert[layer],
                         b_router_expert[layer], w_gate[layer], w_up[layer], w_down[layer])
    return x
```

```python
import functools

import jax
import jax.numpy as jnp
from jax import lax
from jax.experimental import pallas as pl
from jax.experimental.pallas import tpu as pltpu

F32 = jnp.float32
BF16 = jnp.bfloat16
I32 = jnp.int32

EPS = 1e-6
NEG = -1e30
BIG = 1e30
ROPE_THETA = 500000.0
CHUNK = 64
A_HEADS = 16
HEAD_DIM = 128
IDX_HEADS = 16
IDX_DIM = 64
TOPK_MAX = 256
Q_BLOCK = 128
B_QK_HEADS = 16
B_V_HEADS = 32
CONV_WIDTH = 4
N_GROUPS = 8
EXPERTS_PER_GROUP = 16
N_EXPERTS = 128
D_FF = 768
LANES = 128
VMEM_LIMIT = 56 * 1024 * 1024

NT_DIMS = (((1,), (1,)), ((), ()))


def _cp(*sem):
    return pltpu.CompilerParams(dimension_semantics=sem, vmem_limit_bytes=VMEM_LIMIT)


def _dot(a, b):
    return jnp.dot(a, b, preferred_element_type=F32)


def _dot_nt(a, b):
    return lax.dot_general(a, b, NT_DIMS, preferred_element_type=F32)


def _dot_hi(a, b):
    return jnp.dot(a, b, preferred_element_type=F32, precision=lax.Precision.HIGHEST)


def _sigmoid(x):
    return 1.0 / (1.0 + jnp.exp(-x))


def _softplus(x):
    return jnp.maximum(x, 0.0) + jnp.log(1.0 + jnp.exp(-jnp.abs(x)))


def _rmsnorm_kernel(x_ref, g_ref, o_ref):
    x = x_ref[...]
    ms = jnp.mean(x * x, axis=-1, keepdims=True)
    o_ref[...] = (x * lax.rsqrt(ms + EPS) * g_ref[...]).astype(o_ref.dtype)


def _rmsnorm(x, gain, out_dtype=BF16):
    t, d = x.shape
    tm = min(t, 512)
    return pl.pallas_call(
        _rmsnorm_kernel,
        grid=(t // tm,),
        in_specs=[pl.BlockSpec((tm, d), lambda i: (i, 0)), pl.BlockSpec((1, d), lambda i: (0, 0))],
        out_specs=pl.BlockSpec((tm, d), lambda i: (i, 0)),
        out_shape=jax.ShapeDtypeStruct((t, d), out_dtype),
        compiler_params=_cp("parallel"),
    )(x, gain.reshape(1, d))


def _mm_kernel(*refs, n_extra, epilogue):
    a_ref, b_ref = refs[:2]
    extra = refs[2:2 + n_extra]
    outs = refs[2 + n_extra:]
    acc = _dot(a_ref[...], b_ref[...])
    res = epilogue(acc, *[e[...] for e in extra]) if epilogue is not None else (acc,)
    for o_ref, r in zip(outs, res):
        o_ref[...] = r.astype(o_ref.dtype)


def _matmul(a, b, out_dtypes, tm, tn, epilogue=None, extras=()):
    m, k = a.shape
    n = b.shape[1]
    tm, tn = min(tm, m), min(tn, n)
    in_specs = [pl.BlockSpec((tm, k), lambda i, j: (i, 0)), pl.BlockSpec((k, tn), lambda i, j: (0, j))]
    args = [a, b]
    for arr, kind in extras:
        if kind == "tile":
            in_specs.append(pl.BlockSpec((tm, tn), lambda i, j: (i, j)))
        else:
            in_specs.append(pl.BlockSpec((1, tn), lambda i, j: (0, j)))
        args.append(arr)
    out = pl.pallas_call(
        functools.partial(_mm_kernel, n_extra=len(extras), epilogue=epilogue),
        grid=(m // tm, n // tn),
        in_specs=in_specs,
        out_specs=[pl.BlockSpec((tm, tn), lambda i, j: (i, j)) for _ in out_dtypes],
        out_shape=[jax.ShapeDtypeStruct((m, n), dt) for dt in out_dtypes],
        compiler_params=_cp("parallel", "arbitrary"),
    )(*args)
    return out


def _mm_nt_kernel(a_ref, b_ref, o_ref):
    res = _dot_nt(a_ref[...], b_ref[...])
    for c in range(o_ref.shape[0]):
        o_ref[c] = res[:, c * CHUNK:(c + 1) * CHUNK]


def _matmul_nt(a, b, tm):
    n, k = a.shape
    m = b.shape[0]
    tm = min(tm, m)
    return pl.pallas_call(
        _mm_nt_kernel,
        grid=(m // tm,),
        in_specs=[pl.BlockSpec((n, k), lambda i: (0, 0)), pl.BlockSpec((tm, k), lambda i: (i, 0))],
        out_specs=pl.BlockSpec((tm // CHUNK, n, CHUNK), lambda i: (i, 0, 0)),
        out_shape=jax.ShapeDtypeStruct((m // CHUNK, n, CHUNK), F32),
        compiler_params=_cp("parallel"),
    )(a, b)


def _dsa_prep_kernel(p1_ref, p3_ref, ps_ref, pos_ref, ifa_ref, ifi_ref, qn_ref, kn_ref, ikn_ref,
                     q_ref, k_ref, qi_ref, ki_ref, wi_ref):
    tm = p1_ref.shape[0]
    pos = pos_ref[...].astype(F32)
    lane = lax.broadcasted_iota(I32, (tm, LANES), 1)
    ang_a = pos * ifa_ref[...]
    cos_a = jnp.cos(ang_a)
    sin_a = jnp.sin(ang_a)
    sin_a = jnp.where(lane < 16, -sin_a, sin_a)
    low_a = lane < 16

    def rope_a(x):
        partner = jnp.where(low_a, pltpu.roll(x, LANES - 16, 1), pltpu.roll(x, 16, 1))
        return x * cos_a + partner * sin_a

    d64 = lane & 63
    ang_i = pos * ifi_ref[...]
    cos_i = jnp.cos(ang_i)
    sin_i = jnp.sin(ang_i)
    low_i = d64 < 8
    sin_i = jnp.where(low_i, -sin_i, sin_i)

    def rope_i(x):
        partner = jnp.where(low_i, pltpu.roll(x, LANES - 8, 1), pltpu.roll(x, 8, 1))
        return x * cos_i + partner * sin_i

    qn = qn_ref[...]
    kn = kn_ref[...]
    scale = HEAD_DIM ** -0.5
    for h in range(A_HEADS):
        sl = slice(h * HEAD_DIM, (h + 1) * HEAD_DIM)
        xq = p1_ref[:, sl]
        yq = xq * lax.rsqrt(jnp.mean(xq * xq, axis=-1, keepdims=True) + EPS) * qn
        q_ref[:, sl] = (rope_a(yq) * scale).astype(q_ref.dtype)
        xk = p1_ref[:, A_HEADS * HEAD_DIM + h * HEAD_DIM:A_HEADS * HEAD_DIM + (h + 1) * HEAD_DIM]
        yk = xk * lax.rsqrt(jnp.mean(xk * xk, axis=-1, keepdims=True) + EPS) * kn
        k_ref[:, sl] = rope_a(yk).astype(k_ref.dtype)
    for j in range(IDX_HEADS * IDX_DIM // LANES):
        sl = slice(j * LANES, (j + 1) * LANES)
        qi_ref[:, sl] = rope_i(p3_ref[:, sl]).astype(qi_ref.dtype)
    xk = ps_ref[:, 0:LANES]
    ms = jnp.sum(xk * xk, axis=-1, keepdims=True) * (1.0 / IDX_DIM)
    yk = xk * lax.rsqrt(ms + EPS) * ikn_ref[...]
    ki_ref[...] = rope_i(yk)[:, :IDX_DIM].astype(ki_ref.dtype)
    wi_ref[...] = ps_ref[:, LANES:LANES + IDX_HEADS] * (IDX_HEADS ** -0.5 * IDX_DIM ** -0.5)


def _dsa_prep(p1, p3, ps, positions, ifa, ifi, qn, kn, ikn):
    t = p1.shape[0]
    tm = min(t, 256)
    aw = A_HEADS * HEAD_DIM
    iq = IDX_HEADS * IDX_DIM
    row = lambda i: (i, 0)
    fix = lambda i: (0, 0)
    return pl.pallas_call(
        _dsa_prep_kernel,
        grid=(t // tm,),
        in_specs=[
            pl.BlockSpec((tm, 2 * aw), row), pl.BlockSpec((tm, iq), row), pl.BlockSpec((tm, ps.shape[1]), row),
            pl.BlockSpec((tm, 1), row), pl.BlockSpec((1, LANES), fix), pl.BlockSpec((1, LANES), fix),
            pl.BlockSpec((1, LANES), fix), pl.BlockSpec((1, LANES), fix), pl.BlockSpec((1, LANES), fix),
        ],
        out_specs=[
            pl.BlockSpec((tm, aw), row), pl.BlockSpec((tm, aw), row), pl.BlockSpec((tm, iq), row),
            pl.BlockSpec((tm, IDX_DIM), row), pl.BlockSpec((tm, IDX_HEADS), row),
        ],
        out_shape=[
            jax.ShapeDtypeStruct((t, aw), BF16), jax.ShapeDtypeStruct((t, aw), BF16),
            jax.ShapeDtypeStruct((t, iq), BF16), jax.ShapeDtypeStruct((t, IDX_DIM), BF16),
            jax.ShapeDtypeStruct((t, IDX_HEADS), F32),
        ],
        compiler_params=_cp("parallel"),
    )(p1, p3, ps, positions, ifa, ifi, qn, kn, ikn)


def _indexer_kernel(qi_ref, w_ref, ki_ref, bias_ref, s_ref, wb_ref, *, kc, topk, maxit):
    i = pl.program_id(0)
    qb = Q_BLOCK
    n_ch = ((i + 1) * qb + kc - 1) // kc
    nslab = kc // LANES
    for h in range(IDX_HEADS):
        wb_ref[h] = jnp.broadcast_to(w_ref[:, h:h + 1], (qb, LANES))
    row = lax.broadcasted_iota(I32, (qb, LANES), 0)
    lane = lax.broadcasted_iota(I32, (qb, LANES), 1)
    limit = jnp.where(row < CHUNK, i * qb + CHUNK, (i + 1) * qb)

    def score_body(c, carry):
        mn, mx = carry
        off = pl.multiple_of(c * kc, kc)
        kblk = ki_ref[pl.ds(off, kc), :]
        accs = [jnp.zeros((qb, LANES), F32) for _ in range(nslab)]
        for h in range(IDX_HEADS):
            s = _dot_nt(qi_ref[:, h * IDX_DIM:(h + 1) * IDX_DIM], kblk)
            wbh = wb_ref[h]
            for j in range(nslab):
                accs[j] = accs[j] + wbh * jnp.maximum(s[:, j * LANES:(j + 1) * LANES], 0.0)
        for j in range(nslab):
            adm = (lane + (off + j * LANES)) < limit
            val = jnp.where(adm, accs[j], NEG)
            s_ref[:, pl.ds(pl.multiple_of(off + j * LANES, LANES), LANES)] = val
            mn = jnp.minimum(mn, jnp.where(adm, accs[j], BIG))
            mx = jnp.maximum(mx, val)
        return mn, mx

    mn, mx = lax.fori_loop(0, n_ch, score_body,
                           (jnp.full((qb, LANES), BIG, F32), jnp.full((qb, LANES), NEG, F32)))
    lo0 = jnp.min(mn, axis=-1, keepdims=True)
    hi0 = jnp.max(mx, axis=-1, keepdims=True)
    kf = float(topk)

    def count_ge(thr):
        thr_b = jnp.broadcast_to(thr, (qb, LANES))

        def body(c, acc):
            off = pl.multiple_of(c * kc, kc)
            for j in range(nslab):
                sc = s_ref[:, pl.ds(pl.multiple_of(off + j * LANES, LANES), LANES)]
                acc = acc + jnp.where(sc >= thr_b, 1.0, 0.0)
            return acc

        acc = lax.fori_loop(0, n_ch, body, jnp.zeros((qb, LANES), F32))
        return jnp.sum(acc, axis=-1, keepdims=True)

    done0 = jnp.where(limit[:, 0:1] <= topk, 1.0, 0.0)

    def cond(st):
        it, _, _, done = st
        return jnp.logical_and(it < maxit, jnp.min(done) < 0.5)

    def body(st):
        it, lo, hi, done = st
        mid = lo + (hi - lo) * 0.5
        c = count_ge(mid)
        ge = c >= kf
        lo = jnp.where(ge, mid, lo)
        hi = jnp.where(ge, hi, mid)
        done = jnp.maximum(done, jnp.where(c == kf, 1.0, 0.0))
        return it + 1, lo, hi, done

    _, lo, hi, _ = lax.while_loop(cond, body, (jnp.int32(0), lo0, hi0, done0))
    c_hi = count_ge(hi)
    lo = jnp.where(c_hi >= kf, hi, lo)
    lo_b = jnp.broadcast_to(lo, (qb, LANES))

    bias_ref[...] = jnp.full(bias_ref.shape, NEG, bias_ref.dtype)

    def write_body(c, _):
        off = pl.multiple_of(c * kc, kc)
        for j in range(nslab):
            o = pl.multiple_of(off + j * LANES, LANES)
            sc = s_ref[:, pl.ds(o, LANES)]
            bias_ref[:, pl.ds(o, LANES)] = jnp.where(sc >= lo_b, 0.0, NEG).astype(bias_ref.dtype)
        return 0

    lax.fori_loop(0, n_ch, write_body, 0)


def _indexer(q_idx, w_idx, k_idx, topk):
    t = q_idx.shape[0]
    kc = min(t, 256)
    return pl.pallas_call(
        functools.partial(_indexer_kernel, kc=kc, topk=topk, maxit=40),
        grid=(t // Q_BLOCK,),
        in_specs=[
            pl.BlockSpec((Q_BLOCK, IDX_HEADS * IDX_DIM), lambda i: (i, 0)),
            pl.BlockSpec((Q_BLOCK, IDX_HEADS), lambda i: (i, 0)),
            pl.BlockSpec((t, IDX_DIM), lambda i: (0, 0)),
        ],
        out_specs=pl.BlockSpec((Q_BLOCK, t), lambda i: (i, 0)),
        out_shape=jax.ShapeDtypeStruct((t, t), BF16),
        scratch_shapes=[pltpu.VMEM((Q_BLOCK, t), F32), pltpu.VMEM((IDX_HEADS, Q_BLOCK, LANES), F32)],
        compiler_params=_cp("parallel"),
    )(q_idx, w_idx, k_idx)


def _attn_kernel(q_ref, k_ref, v_ref, b_ref, o_ref, acc_ref, m_ref, l_ref):
    qi = pl.program_id(0)
    ki = pl.program_id(1)

    @pl.when(ki == 0)
    def _():
        acc_ref[...] = jnp.zeros_like(acc_ref)
        m_ref[...] = jnp.full(m_ref.shape, NEG, F32)
        l_ref[...] = jnp.zeros_like(l_ref)

    @pl.when(ki <= qi)
    def _():
        bias = b_ref[...].astype(F32)
        for h in range(A_HEADS):
            sl = slice(h * HEAD_DIM, (h + 1) * HEAD_DIM)
            s = _dot_nt(q_ref[:, sl], k_ref[:, sl]) + bias
            m_prev = m_ref[h]
            m_new = jnp.maximum(m_prev, jnp.max(s, axis=-1, keepdims=True))
            alpha = jnp.exp(m_prev - m_new)
            p = jnp.exp(s - m_new[:, 0:1])
            l_ref[h] = alpha * l_ref[h] + jnp.sum(p, axis=-1, keepdims=True)
            acc_ref[:, sl] = alpha * acc_ref[:, sl] + _dot(p.astype(BF16), v_ref[:, sl])
            m_ref[h] = m_new

    @pl.when(ki == qi)
    def _():
        for h in range(A_HEADS):
            sl = slice(h * HEAD_DIM, (h + 1) * HEAD_DIM)
            o_ref[:, sl] = (acc_ref[:, sl] / l_ref[h]).astype(o_ref.dtype)


def _attention(q, k, v, bias):
    t, aw = q.shape
    bq = min(t, 512)
    nb = t // bq
    kv_map = lambda i, j: (jnp.minimum(i, j), 0)
    return pl.pallas_call(
        _attn_kernel,
        grid=(nb, nb),
        in_specs=[
            pl.BlockSpec((bq, aw), lambda i, j: (i, 0)),
            pl.BlockSpec((bq, aw), kv_map),
            pl.BlockSpec((bq, aw), kv_map),
            pl.BlockSpec((bq, bq), lambda i, j: (i, jnp.minimum(i, j))),
        ],
        out_specs=pl.BlockSpec((bq, aw), lambda i, j: (i, 0)),
        out_shape=jax.ShapeDtypeStruct((t, aw), BF16),
        scratch_shapes=[pltpu.VMEM((bq, aw), F32), pltpu.VMEM((A_HEADS, bq, LANES), F32),
                        pltpu.VMEM((A_HEADS, bq, LANES), F32)],
        compiler_params=_cp("parallel", "arbitrary"),
    )(q, k, v, bias)


def _gdn_prep_kernel(cur_ref, prev_ref, w_ref, o_ref, *, qk_blocks):
    i = pl.program_id(0)
    j = pl.program_id(1)
    tm = cur_ref.shape[0]
    cur = cur_ref[...]
    prev = prev_ref[...] * jnp.where(i > 0, 1.0, 0.0)
    xcat = jnp.concatenate([prev, cur], axis=0)
    y = cur * w_ref[CONV_WIDTH - 1:CONV_WIDTH, :]
    for tap in range(CONV_WIDTH - 1):
        y = y + xcat[8 - (CONV_WIDTH - 1) + tap:8 - (CONV_WIDTH - 1) + tap + tm, :] * w_ref[tap:tap + 1, :]
    y = y * _sigmoid(y)
    is_qk = j < 2 * qk_blocks
    qscale = jnp.where(j < qk_blocks, HEAD_DIM ** -0.5, 1.0)
    for h in range(cur.shape[1] // HEAD_DIM):
        yh = y[:, h * HEAD_DIM:(h + 1) * HEAD_DIM]
        nrm = lax.rsqrt(jnp.sum(yh * yh, axis=-1, keepdims=True) + EPS) * qscale
        o_ref[h] = jnp.where(is_qk, yh * nrm, yh)


def _gdn_prep(p4, conv_w):
    t, ch = p4.shape
    tm = min(t, 256)
    cb = 1024
    hb = cb // HEAD_DIM
    qk_blocks = B_QK_HEADS * HEAD_DIM // cb
    return pl.pallas_call(
        functools.partial(_gdn_prep_kernel, qk_blocks=qk_blocks),
        grid=(t // tm, ch // cb),
        in_specs=[
            pl.BlockSpec((tm, cb), lambda i, j: (i, j)),
            pl.BlockSpec((8, cb), lambda i, j: (jnp.maximum(i * (tm // 8) - 1, 0), j)),
            pl.BlockSpec((CONV_WIDTH, cb), lambda i, j: (0, j)),
        ],
        out_specs=pl.BlockSpec((hb, tm, HEAD_DIM), lambda i, j: (j, i, 0)),
        out_shape=jax.ShapeDtypeStruct((ch // HEAD_DIM, t, HEAD_DIM), F32),
        compiler_params=_cp("parallel", "parallel"),
    )(p4, p4, conv_w)


def _tri_inverse(a, row, col, eye):
    rb16, cb16 = row // 16, col // 16
    rb32, cb32 = row // 32, col // 32
    blk16 = rb16 == cb16
    d1 = jnp.where(blk16, a, 0.0)
    d2 = _dot_hi(d1, d1)
    d4 = _dot_hi(d2, d2)
    d8 = _dot_hi(d4, d4)
    x = _dot_hi(_dot_hi(_dot_hi(eye - d1, eye + d2), eye + d4), eye + d8)
    l1 = jnp.where(jnp.logical_and(rb32 == cb32, jnp.logical_not(blk16)), a, 0.0)
    x = x - _dot_hi(_dot_hi(x, l1), x)
    l2 = jnp.where(rb32 != cb32, a, 0.0)
    x = x - _dot_hi(_dot_hi(x, l2), x)
    return x


def _gdn_a_kernel(qkv_ref, ps_ref, pst_ref, alr_ref, alc_ref, dtr_ref, dtc_ref,
                  wq_ref, u_ref, ak_ref, egl_ref):
    c = CHUNK
    row = lax.broadcasted_iota(I32, (c, c), 0)
    col = lax.broadcasted_iota(I32, (c, c), 1)
    incl = row >= col
    strict = row > col
    eye = jnp.where(row == col, 1.0, 0.0)
    lower = jnp.where(incl, 1.0, 0.0)
    upper = jnp.where(row <= col, 1.0, 0.0)

    beta_col = _sigmoid(ps_ref[:, 2 * LANES:2 * LANES + B_V_HEADS])
    g_col = -jnp.exp(alr_ref[...]) * _softplus(ps_ref[:, 3 * LANES:3 * LANES + B_V_HEADS] + dtr_ref[...])
    g_row = -jnp.exp(alc_ref[...]) * _softplus(pst_ref[0] + dtc_ref[...])
    gc_col = _dot_hi(lower, g_col)
    gc_row = _dot_hi(g_row, upper)
    egl_ref[0] = jnp.exp(gc_col[c - 1:c, :])

    for hq in range(B_QK_HEADS):
        q = qkv_ref[hq]
        k = qkv_ref[B_QK_HEADS + hq]
        kbf = k.astype(BF16)
        g = _dot_nt(jnp.concatenate([q.astype(BF16), kbf], axis=0), kbf)
        qk, kk = g[:c], g[c:]
        for r in range(B_V_HEADS // B_QK_HEADS):
            h = hq * (B_V_HEADS // B_QK_HEADS) + r
            v = qkv_ref[2 * B_QK_HEADS + h]
            bc = beta_col[:, h:h + 1]
            gcc = gc_col[:, h:h + 1]
            gcr = gc_row[h:h + 1, :]
            glast = gc_col[c - 1:c, h:h + 1]
            decay = jnp.exp(jnp.where(incl, gcc - gcr, NEG))
            a_mat = jnp.where(strict, bc * kk * decay, 0.0)
            attn = qk * decay
            t_mat = _tri_inverse(a_mat, row, col, eye)
            rhs = jnp.concatenate([v * bc, k * (bc * jnp.exp(gcc))], axis=1)
            uw = _dot(t_mat.astype(BF16), rhs.astype(BF16))
            u_ref[h] = uw[:, :HEAD_DIM]
            qe = q * jnp.exp(gcc)
            wq_ref[h] = jnp.concatenate([uw[:, HEAD_DIM:], qe], axis=0).astype(wq_ref.dtype)
            kd = k * jnp.exp(glast - gcc)
            ak_ref[h] = jnp.concatenate([attn, kd.T], axis=0).astype(ak_ref.dtype)


def _gdn_phase_a(qkv, ps, pst, a_log, dt_bias):
    nh, t, _ = qkv.shape
    n = t // CHUNK
    hv = B_V_HEADS
    fix2 = lambda i: (0, 0)
    return pl.pallas_call(
        _gdn_a_kernel,
        grid=(n,),
        in_specs=[
            pl.BlockSpec((nh, CHUNK, HEAD_DIM), lambda i: (0, i, 0)),
            pl.BlockSpec((CHUNK, ps.shape[1]), lambda i: (i, 0)),
            pl.BlockSpec((1, hv, CHUNK), lambda i: (i, 0, 0)),
            pl.BlockSpec((1, hv), fix2), pl.BlockSpec((hv, 1), fix2),
            pl.BlockSpec((1, hv), fix2), pl.BlockSpec((hv, 1), fix2),
        ],
        out_specs=[
            pl.BlockSpec((hv, 2 * CHUNK, HEAD_DIM), lambda i: (0, i, 0)),
            pl.BlockSpec((hv, CHUNK, HEAD_DIM), lambda i: (0, i, 0)),
            pl.BlockSpec((hv, CHUNK + HEAD_DIM, CHUNK), lambda i: (0, i, 0)),
            pl.BlockSpec((1, 1, hv), lambda i: (i, 0, 0)),
        ],
        out_shape=[
            jax.ShapeDtypeStruct((hv, 2 * t, HEAD_DIM), BF16),
            jax.ShapeDtypeStruct((hv, t, HEAD_DIM), F32),
            jax.ShapeDtypeStruct((hv, n * (CHUNK + HEAD_DIM), CHUNK), BF16),
            jax.ShapeDtypeStruct((n, 1, hv), F32),
        ],
        compiler_params=_cp("parallel"),
    )(qkv, ps, pst, a_log.reshape(1, hv), a_log.reshape(hv, 1), dt_bias.reshape(1, hv), dt_bias.reshape(hv, 1))


def _gdn_b_kernel(wq_ref, u_ref, ak_ref, egl_ref, z_ref, gn_ref, o_ref, s_ref):
    c = CHUNK

    @pl.when(pl.program_id(0) == 0)
    def _():
        s_ref[...] = jnp.zeros_like(s_ref)

    gn = gn_ref[...]
    for h in range(B_V_HEADS):
        sl = slice(h * HEAD_DIM, (h + 1) * HEAD_DIM)
        s = s_ref[h]
        r1 = _dot(wq_ref[h], s.astype(BF16))
        vn = u_ref[h] - r1[:c]
        r2 = _dot(ak_ref[h], vn.astype(BF16))
        o = r1[c:] + r2[:c]
        s_ref[h] = s * egl_ref[0, :, h:h + 1] + r2[c:]
        on = o * lax.rsqrt(jnp.mean(o * o, axis=-1, keepdims=True) + EPS) * gn
        z = z_ref[:, sl]
        o_ref[:, sl] = (on * (z * _sigmoid(z))).astype(o_ref.dtype)


def _gdn_phase_b(wq, u, ak, egl, z, out_norm):
    hv, t, _ = u.shape
    n = t // CHUNK
    return pl.pallas_call(
        _gdn_b_kernel,
        grid=(n,),
        in_specs=[
            pl.BlockSpec((hv, 2 * CHUNK, HEAD_DIM), lambda i: (0, i, 0)),
            pl.BlockSpec((hv, CHUNK, HEAD_DIM), lambda i: (0, i, 0)),
            pl.BlockSpec((hv, CHUNK + HEAD_DIM, CHUNK), lambda i: (0, i, 0)),
            pl.BlockSpec((1, 1, hv), lambda i: (i, 0, 0)),
            pl.BlockSpec((CHUNK, hv * HEAD_DIM), lambda i: (i, 0)),
            pl.BlockSpec((1, HEAD_DIM), lambda i: (0, 0)),
        ],
        out_specs=pl.BlockSpec((CHUNK, hv * HEAD_DIM), lambda i: (i, 0)),
        out_shape=jax.ShapeDtypeStruct((t, hv * HEAD_DIM), BF16),
        scratch_shapes=[pltpu.VMEM((hv, HEAD_DIM, HEAD_DIM), F32)],
        compiler_params=_cp("arbitrary"),
    )(wq, u, ak, egl, z, out_norm.reshape(1, HEAD_DIM))


_A_W = A_HEADS * HEAD_DIM
_A_IQ = IDX_HEADS * IDX_DIM
_B_QK = B_QK_HEADS * HEAD_DIM
_B_V = B_V_HEADS * HEAD_DIM
_OFF_AIK = 3 * _A_W + _A_IQ
_OFF_AIW = _OFF_AIK + IDX_DIM
_OFF_BQ = _OFF_AIW + IDX_HEADS
_OFF_BZ = _OFF_BQ + 2 * _B_QK + _B_V
_OFF_BB = _OFF_BZ + _B_V
_OFF_BA = _OFF_BB + B_V_HEADS
_OFF_GATE = _OFF_BA + B_V_HEADS


def _small_weights(w_in):
    d = w_in.shape[0]
    pad = lambda a: jnp.pad(a, ((0, 0), (0, LANES - a.shape[1])))
    return jnp.concatenate([
        pad(w_in[:, _OFF_AIK:_OFF_AIK + IDX_DIM]), pad(w_in[:, _OFF_AIW:_OFF_AIW + IDX_HEADS]),
        pad(w_in[:, _OFF_BB:_OFF_BB + B_V_HEADS]), pad(w_in[:, _OFF_BA:_OFF_BA + B_V_HEADS]),
    ], axis=1).astype(BF16)


def _rope_freqs(rot, width):
    half = rot // 2
    inv = jnp.power(ROPE_THETA, -jnp.arange(half, dtype=F32) * 2.0 / rot)
    pat = jnp.concatenate([inv, inv, jnp.zeros((width - rot,), F32)])
    return jnp.tile(pat, LANES // width).reshape(1, LANES)


def _dsa_branch(h, w_in, ps, positions, q_norm, k_norm, idx_k_norm):
    t = h.shape[0]
    tm = 1024
    (p1,) = _matmul(h, w_in[:, 0:2 * _A_W].astype(BF16), [F32], tm, 1024)
    (v,) = _matmul(h, w_in[:, 2 * _A_W:3 * _A_W].astype(BF16), [BF16], tm, 1024)
    (p3,) = _matmul(h, w_in[:, 3 * _A_W:3 * _A_W + _A_IQ].astype(BF16), [F32], tm, 1024)
    ikn = jnp.pad(idx_k_norm.reshape(1, IDX_DIM), ((0, 0), (0, LANES - IDX_DIM)))
    q, k, q_idx, k_idx, w_idx = _dsa_prep(
        p1, p3, ps, positions.reshape(t, 1), _rope_freqs(HEAD_DIM // 4, HEAD_DIM), _rope_freqs(IDX_DIM // 4, IDX_DIM),
        q_norm.reshape(1, HEAD_DIM), k_norm.reshape(1, HEAD_DIM), ikn)
    bias = _indexer(q_idx, w_idx, k_idx, min(TOPK_MAX, t // 4))
    return _attention(q, k, v, bias)


def _gdn_branch(h, w_in, ps, pst, conv_w, a_log, dt_bias, out_norm):
    tm = 1024
    (p4,) = _matmul(h, w_in[:, _OFF_BQ:_OFF_BZ].astype(BF16), [F32], tm, 1024)
    (z,) = _matmul(h, w_in[:, _OFF_BZ:_OFF_BB].astype(BF16), [F32], tm, 1024)
    qkv = _gdn_prep(p4, conv_w)
    wq, u, ak, egl = _gdn_phase_a(qkv, ps, pst, a_log, dt_bias)
    return _gdn_phase_b(wq, u, ak, egl, z, out_norm)


def _gate_a_epilogue(acc, g):
    return (_sigmoid(g) * acc,)


def _gate_b_epilogue(acc, g, m):
    return (m + _sigmoid(g) * acc,)


def _residual_norm_epilogue(acc, x, gain):
    x1 = x + acc
    h2 = x1 * lax.rsqrt(jnp.mean(x1 * x1, axis=-1, keepdims=True) + EPS) * gain
    return x1, h2


def _mix(x, h, w_in, out_a, out_b, w_o_a, w_o_b, w_out, ffn_norm):
    d = x.shape[1]
    (ga,) = _matmul(h, w_in[:, _OFF_GATE:_OFF_GATE + d].astype(BF16), [F32], 1024, 1024)
    (gb,) = _matmul(h, w_in[:, _OFF_GATE + d:_OFF_GATE + 2 * d].astype(BF16), [F32], 1024, 1024)
    (mixa,) = _matmul(out_a, w_o_a.astype(BF16), [F32], 1024, 1024, _gate_a_epilogue, [(ga, "tile")])
    (mix,) = _matmul(out_b, w_o_b.astype(BF16), [BF16], 1024, 512, _gate_b_epilogue, [(gb, "tile"), (mixa, "tile")])
    return _matmul(mix, w_out.astype(BF16), [F32, F32], 512, d, _residual_norm_epilogue,
                   [(x, "tile"), (ffn_norm.reshape(1, d), "row")])


MOE_TM = 256
MOE_NF = 2
_R_E1, _R_E2, _R_RANK1, _R_RANK2, _R_W1, _R_W2 = 0, 1, 2, 3, 4, 5


def _router_kernel(h_ref, w_ref, b_ref, r_ref, cnt_ref, carry_ref):
    i = pl.program_id(0)

    @pl.when(i == 0)
    def _():
        carry_ref[...] = jnp.zeros_like(carry_ref)

    tb = h_ref.shape[0]
    logits = _dot(h_ref[...].astype(BF16), w_ref[...]) + b_ref[...]
    le = logits[:, :N_EXPERTS]
    lg = logits[:, N_EXPERTS:]
    lane = lax.broadcasted_iota(I32, (tb, LANES), 1)
    lanef = lane.astype(F32)
    far = float(4 * LANES)

    lgm = jnp.where(lane < N_GROUPS, lg, NEG)
    gmax = jnp.max(lgm, axis=-1, keepdims=True)
    g_idx = jnp.min(jnp.where(lgm == gmax, lanef, far), axis=-1, keepdims=True)
    g_top = 1.0 / jnp.sum(jnp.exp(lgm - gmax), axis=-1, keepdims=True)

    in_grp = (lane // EXPERTS_PER_GROUP).astype(F32) == g_idx
    lem = jnp.where(in_grp, le, NEG)
    e1 = jnp.max(lem, axis=-1, keepdims=True)
    i1 = jnp.min(jnp.where(lem == e1, lanef, far), axis=-1, keepdims=True)
    lem2 = jnp.where(lanef == i1, NEG, lem)
    e2 = jnp.max(lem2, axis=-1, keepdims=True)
    i2 = jnp.min(jnp.where(lem2 == e2, lanef, far), axis=-1, keepdims=True)
    se = jnp.sum(jnp.exp(lem - e1), axis=-1, keepdims=True)
    p1 = 1.0 / se
    p2 = jnp.exp(e2 - e1) / se
    w1 = g_top * p1 / (p1 + p2)
    w2 = g_top * p2 / (p1 + p2)

    o1 = jnp.where(lanef == i1, 1.0, 0.0)
    o2 = jnp.where(lanef == i2, 1.0, 0.0)
    osum = o1 + o2
    rr = lax.broadcasted_iota(I32, (tb, tb), 0)
    cc = lax.broadcasted_iota(I32, (tb, tb), 1)
    before = jnp.where(cc < rr, 1.0, 0.0).astype(BF16)
    prefix = _dot(before, osum.astype(BF16)) + carry_ref[0:1, :]
    rank1 = jnp.sum(prefix * o1, axis=-1, keepdims=True)
    rank2 = jnp.sum(prefix * o2, axis=-1, keepdims=True)
    carry_ref[...] = carry_ref[...] + jnp.sum(osum, axis=0, keepdims=True)
    cnt_ref[...] = carry_ref[...]

    rec = jnp.zeros((tb, LANES), F32)
    for idx, val in ((_R_E1, i1), (_R_E2, i2), (_R_RANK1, rank1), (_R_RANK2, rank2), (_R_W1, w1), (_R_W2, w2)):
        rec = jnp.where(lane == idx, val, rec)
    r_ref[...] = rec


def _router(h2, w_router, b_router):
    t, d = h2.shape
    tb = min(t, 512)
    return pl.pallas_call(
        _router_kernel,
        grid=(t // tb,),
        in_specs=[pl.BlockSpec((tb, d), lambda i: (i, 0)), pl.BlockSpec((d, 2 * LANES), lambda i: (0, 0)),
                  pl.BlockSpec((1, 2 * LANES), lambda i: (0, 0))],
        out_specs=[pl.BlockSpec((tb, LANES), lambda i: (i, 0)), pl.BlockSpec((8, LANES), lambda i: (0, 0))],
        out_shape=[jax.ShapeDtypeStruct((t, LANES), F32), jax.ShapeDtypeStruct((8, LANES), F32)],
        scratch_shapes=[pltpu.VMEM((8, LANES), F32)],
        compiler_params=_cp("arbitrary"),
    )(h2, w_router, b_router)


def _positions_kernel(r_ref, cnt_ref, pos_ref, te_ref, nv_ref):
    t = r_ref.shape[0]
    nt = te_ref.shape[0]
    cnt = cnt_ref[...]
    tiles = jnp.floor((cnt + (MOE_TM - 1)) * (1.0 / MOE_TM))
    rr = lax.broadcasted_iota(I32, (LANES, LANES), 0)
    cc = lax.broadcasted_iota(I32, (LANES, LANES), 1)
    start_tiles = _dot(tiles.astype(BF16), jnp.where(rr < cc, 1.0, 0.0).astype(BF16))
    start = start_tiles[0:1, :] * float(MOE_TM)
    rec = r_ref[...]
    lane = lax.broadcasted_iota(I32, (t, LANES), 1)
    lanef = lane.astype(F32)
    pos1 = jnp.sum(jnp.where(lanef == rec[:, _R_E1:_R_E1 + 1], start, 0.0), axis=-1, keepdims=True) \
        + rec[:, _R_RANK1:_R_RANK1 + 1]
    pos2 = jnp.sum(jnp.where(lanef == rec[:, _R_E2:_R_E2 + 1], start, 0.0), axis=-1, keepdims=True) \
        + rec[:, _R_RANK2:_R_RANK2 + 1]
    pos_ref[...] = jnp.where(lane == 0, pos1, jnp.where(lane == 1, pos2, 0.0)).astype(I32)
    tile_id = lax.broadcasted_iota(I32, (nt, LANES), 0).astype(F32)
    owner = jnp.sum(jnp.where(start_tiles[0:1, :] <= tile_id, 1.0, 0.0), axis=-1, keepdims=True) - 1.0
    te_ref[...] = jnp.broadcast_to(owner, (nt, LANES)).astype(I32)
    nv_ref[...] = jnp.broadcast_to(jnp.sum(tiles[0:1, :], axis=-1, keepdims=True), (8, LANES)).astype(I32)


def _positions(rec, cnt, n_tiles):
    t = rec.shape[0]
    full = lambda shape: pl.BlockSpec(shape, lambda i: (0, 0))
    return pl.pallas_call(
        _positions_kernel,
        grid=(1,),
        in_specs=[full((t, LANES)), full((8, LANES))],
        out_specs=[full((t, LANES)), full((n_tiles, LANES)), full((8, LANES))],
        out_shape=[jax.ShapeDtypeStruct((t, LANES), I32), jax.ShapeDtypeStruct((n_tiles, LANES), I32),
                   jax.ShapeDtypeStruct((8, LANES), I32)],
        compiler_params=_cp("arbitrary"),
    )(rec, cnt)


def _invert_kernel(pos_ref, tok_ref):
    n_slots = tok_ref.shape[0]
    n_assign = pos_ref.shape[0]

    def clear(p, _):
        tok_ref[p] = 0
        return 0

    lax.fori_loop(0, n_slots, clear, 0)

    def scatter(a, _):
        tok_ref[pos_ref[a]] = a // 2
        return 0

    lax.fori_loop(0, n_assign, scatter, 0)


def _invert(pos, n_slots):
    return pl.pallas_call(
        _invert_kernel,
        in_specs=[pl.BlockSpec(memory_space=pltpu.SMEM)],
        out_specs=pl.BlockSpec(memory_space=pltpu.SMEM),
        out_shape=jax.ShapeDtypeStruct((n_slots,), I32),
    )(pos)


def _row_copy(src_ref, src_row, dst_ref, dst_row, sem):
    return pltpu.make_async_copy(src_ref.at[pl.ds(src_row, 1)], dst_ref.at[pl.ds(dst_row, 1)], sem)


def _dispatch_kernel(tok_ref, nv_ref, h_ref, o_ref, buf_ref, sem):
    i = pl.program_id(0)
    tm = buf_ref.shape[0]

    @pl.when(i < nv_ref[0])
    def _():
        def issue(r, _):
            _row_copy(h_ref, tok_ref[i * tm + r], buf_ref, r, sem).start()
            return 0

        lax.fori_loop(0, tm, issue, 0)

        def drain(r, _):
            _row_copy(h_ref, 0, buf_ref, r, sem).wait()
            return 0

        lax.fori_loop(0, tm, drain, 0)
        o_ref[...] = buf_ref[...].astype(o_ref.dtype)


def _dispatch(tok, nv, h2, n_tiles):
    t, d = h2.shape
    return pl.pallas_call(
        _dispatch_kernel,
        grid_spec=pltpu.PrefetchScalarGridSpec(
            num_scalar_prefetch=2,
            grid=(n_tiles,),
            in_specs=[pl.BlockSpec(memory_space=pl.ANY)],
            out_specs=pl.BlockSpec((MOE_TM, d), lambda i, tok, nv: (jnp.minimum(i, nv[0] - 1), 0)),
            scratch_shapes=[pltpu.VMEM((MOE_TM, d), F32), pltpu.SemaphoreType.DMA(())],
        ),
        out_shape=jax.ShapeDtypeStruct((n_tiles * MOE_TM, d), BF16),
        compiler_params=_cp("arbitrary"),
    )(tok, nv, h2)


def _ffn_kernel(te_ref, nv_ref, x_ref, wg_ref, wu_ref, wd_ref, o_ref):
    i = pl.program_id(0)
    f = pl.program_id(1)

    @pl.when(i < nv_ref[0])
    def _():
        x = x_ref[...]
        g = _dot(x, wg_ref[0].astype(BF16))
        u = _dot(x, wu_ref[0].astype(BF16))
        act = (g * _sigmoid(g)) * u
        y = _dot(act.astype(BF16), wd_ref[0].astype(BF16))

        @pl.when(f == 0)
        def _():
            o_ref[...] = y

        @pl.when(f > 0)
        def _():
            o_ref[...] = o_ref[...] + y


def _ffn(te, nv, xs, w_gate, w_up, w_down):
    n_slots, d = xs.shape
    n_tiles = n_slots // MOE_TM
    fb = D_FF // MOE_NF

    def tile(i, nv):
        return jnp.minimum(i, nv[0] - 1)

    def fblk(i, f, nv):
        return jnp.where(i < nv[0], f, MOE_NF - 1)

    return pl.pallas_call(
        _ffn_kernel,
        grid_spec=pltpu.PrefetchScalarGridSpec(
            num_scalar_prefetch=2,
            grid=(n_tiles, MOE_NF),
            in_specs=[
                pl.BlockSpec((MOE_TM, d), lambda i, f, te, nv: (tile(i, nv), 0)),
                pl.BlockSpec((1, d, fb), lambda i, f, te, nv: (te[tile(i, nv)], 0, fblk(i, f, nv))),
                pl.BlockSpec((1, d, fb), lambda i, f, te, nv: (te[tile(i, nv)], 0, fblk(i, f, nv))),
                pl.BlockSpec((1, fb, d), lambda i, f, te, nv: (te[tile(i, nv)], fblk(i, f, nv), 0)),
            ],
            out_specs=pl.BlockSpec((MOE_TM, d), lambda i, f, te, nv: (tile(i, nv), 0)),
        ),
        out_shape=jax.ShapeDtypeStruct((n_slots, d), F32),
        compiler_params=_cp("arbitrary", "arbitrary"),
    )(te, nv, xs, w_gate, w_up, w_down)


def _combine_kernel(pos_ref, y_ref, x_ref, r_ref, o_ref, buf_ref, sem):
    i = pl.program_id(0)
    tb = x_ref.shape[0]

    def issue(r, _):
        a = 2 * (i * tb + r)
        _row_copy(y_ref, pos_ref[a], buf_ref.at[0], r, sem).start()
        _row_copy(y_ref, pos_ref[a + 1], buf_ref.at[1], r, sem).start()
        return 0

    lax.fori_loop(0, tb, issue, 0)

    def drain(r, _):
        _row_copy(y_ref, 0, buf_ref.at[0], r, sem).wait()
        _row_copy(y_ref, 0, buf_ref.at[1], r, sem).wait()
        return 0

    lax.fori_loop(0, tb, drain, 0)
    rec = r_ref[...]
    o_ref[...] = x_ref[...] + rec[:, _R_W1:_R_W1 + 1] * buf_ref[0] + rec[:, _R_W2:_R_W2 + 1] * buf_ref[1]


def _combine(pos, ys, x1, rec):
    t, d = x1.shape
    tb = min(t, 256)
    return pl.pallas_call(
        _combine_kernel,
        grid_spec=pltpu.PrefetchScalarGridSpec(
            num_scalar_prefetch=1,
            grid=(t // tb,),
            in_specs=[pl.BlockSpec(memory_space=pl.ANY), pl.BlockSpec((tb, d), lambda i, pos: (i, 0)),
                      pl.BlockSpec((tb, LANES), lambda i, pos: (i, 0))],
            out_specs=pl.BlockSpec((tb, d), lambda i, pos: (i, 0)),
            scratch_shapes=[pltpu.VMEM((2, tb, d), F32), pltpu.SemaphoreType.DMA(())],
        ),
        out_shape=jax.ShapeDtypeStruct((t, d), F32),
        compiler_params=_cp("arbitrary"),
    )(pos, ys, x1, rec)


def _moe(x1, h2, w_rg, b_rg, w_re, b_re, w_gate, w_up, w_down):
    t, d = x1.shape
    pad_w = jnp.zeros((d, LANES - N_GROUPS), F32)
    w_router = jnp.concatenate([w_re, w_rg, pad_w], axis=1).astype(BF16)
    b_router = jnp.concatenate([b_re, b_rg, jnp.zeros((LANES - N_GROUPS,), F32)]).reshape(1, 2 * LANES)
    n_tiles = 2 * t // MOE_TM + N_EXPERTS
    rec, cnt = _router(h2, w_router, b_router)
    pos2d, te2d, nv2d = _positions(rec, cnt, n_tiles)
    pos = pos2d[:, :2].reshape(2 * t)
    te = te2d[:, 0]
    nv = nv2d[0, :1]
    tok = _invert(pos, n_tiles * MOE_TM)
    xs = _dispatch(tok, nv, h2, n_tiles)
    ys = _ffn(te, nv, xs, w_gate, w_up, w_down)
    return _combine(pos, ys, x1, rec)


def kernel(x, positions, attn_norm, w_in, q_norm, k_norm, idx_k_norm, conv_w, a_log, dt_bias, gdn_out_norm,
           w_o_a, w_o_b, w_out, ffn_norm, w_router_group, b_router_group, w_router_expert, b_router_expert,
           w_gate, w_up, w_down):
    b, t, d = x.shape
    x2 = x.reshape(t, d)
    layer = 0
    h = _rmsnorm(x2, attn_norm[layer])
    w_small = _small_weights(w_in[layer])
    (ps,) = _matmul(h, w_small, [F32], 1024, w_small.shape[1])
    pst = _matmul_nt(w_in[layer][:, _OFF_BA:_OFF_BA + B_V_HEADS].T.astype(BF16), h, 1024)
    out_a = _dsa_branch(h, w_in[layer], ps, positions, q_norm[layer], k_norm[layer], idx_k_norm[layer])
    out_b = _gdn_branch(h, w_in[layer], ps, pst, conv_w[layer], a_log[layer], dt_bias[layer], gdn_out_norm[layer])
    x1, h2 = _mix(x2, h, w_in[layer], out_a, out_b, w_o_a[layer], w_o_b[layer], w_out[layer], ffn_norm[layer])
    out = _moe(x1, h2, w_router_group[layer], b_router_group[layer], w_router_expert[layer], b_router_expert[layer],
               w_gate[layer], w_up[layer], w_down[layer])
    return out.reshape(b, t, d)
```

```python
import functools

import jax
import jax.numpy as jnp
from jax import lax
from jax.experimental import pallas as pl
from jax.experimental.pallas import tpu as pltpu

F32 = jnp.float32
BF16 = jnp.bfloat16
I32 = jnp.int32

EPS = 1e-6
NEG = -1e30
BIG = 1e30
LOG2_E = 1.4426950408889634
ROPE_THETA = 500000.0
CHUNK = 64
A_HEADS = 16
HEAD_DIM = 128
IDX_HEADS = 16
IDX_DIM = 64
TOPK_MAX = 256
Q_BLOCK = 128
B_QK_HEADS = 16
B_V_HEADS = 32
CONV_WIDTH = 4
N_GROUPS = 8
EXPERTS_PER_GROUP = 16
N_EXPERTS = 128
D_FF = 768
LANES = 128
VMEM_LIMIT = 56 * 1024 * 1024

NT_DIMS = (((1,), (1,)), ((), ()))


def _cp(*sem):
    return pltpu.CompilerParams(dimension_semantics=sem, vmem_limit_bytes=VMEM_LIMIT)


def _dot(a, b):
    return jnp.dot(a, b, preferred_element_type=F32)


def _dot_nt(a, b):
    return lax.dot_general(a, b, NT_DIMS, preferred_element_type=F32)


def _dot_hi(a, b):
    return jnp.dot(a, b, preferred_element_type=F32, precision=lax.Precision.HIGHEST)


def _sigmoid(x):
    return 1.0 / (1.0 + jnp.exp(-x))


def _softplus(x):
    return jnp.maximum(x, 0.0) + jnp.log(1.0 + jnp.exp(-jnp.abs(x)))


def _rmsnorm_kernel(x_ref, g_ref, o_ref):
    x = x_ref[...]
    ms = jnp.mean(x * x, axis=-1, keepdims=True)
    o_ref[...] = (x * lax.rsqrt(ms + EPS) * g_ref[...]).astype(o_ref.dtype)


def _rmsnorm(x, gain, out_dtype=BF16):
    t, d = x.shape
    tm = min(t, 512)
    return pl.pallas_call(
        _rmsnorm_kernel,
        grid=(t // tm,),
        in_specs=[pl.BlockSpec((tm, d), lambda i: (i, 0)), pl.BlockSpec((1, d), lambda i: (0, 0))],
        out_specs=pl.BlockSpec((tm, d), lambda i: (i, 0)),
        out_shape=jax.ShapeDtypeStruct((t, d), out_dtype),
        compiler_params=_cp("parallel"),
    )(x, gain.reshape(1, d))


def _mm_kernel(*refs, n_extra, epilogue):
    a_ref, b_ref = refs[:2]
    extra = refs[2:2 + n_extra]
    outs = refs[2 + n_extra:]
    acc = _dot(a_ref[...], b_ref[...])
    res = epilogue(acc, *[e[...] for e in extra]) if epilogue is not None else (acc,)
    for o_ref, r in zip(outs, res):
        o_ref[...] = r.astype(o_ref.dtype)


def _matmul(a, b, out_dtypes, tm, tn, epilogue=None, extras=()):
    m, k = a.shape
    n = b.shape[1]
    tm, tn = min(tm, m), min(tn, n)
    in_specs = [pl.BlockSpec((tm, k), lambda i, j: (i, 0)), pl.BlockSpec((k, tn), lambda i, j: (0, j))]
    args = [a, b]
    for arr, kind in extras:
        if kind == "tile":
            in_specs.append(pl.BlockSpec((tm, tn), lambda i, j: (i, j)))
        else:
            in_specs.append(pl.BlockSpec((1, tn), lambda i, j: (0, j)))
        args.append(arr)
    out = pl.pallas_call(
        functools.partial(_mm_kernel, n_extra=len(extras), epilogue=epilogue),
        grid=(m // tm, n // tn),
        in_specs=in_specs,
        out_specs=[pl.BlockSpec((tm, tn), lambda i, j: (i, j)) for _ in out_dtypes],
        out_shape=[jax.ShapeDtypeStruct((m, n), dt) for dt in out_dtypes],
        compiler_params=_cp("parallel", "arbitrary"),
    )(*args)
    return out


def _mm_nt_kernel(a_ref, b_ref, o_ref):
    res = _dot_nt(a_ref[...], b_ref[...])
    half = res.shape[0] // 2
    for c in range(o_ref.shape[0]):
        cols = slice(c * CHUNK, (c + 1) * CHUNK)
        o_ref[c] = jnp.concatenate([res[:half, cols], res[half:, cols]], axis=1)


def _matmul_nt(a, b, tm):
    n, k = a.shape
    m = b.shape[0]
    tm = min(tm, m)
    return pl.pallas_call(
        _mm_nt_kernel,
        grid=(m // tm,),
        in_specs=[pl.BlockSpec((n, k), lambda i: (0, 0)), pl.BlockSpec((tm, k), lambda i: (i, 0))],
        out_specs=pl.BlockSpec((tm // CHUNK, n // 2, 2 * CHUNK), lambda i: (i, 0, 0)),
        out_shape=jax.ShapeDtypeStruct((m // CHUNK, n // 2, 2 * CHUNK), F32),
        compiler_params=_cp("parallel"),
    )(a, b)


def _dsa_prep_kernel(p1_ref, p3_ref, ps_ref, pos_ref, ifa_ref, ifi_ref, qn_ref, kn_ref, ikn_ref,
                     q_ref, k_ref, qi_ref, ki_ref, wi_ref):
    tm = p1_ref.shape[0]
    pos = pos_ref[...].astype(F32)
    lane = lax.broadcasted_iota(I32, (tm, LANES), 1)
    ang_a = pos * ifa_ref[...]
    cos_a = jnp.cos(ang_a)
    sin_a = jnp.sin(ang_a)
    sin_a = jnp.where(lane < 16, -sin_a, sin_a)
    low_a = lane < 16

    def rope_a(x):
        partner = jnp.where(low_a, pltpu.roll(x, LANES - 16, 1), pltpu.roll(x, 16, 1))
        return x * cos_a + partner * sin_a

    d64 = lane & 63
    ang_i = pos * ifi_ref[...]
    cos_i = jnp.cos(ang_i)
    sin_i = jnp.sin(ang_i)
    low_i = d64 < 8
    sin_i = jnp.where(low_i, -sin_i, sin_i)

    def rope_i(x):
        partner = jnp.where(low_i, pltpu.roll(x, LANES - 8, 1), pltpu.roll(x, 8, 1))
        return x * cos_i + partner * sin_i

    qn = qn_ref[...]
    kn = kn_ref[...]
    scale = HEAD_DIM ** -0.5 * LOG2_E
    for h in range(A_HEADS):
        sl = slice(h * HEAD_DIM, (h + 1) * HEAD_DIM)
        xq = p1_ref[:, sl]
        yq = xq * lax.rsqrt(jnp.mean(xq * xq, axis=-1, keepdims=True) + EPS) * qn
        q_ref[:, sl] = (rope_a(yq) * scale).astype(q_ref.dtype)
        xk = p1_ref[:, A_HEADS * HEAD_DIM + h * HEAD_DIM:A_HEADS * HEAD_DIM + (h + 1) * HEAD_DIM]
        yk = xk * lax.rsqrt(jnp.mean(xk * xk, axis=-1, keepdims=True) + EPS) * kn
        k_ref[:, sl] = rope_a(yk).astype(k_ref.dtype)
    for j in range(IDX_HEADS * IDX_DIM // LANES):
        sl = slice(j * LANES, (j + 1) * LANES)
        qi_ref[:, sl] = rope_i(p3_ref[:, sl]).astype(qi_ref.dtype)
    xk = ps_ref[:, 0:LANES]
    ms = jnp.sum(xk * xk, axis=-1, keepdims=True) * (1.0 / IDX_DIM)
    yk = xk * lax.rsqrt(ms + EPS) * ikn_ref[...]
    ki_ref[...] = rope_i(yk)[:, :IDX_DIM].astype(ki_ref.dtype)
    wi_ref[...] = ps_ref[:, LANES:LANES + IDX_HEADS] * (IDX_HEADS ** -0.5 * IDX_DIM ** -0.5)


def _dsa_prep(p1, p3, ps, positions, ifa, ifi, qn, kn, ikn):
    t = p1.shape[0]
    tm = min(t, 256)
    aw = A_HEADS * HEAD_DIM
    iq = IDX_HEADS * IDX_DIM
    row = lambda i: (i, 0)
    fix = lambda i: (0, 0)
    return pl.pallas_call(
        _dsa_prep_kernel,
        grid=(t // tm,),
        in_specs=[
            pl.BlockSpec((tm, 2 * aw), row), pl.BlockSpec((tm, iq), row), pl.BlockSpec((tm, ps.shape[1]), row),
            pl.BlockSpec((tm, 1), row), pl.BlockSpec((1, LANES), fix), pl.BlockSpec((1, LANES), fix),
            pl.BlockSpec((1, LANES), fix), pl.BlockSpec((1, LANES), fix), pl.BlockSpec((1, LANES), fix),
        ],
        out_specs=[
            pl.BlockSpec((tm, aw), row), pl.BlockSpec((tm, aw), row), pl.BlockSpec((tm, iq), row),
            pl.BlockSpec((tm, IDX_DIM), row), pl.BlockSpec((tm, IDX_HEADS), row),
        ],
        out_shape=[
            jax.ShapeDtypeStruct((t, aw), BF16), jax.ShapeDtypeStruct((t, aw), BF16),
            jax.ShapeDtypeStruct((t, iq), BF16), jax.ShapeDtypeStruct((t, IDX_DIM), BF16),
            jax.ShapeDtypeStruct((t, IDX_HEADS), F32),
        ],
        compiler_params=_cp("parallel"),
    )(p1, p3, ps, positions, ifa, ifi, qn, kn, ikn)


def _indexer_kernel(qi_ref, w_ref, ki_ref, bias_ref, s_ref, wb_ref, *, kc, topk, maxit):
    i = pl.program_id(0)
    qb = Q_BLOCK
    n_ch = ((i + 1) * qb + kc - 1) // kc
    nslab = kc // LANES
    for h in range(IDX_HEADS):
        wb_ref[h] = jnp.broadcast_to(w_ref[:, h:h + 1], (qb, LANES))
    row = lax.broadcasted_iota(I32, (qb, LANES), 0)
    lane = lax.broadcasted_iota(I32, (qb, LANES), 1)
    limit = jnp.where(row < CHUNK, i * qb + CHUNK, (i + 1) * qb)

    def score_body(c, carry):
        mn, mx = carry
        off = pl.multiple_of(c * kc, kc)
        kblk = ki_ref[pl.ds(off, kc), :]
        accs = [jnp.zeros((qb, LANES), F32) for _ in range(nslab)]
        for h in range(IDX_HEADS):
            s = _dot_nt(qi_ref[:, h * IDX_DIM:(h + 1) * IDX_DIM], kblk)
            wbh = wb_ref[h]
            for j in range(nslab):
                accs[j] = accs[j] + wbh * jnp.maximum(s[:, j * LANES:(j + 1) * LANES], 0.0)
        for j in range(nslab):
            adm = (lane + (off + j * LANES)) < limit
            val = jnp.where(adm, accs[j], NEG)
            s_ref[:, pl.ds(pl.multiple_of(off + j * LANES, LANES), LANES)] = val
            mn = jnp.minimum(mn, jnp.where(adm, accs[j], BIG))
            mx = jnp.maximum(mx, val)
        return mn, mx

    mn, mx = lax.fori_loop(0, n_ch, score_body,
                           (jnp.full((qb, LANES), BIG, F32), jnp.full((qb, LANES), NEG, F32)))
    lo0 = jnp.min(mn, axis=-1, keepdims=True)
    hi0 = jnp.max(mx, axis=-1, keepdims=True)
    kf = float(topk)

    def count_ge(thr):
        thr_b = jnp.broadcast_to(thr, (qb, LANES))

        def body(c, acc):
            off = pl.multiple_of(c * kc, kc)
            for j in range(nslab):
                sc = s_ref[:, pl.ds(pl.multiple_of(off + j * LANES, LANES), LANES)]
                acc = acc + jnp.where(sc >= thr_b, 1.0, 0.0)
            return acc

        acc = lax.fori_loop(0, n_ch, body, jnp.zeros((qb, LANES), F32))
        return jnp.sum(acc, axis=-1, keepdims=True)

    done0 = jnp.where(limit[:, 0:1] <= topk, 1.0, 0.0)

    def cond(st):
        it, _, _, done = st
        return jnp.logical_and(it < maxit, jnp.min(done) < 0.5)

    def body(st):
        it, lo, hi, done = st
        mid = lo + (hi - lo) * 0.5
        stuck = jnp.logical_or(mid <= lo, mid >= hi)
        c = count_ge(mid)
        ge = c >= kf
        lo = jnp.where(ge, mid, lo)
        hi = jnp.where(ge, hi, mid)
        done = jnp.maximum(done, jnp.where(jnp.logical_or(c == kf, stuck), 1.0, 0.0))
        return it + 1, lo, hi, done

    _, lo, hi, _ = lax.while_loop(cond, body, (jnp.int32(0), lo0, hi0, done0))
    c_hi = count_ge(hi)
    lo = jnp.where(c_hi >= kf, hi, lo)
    lo_b = jnp.broadcast_to(lo, (qb, LANES))

    bias_ref[...] = jnp.full(bias_ref.shape, NEG, bias_ref.dtype)

    def write_body(c, _):
        off = pl.multiple_of(c * kc, kc)
        for j in range(nslab):
            o = pl.multiple_of(off + j * LANES, LANES)
            sc = s_ref[:, pl.ds(o, LANES)]
            bias_ref[:, pl.ds(o, LANES)] = jnp.where(sc >= lo_b, 0.0, NEG).astype(bias_ref.dtype)
        return 0

    lax.fori_loop(0, n_ch, write_body, 0)


def _indexer(q_idx, w_idx, k_idx, topk):
    t = q_idx.shape[0]
    kc = min(t, 512)
    return pl.pallas_call(
        functools.partial(_indexer_kernel, kc=kc, topk=topk, maxit=64),
        grid=(t // Q_BLOCK,),
        in_specs=[
            pl.BlockSpec((Q_BLOCK, IDX_HEADS * IDX_DIM), lambda i: (i, 0)),
            pl.BlockSpec((Q_BLOCK, IDX_HEADS), lambda i: (i, 0)),
            pl.BlockSpec((t, IDX_DIM), lambda i: (0, 0)),
        ],
        out_specs=pl.BlockSpec((Q_BLOCK, t), lambda i: (i, 0)),
        out_shape=jax.ShapeDtypeStruct((t, t), BF16),
        scratch_shapes=[pltpu.VMEM((Q_BLOCK, t), F32), pltpu.VMEM((IDX_HEADS, Q_BLOCK, LANES), F32)],
        compiler_params=_cp("parallel"),
    )(q_idx, w_idx, k_idx)


def _attn_kernel(q_ref, k_ref, v_ref, b_ref, o_ref, acc_ref, m_ref, l_ref):
    qi = pl.program_id(0)
    ki = pl.program_id(1)

    @pl.when(ki == 0)
    def _():
        acc_ref[...] = jnp.zeros_like(acc_ref)
        m_ref[...] = jnp.full(m_ref.shape, NEG, F32)
        l_ref[...] = jnp.zeros_like(l_ref)

    @pl.when(ki <= qi)
    def _():
        bias = b_ref[...].astype(F32)
        ones = jnp.ones((k_ref.shape[0], HEAD_DIM), BF16)
        for h in range(A_HEADS):
            sl = slice(h * HEAD_DIM, (h + 1) * HEAD_DIM)
            s = _dot_nt(q_ref[:, sl], k_ref[:, sl]) + bias
            m_prev = m_ref[h]
            m_new = jnp.maximum(m_prev, jnp.max(s, axis=-1, keepdims=True))
            alpha = jnp.exp2(m_prev - m_new)
            p = jnp.exp2(s - m_new[:, 0:1]).astype(BF16)
            pv = _dot(p, jnp.concatenate([v_ref[:, sl], ones], axis=1))
            l_ref[h] = alpha * l_ref[h] + pv[:, HEAD_DIM:]
            acc_ref[:, sl] = alpha * acc_ref[:, sl] + pv[:, :HEAD_DIM]
            m_ref[h] = m_new

    @pl.when(ki == qi)
    def _():
        for h in range(A_HEADS):
            sl = slice(h * HEAD_DIM, (h + 1) * HEAD_DIM)
            o_ref[:, sl] = (acc_ref[:, sl] / l_ref[h]).astype(o_ref.dtype)


def _attention(q, k, v, bias):
    t, aw = q.shape
    bq = min(t, 512)
    nb = t // bq
    kv_map = lambda i, j: (jnp.minimum(i, j), 0)
    return pl.pallas_call(
        _attn_kernel,
        grid=(nb, nb),
        in_specs=[
            pl.BlockSpec((bq, aw), lambda i, j: (i, 0)),
            pl.BlockSpec((bq, aw), kv_map),
            pl.BlockSpec((bq, aw), kv_map),
            pl.BlockSpec((bq, bq), lambda i, j: (i, jnp.minimum(i, j))),
        ],
        out_specs=pl.BlockSpec((bq, aw), lambda i, j: (i, 0)),
        out_shape=jax.ShapeDtypeStruct((t, aw), BF16),
        scratch_shapes=[pltpu.VMEM((bq, aw), F32), pltpu.VMEM((A_HEADS, bq, LANES), F32),
                        pltpu.VMEM((A_HEADS, bq, LANES), F32)],
        compiler_params=_cp("parallel", "arbitrary"),
    )(q, k, v, bias)


def _gdn_prep_kernel(cur_ref, prev_ref, w_ref, o_ref, *, qk_blocks):
    i = pl.program_id(0)
    j = pl.program_id(1)
    tm = cur_ref.shape[0]
    cur = cur_ref[...]
    prev = prev_ref[...] * jnp.where(i > 0, 1.0, 0.0)
    xcat = jnp.concatenate([prev, cur], axis=0)
    y = cur * w_ref[CONV_WIDTH - 1:CONV_WIDTH, :]
    for tap in range(CONV_WIDTH - 1):
        y = y + xcat[8 - (CONV_WIDTH - 1) + tap:8 - (CONV_WIDTH - 1) + tap + tm, :] * w_ref[tap:tap + 1, :]
    y = y * _sigmoid(y)
    is_qk = j < 2 * qk_blocks
    qscale = jnp.where(j < qk_blocks, HEAD_DIM ** -0.5, 1.0)
    for h in range(cur.shape[1] // HEAD_DIM):
        yh = y[:, h * HEAD_DIM:(h + 1) * HEAD_DIM]
        nrm = lax.rsqrt(jnp.sum(yh * yh, axis=-1, keepdims=True) + EPS) * qscale
        o_ref[h] = jnp.where(is_qk, yh * nrm, yh)


def _gdn_prep(p4, conv_w):
    t, ch = p4.shape
    tm = min(t, 256)
    cb = 1024
    hb = cb // HEAD_DIM
    qk_blocks = B_QK_HEADS * HEAD_DIM // cb
    return pl.pallas_call(
        functools.partial(_gdn_prep_kernel, qk_blocks=qk_blocks),
        grid=(t // tm, ch // cb),
        in_specs=[
            pl.BlockSpec((tm, cb), lambda i, j: (i, j)),
            pl.BlockSpec((8, cb), lambda i, j: (jnp.maximum(i * (tm // 8) - 1, 0), j)),
            pl.BlockSpec((CONV_WIDTH, cb), lambda i, j: (0, j)),
        ],
        out_specs=pl.BlockSpec((hb, tm, HEAD_DIM), lambda i, j: (j, i, 0)),
        out_shape=jax.ShapeDtypeStruct((ch // HEAD_DIM, t, HEAD_DIM), F32),
        compiler_params=_cp("parallel", "parallel"),
    )(p4, p4, conv_w)


N_PAIRS = B_V_HEADS // 2


def _tri_inverse_minus_eye(a, row, col, lane_lo):
    n = len(a)
    lane_hi = jnp.logical_not(lane_lo)

    def mm(x, y):
        bd = jnp.concatenate([jnp.where(lane_lo, y, 0.0), jnp.where(lane_hi, y, 0.0)], axis=0)
        return _dot(x.astype(BF16), bd.astype(BF16))

    blk16 = (row // 16) == (col // 16)
    blk32 = (row // 32) == (col // 32)
    d1 = [jnp.where(blk16, a[p], 0.0) for p in range(n)]
    e = [-d1[p] for p in range(n)]
    dk = d1
    for _ in range(3):
        dk = [mm(dk[p], dk[p]) for p in range(n)]
        de = [mm(dk[p], e[p]) for p in range(n)]
        e = [e[p] + dk[p] + de[p] for p in range(n)]
    for sel in (jnp.logical_and(blk32, jnp.logical_not(blk16)), jnp.logical_not(blk32)):
        lo = [jnp.where(sel, a[p], 0.0) for p in range(n)]
        m1 = [lo[p] + mm(e[p], lo[p]) for p in range(n)]
        m2 = [m1[p] + mm(m1[p], e[p]) for p in range(n)]
        e = [e[p] - m2[p] for p in range(n)]
    return e


def _gdn_a_kernel(qkv_ref, ps_ref, pst_ref, alr_ref, alc_ref, dtr_ref, dtc_ref,
                  wq_ref, u_ref, ak_ref, egl_ref):
    c = CHUNK
    pairs = range(N_PAIRS)
    row = lax.broadcasted_iota(I32, (c, LANES), 0)
    lane = lax.broadcasted_iota(I32, (c, LANES), 1)
    col = lane & (c - 1)
    lane_lo = lane < c
    lane_hi = jnp.logical_not(lane_lo)
    incl = row >= col
    strict = row > col
    eye2 = jnp.where(row == col, 1.0, 0.0)
    r64 = lax.broadcasted_iota(I32, (c, c), 0)
    c64 = lax.broadcasted_iota(I32, (c, c), 1)
    lower = jnp.where(r64 >= c64, 1.0, 0.0)
    r128 = lax.broadcasted_iota(I32, (LANES, LANES), 0)
    c128 = lax.broadcasted_iota(I32, (LANES, LANES), 1)
    upper2 = jnp.where(jnp.logical_and(r128 // c == c128 // c, r128 <= c128), 1.0, 0.0)

    beta_col = _sigmoid(ps_ref[:, 2 * LANES:2 * LANES + B_V_HEADS])
    g_col = -jnp.exp(alr_ref[...]) * _softplus(ps_ref[:, 3 * LANES:3 * LANES + B_V_HEADS] + dtr_ref[...])
    g_row2 = -jnp.exp(alc_ref[...]) * _softplus(pst_ref[0] + dtc_ref[...])
    gc_col = _dot_hi(lower, g_col)
    gc_row2 = _dot_hi(g_row2, upper2)
    eg_col = jnp.exp(gc_col)
    egl_ref[0] = eg_col[c - 1:c, :]

    def col_pair(m, p):
        return jnp.where(lane_lo, m[:, 2 * p:2 * p + 1], m[:, 2 * p + 1:2 * p + 2])

    q = [qkv_ref[p] for p in pairs]
    k = [qkv_ref[B_QK_HEADS + p] for p in pairs]
    kbf = [k[p].astype(BF16) for p in pairs]
    g = [_dot_nt(jnp.concatenate([q[p].astype(BF16), kbf[p]], axis=0),
                 jnp.concatenate([kbf[p], kbf[p]], axis=0)) for p in pairs]
    decay = [jnp.exp(jnp.where(incl, col_pair(gc_col, p) - gc_row2[p:p + 1, :], NEG)) for p in pairs]
    a_mat = [jnp.where(strict, col_pair(beta_col, p) * g[p][c:] * decay[p], 0.0) for p in pairs]
    attn = [g[p][:c] * decay[p] for p in pairs]
    e = _tri_inverse_minus_eye(a_mat, row, col, lane_lo)

    def rhs_of(p, h):
        bc = beta_col[:, h:h + 1]
        return jnp.concatenate([qkv_ref[2 * B_QK_HEADS + h] * bc, k[p] * (bc * eg_col[:, h:h + 1])], axis=1)

    rhs = [jnp.concatenate([rhs_of(p, 2 * p), rhs_of(p, 2 * p + 1)], axis=0).astype(BF16) for p in pairs]
    t_mat = [eye2 + e[p] for p in pairs]
    uw = [[_dot(jnp.where(sel, t_mat[p], 0.0).astype(BF16), rhs[p]) for sel in (lane_lo, lane_hi)] for p in pairs]
    for p in pairs:
        kd = []
        for r in range(2):
            h = 2 * p + r
            u_ref[h] = uw[p][r][:, :HEAD_DIM]
            qe = q[p] * eg_col[:, h:h + 1]
            wq_ref[h] = jnp.concatenate([uw[p][r][:, HEAD_DIM:], qe], axis=0).astype(wq_ref.dtype)
            kd.append(k[p] * jnp.exp(gc_col[c - 1:c, h:h + 1] - gc_col[:, h:h + 1]))
        kdt = jnp.concatenate(kd, axis=0).T
        ak_ref[p] = jnp.concatenate([attn[p], kdt], axis=0).astype(ak_ref.dtype)


def _gdn_phase_a(qkv, ps, pst, a_log, dt_bias):
    nh, t, _ = qkv.shape
    n = t // CHUNK
    hv = B_V_HEADS
    fix2 = lambda i: (0, 0)
    return pl.pallas_call(
        _gdn_a_kernel,
        grid=(n,),
        in_specs=[
            pl.BlockSpec((nh, CHUNK, HEAD_DIM), lambda i: (0, i, 0)),
            pl.BlockSpec((CHUNK, ps.shape[1]), lambda i: (i, 0)),
            pl.BlockSpec((1, N_PAIRS, LANES), lambda i: (i, 0, 0)),
            pl.BlockSpec((1, hv), fix2), pl.BlockSpec((N_PAIRS, LANES), fix2),
            pl.BlockSpec((1, hv), fix2), pl.BlockSpec((N_PAIRS, LANES), fix2),
        ],
        out_specs=[
            pl.BlockSpec((hv, 2 * CHUNK, HEAD_DIM), lambda i: (0, i, 0)),
            pl.BlockSpec((hv, CHUNK, HEAD_DIM), lambda i: (0, i, 0)),
            pl.BlockSpec((N_PAIRS, CHUNK + HEAD_DIM, LANES), lambda i: (0, i, 0)),
            pl.BlockSpec((1, 1, hv), lambda i: (i, 0, 0)),
        ],
        out_shape=[
            jax.ShapeDtypeStruct((hv, 2 * t, HEAD_DIM), BF16),
            jax.ShapeDtypeStruct((hv, t, HEAD_DIM), F32),
            jax.ShapeDtypeStruct((N_PAIRS, n * (CHUNK + HEAD_DIM), LANES), BF16),
            jax.ShapeDtypeStruct((n, 1, hv), F32),
        ],
        compiler_params=_cp("parallel"),
    )(qkv, ps, pst, a_log.reshape(1, hv), _pair_lanes(a_log), dt_bias.reshape(1, hv), _pair_lanes(dt_bias))


def _pair_lanes(per_head):
    return jnp.repeat(per_head.reshape(N_PAIRS, 2), CHUNK, axis=1)


def _gdn_b_kernel(wq_ref, u_ref, ak_ref, egl_ref, z_ref, gn_ref, o_ref, s_ref):
    c = CHUNK

    @pl.when(pl.program_id(0) == 0)
    def _():
        s_ref[...] = jnp.zeros_like(s_ref)

    gn = gn_ref[...]
    heads = range(B_V_HEADS)
    lane_lo = lax.broadcasted_iota(I32, (c + HEAD_DIM, LANES), 1) < c
    lane_hi = jnp.logical_not(lane_lo)
    zero = jnp.zeros((), ak_ref.dtype)
    s = [s_ref[h] for h in heads]
    r1 = [_dot(wq_ref[h], s[h].astype(BF16)) for h in heads]
    vn = [u_ref[h] - r1[h][:c] for h in heads]
    r2 = []
    for p in range(N_PAIRS):
        vst = jnp.concatenate([vn[2 * p], vn[2 * p + 1]], axis=0).astype(BF16)
        akp = ak_ref[p]
        r2.append(_dot(jnp.where(lane_lo, akp, zero), vst))
        r2.append(_dot(jnp.where(lane_hi, akp, zero), vst))
    for h in heads:
        sl = slice(h * HEAD_DIM, (h + 1) * HEAD_DIM)
        o = r1[h][c:] + r2[h][:c]
        s_ref[h] = s[h] * egl_ref[0, :, h:h + 1] + r2[h][c:]
        on = o * lax.rsqrt(jnp.mean(o * o, axis=-1, keepdims=True) + EPS) * gn
        z = z_ref[:, sl]
        o_ref[:, sl] = (on * (z * _sigmoid(z))).astype(o_ref.dtype)


def _gdn_phase_b(wq, u, ak, egl, z, out_norm):
    hv, t, _ = u.shape
    n = t // CHUNK
    return pl.pallas_call(
        _gdn_b_kernel,
        grid=(n,),
        in_specs=[
            pl.BlockSpec((hv, 2 * CHUNK, HEAD_DIM), lambda i: (0, i, 0)),
            pl.BlockSpec((hv, CHUNK, HEAD_DIM), lambda i: (0, i, 0)),
            pl.BlockSpec((N_PAIRS, CHUNK + HEAD_DIM, LANES), lambda i: (0, i, 0)),
            pl.BlockSpec((1, 1, hv), lambda i: (i, 0, 0)),
            pl.BlockSpec((CHUNK, hv * HEAD_DIM), lambda i: (i, 0)),
            pl.BlockSpec((1, HEAD_DIM), lambda i: (0, 0)),
        ],
        out_specs=pl.BlockSpec((CHUNK, hv * HEAD_DIM), lambda i: (i, 0)),
        out_shape=jax.ShapeDtypeStruct((t, hv * HEAD_DIM), BF16),
        scratch_shapes=[pltpu.VMEM((hv, HEAD_DIM, HEAD_DIM), F32)],
        compiler_params=_cp("arbitrary"),
    )(wq, u, ak, egl, z, out_norm.reshape(1, HEAD_DIM))


_A_W = A_HEADS * HEAD_DIM
_A_IQ = IDX_HEADS * IDX_DIM
_B_QK = B_QK_HEADS * HEAD_DIM
_B_V = B_V_HEADS * HEAD_DIM
_OFF_AIK = 3 * _A_W + _A_IQ
_OFF_AIW = _OFF_AIK + IDX_DIM
_OFF_BQ = _OFF_AIW + IDX_HEADS
_OFF_BZ = _OFF_BQ + 2 * _B_QK + _B_V
_OFF_BB = _OFF_BZ + _B_V
_OFF_BA = _OFF_BB + B_V_HEADS
_OFF_GATE = _OFF_BA + B_V_HEADS


def _small_weights(w_in):
    d = w_in.shape[0]
    pad = lambda a: jnp.pad(a, ((0, 0), (0, LANES - a.shape[1])))
    return jnp.concatenate([
        pad(w_in[:, _OFF_AIK:_OFF_AIK + IDX_DIM]), pad(w_in[:, _OFF_AIW:_OFF_AIW + IDX_HEADS]),
        pad(w_in[:, _OFF_BB:_OFF_BB + B_V_HEADS]), pad(w_in[:, _OFF_BA:_OFF_BA + B_V_HEADS]),
    ], axis=1).astype(BF16)


def _ba_weights_t(w_in):
    w = w_in[:, _OFF_BA:_OFF_BA + B_V_HEADS]
    return jnp.concatenate([w[:, 0::2], w[:, 1::2]], axis=1).T.astype(BF16)


def _rope_freqs(rot, width):
    half = rot // 2
    inv = jnp.power(ROPE_THETA, -jnp.arange(half, dtype=F32) * 2.0 / rot)
    pat = jnp.concatenate([inv, inv, jnp.zeros((width - rot,), F32)])
    return jnp.tile(pat, LANES // width).reshape(1, LANES)


def _dsa_branch(h, w_in, ps, positions, q_norm, k_norm, idx_k_norm):
    t = h.shape[0]
    tm = 1024
    (p1,) = _matmul(h, w_in[:, 0:2 * _A_W].astype(BF16), [F32], tm, 1024)
    (v,) = _matmul(h, w_in[:, 2 * _A_W:3 * _A_W].astype(BF16), [BF16], tm, 1024)
    (p3,) = _matmul(h, w_in[:, 3 * _A_W:3 * _A_W + _A_IQ].astype(BF16), [F32], tm, 1024)
    ikn = jnp.pad(idx_k_norm.reshape(1, IDX_DIM), ((0, 0), (0, LANES - IDX_DIM)))
    q, k, q_idx, k_idx, w_idx = _dsa_prep(
        p1, p3, ps, positions.reshape(t, 1), _rope_freqs(HEAD_DIM // 4, HEAD_DIM), _rope_freqs(IDX_DIM // 4, IDX_DIM),
        q_norm.reshape(1, HEAD_DIM), k_norm.reshape(1, HEAD_DIM), ikn)
    bias = _indexer(q_idx, w_idx, k_idx, min(TOPK_MAX, t // 4))
    return _attention(q, k, v, bias)


def _gdn_branch(h, w_in, ps, pst, conv_w, a_log, dt_bias, out_norm):
    tm = 1024
    (p4,) = _matmul(h, w_in[:, _OFF_BQ:_OFF_BZ].astype(BF16), [F32], tm, 1024)
    (z,) = _matmul(h, w_in[:, _OFF_BZ:_OFF_BB].astype(BF16), [F32], tm, 1024)
    qkv = _gdn_prep(p4, conv_w)
    wq, u, ak, egl = _gdn_phase_a(qkv, ps, pst, a_log, dt_bias)
    return _gdn_phase_b(wq, u, ak, egl, z, out_norm)


def _gate_a_epilogue(acc, g):
    return (_sigmoid(g) * acc,)


def _gate_b_epilogue(acc, g, m):
    return (m + _sigmoid(g) * acc,)


def _residual_norm_epilogue(acc, x, gain):
    x1 = x + acc
    h2 = x1 * lax.rsqrt(jnp.mean(x1 * x1, axis=-1, keepdims=True) + EPS) * gain
    return x1, h2


def _mix(x, h, w_in, out_a, out_b, w_o_a, w_o_b, w_out, ffn_norm):
    d = x.shape[1]
    (ga,) = _matmul(h, w_in[:, _OFF_GATE:_OFF_GATE + d].astype(BF16), [F32], 1024, 1024)
    (gb,) = _matmul(h, w_in[:, _OFF_GATE + d:_OFF_GATE + 2 * d].astype(BF16), [F32], 1024, 1024)
    (mixa,) = _matmul(out_a, w_o_a.astype(BF16), [F32], 1024, 1024, _gate_a_epilogue, [(ga, "tile")])
    (mix,) = _matmul(out_b, w_o_b.astype(BF16), [BF16], 1024, 512, _gate_b_epilogue, [(gb, "tile"), (mixa, "tile")])
    return _matmul(mix, w_out.astype(BF16), [F32, F32], 512, d, _residual_norm_epilogue,
                   [(x, "tile"), (ffn_norm.reshape(1, d), "row")])


MOE_TM = 256
MOE_NF = 2
_R_E1, _R_E2, _R_RANK1, _R_RANK2, _R_W1, _R_W2 = 0, 1, 2, 3, 4, 5


def _router_kernel(h_ref, w_ref, b_ref, r_ref, cnt_ref, carry_ref):
    i = pl.program_id(0)

    @pl.when(i == 0)
    def _():
        carry_ref[...] = jnp.zeros_like(carry_ref)

    tb = h_ref.shape[0]
    logits = _dot(h_ref[...].astype(BF16), w_ref[...]) + b_ref[...]
    le = logits[:, :N_EXPERTS]
    lg = logits[:, N_EXPERTS:]
    lane = lax.broadcasted_iota(I32, (tb, LANES), 1)
    lanef = lane.astype(F32)
    far = float(4 * LANES)

    lgm = jnp.where(lane < N_GROUPS, lg, NEG)
    gmax = jnp.max(lgm, axis=-1, keepdims=True)
    g_idx = jnp.min(jnp.where(lgm == gmax, lanef, far), axis=-1, keepdims=True)
    g_top = 1.0 / jnp.sum(jnp.exp(lgm - gmax), axis=-1, keepdims=True)

    in_grp = (lane // EXPERTS_PER_GROUP).astype(F32) == g_idx
    lem = jnp.where(in_grp, le, NEG)
    e1 = jnp.max(lem, axis=-1, keepdims=True)
    i1 = jnp.min(jnp.where(lem == e1, lanef, far), axis=-1, keepdims=True)
    lem2 = jnp.where(lanef == i1, NEG, lem)
    e2 = jnp.max(lem2, axis=-1, keepdims=True)
    i2 = jnp.min(jnp.where(lem2 == e2, lanef, far), axis=-1, keepdims=True)
    se = jnp.sum(jnp.exp(lem - e1), axis=-1, keepdims=True)
    p1 = 1.0 / se
    p2 = jnp.exp(e2 - e1) / se
    w1 = g_top * p1 / (p1 + p2)
    w2 = g_top * p2 / (p1 + p2)

    o1 = jnp.where(lanef == i1, 1.0, 0.0)
    o2 = jnp.where(lanef == i2, 1.0, 0.0)
    osum = o1 + o2
    rr = lax.broadcasted_iota(I32, (tb, tb), 0)
    cc = lax.broadcasted_iota(I32, (tb, tb), 1)
    before = jnp.where(cc < rr, 1.0, 0.0).astype(BF16)
    prefix = _dot(before, osum.astype(BF16)) + carry_ref[0:1, :]
    rank1 = jnp.sum(prefix * o1, axis=-1, keepdims=True)
    rank2 = jnp.sum(prefix * o2, axis=-1, keepdims=True)
    carry_ref[...] = carry_ref[...] + jnp.sum(osum, axis=0, keepdims=True)
    cnt_ref[...] = carry_ref[...]

    rec = jnp.zeros((tb, LANES), F32)
    for idx, val in ((_R_E1, i1), (_R_E2, i2), (_R_RANK1, rank1), (_R_RANK2, rank2), (_R_W1, w1), (_R_W2, w2)):
        rec = jnp.where(lane == idx, val, rec)
    r_ref[...] = rec


def _router(h2, w_router, b_router):
    t, d = h2.shape
    tb = min(t, 512)
    return pl.pallas_call(
        _router_kernel,
        grid=(t // tb,),
        in_specs=[pl.BlockSpec((tb, d), lambda i: (i, 0)), pl.BlockSpec((d, 2 * LANES), lambda i: (0, 0)),
                  pl.BlockSpec((1, 2 * LANES), lambda i: (0, 0))],
        out_specs=[pl.BlockSpec((tb, LANES), lambda i: (i, 0)), pl.BlockSpec((8, LANES), lambda i: (0, 0))],
        out_shape=[jax.ShapeDtypeStruct((t, LANES), F32), jax.ShapeDtypeStruct((8, LANES), F32)],
        scratch_shapes=[pltpu.VMEM((8, LANES), F32)],
        compiler_params=_cp("arbitrary"),
    )(h2, w_router, b_router)


def _positions_kernel(r_ref, cnt_ref, pos_ref, te_ref, nv_ref):
    t = r_ref.shape[0]
    nt = te_ref.shape[0]
    cnt = cnt_ref[...]
    tiles = jnp.floor((cnt + (MOE_TM - 1)) * (1.0 / MOE_TM))
    rr = lax.broadcasted_iota(I32, (LANES, LANES), 0)
    cc = lax.broadcasted_iota(I32, (LANES, LANES), 1)
    start_tiles = _dot(tiles.astype(BF16), jnp.where(rr < cc, 1.0, 0.0).astype(BF16))
    start = start_tiles[0:1, :] * float(MOE_TM)
    rec = r_ref[...]
    lane = lax.broadcasted_iota(I32, (t, LANES), 1)
    lanef = lane.astype(F32)
    pos1 = jnp.sum(jnp.where(lanef == rec[:, _R_E1:_R_E1 + 1], start, 0.0), axis=-1, keepdims=True) \
        + rec[:, _R_RANK1:_R_RANK1 + 1]
    pos2 = jnp.sum(jnp.where(lanef == rec[:, _R_E2:_R_E2 + 1], start, 0.0), axis=-1, keepdims=True) \
        + rec[:, _R_RANK2:_R_RANK2 + 1]
    pos_ref[...] = jnp.where(lane == 0, pos1, jnp.where(lane == 1, pos2, 0.0)).astype(I32)
    tile_id = lax.broadcasted_iota(I32, (nt, LANES), 0).astype(F32)
    tlane = lax.broadcasted_iota(I32, (nt, LANES), 1)
    owner = jnp.sum(jnp.where(start_tiles[0:1, :] <= tile_id, 1.0, 0.0), axis=-1, keepdims=True) - 1.0
    is_owner = tlane.astype(F32) == owner
    own_cnt = jnp.sum(jnp.where(is_owner, cnt[0:1, :], 0.0), axis=-1, keepdims=True)
    own_start = jnp.sum(jnp.where(is_owner, start_tiles[0:1, :], 0.0), axis=-1, keepdims=True)
    rows = jnp.clip(own_cnt - (tile_id[:, 0:1] - own_start) * float(MOE_TM), 0.0, float(MOE_TM))
    te_ref[...] = jnp.where(tlane == 0, owner, jnp.where(tlane == 1, rows, 0.0)).astype(I32)
    nv_ref[...] = jnp.broadcast_to(jnp.sum(tiles[0:1, :], axis=-1, keepdims=True), (8, LANES)).astype(I32)


def _positions(rec, cnt, n_tiles):
    t = rec.shape[0]
    full = lambda shape: pl.BlockSpec(shape, lambda i: (0, 0))
    return pl.pallas_call(
        _positions_kernel,
        grid=(1,),
        in_specs=[full((t, LANES)), full((8, LANES))],
        out_specs=[full((t, LANES)), full((n_tiles, LANES)), full((8, LANES))],
        out_shape=[jax.ShapeDtypeStruct((t, LANES), I32), jax.ShapeDtypeStruct((n_tiles, LANES), I32),
                   jax.ShapeDtypeStruct((8, LANES), I32)],
        compiler_params=_cp("arbitrary"),
    )(rec, cnt)


def _invert_kernel(pos_ref, tok_ref):
    def scatter(a, _):
        tok_ref[pos_ref[a]] = a // 2
        return 0

    lax.fori_loop(0, pos_ref.shape[0], scatter, 0, unroll=8)


def _invert(pos, n_slots):
    return pl.pallas_call(
        _invert_kernel,
        in_specs=[pl.BlockSpec(memory_space=pltpu.SMEM)],
        out_specs=pl.BlockSpec(memory_space=pltpu.SMEM),
        out_shape=jax.ShapeDtypeStruct((n_slots,), I32),
    )(pos)


def _row_copy(src_ref, src_row, dst_ref, dst_row, sem):
    return pltpu.make_async_copy(src_ref.at[pl.ds(src_row, 1)], dst_ref.at[pl.ds(dst_row, 1)], sem)


def _dispatch_kernel(tok_ref, rows_ref, nv_ref, h_ref, o_ref, buf_ref, sem):
    i = pl.program_id(0)
    tm = buf_ref.shape[1]
    nv = nv_ref[0]

    def issue(tile, slot):
        def body(r, _):
            _row_copy(h_ref, tok_ref[tile * tm + r], buf_ref.at[slot], r, sem.at[slot]).start()
            return 0

        lax.fori_loop(0, rows_ref[tile], body, 0)

    @pl.when(i == 0)
    def _():
        buf_ref[...] = jnp.zeros_like(buf_ref)
        issue(0, 0)

    @pl.when(i + 1 < nv)
    def _():
        issue(i + 1, (i + 1) % 2)

    @pl.when(i < nv)
    def _():
        slot = i % 2

        def drain(r, _):
            _row_copy(h_ref, 0, buf_ref.at[slot], r, sem.at[slot]).wait()
            return 0

        lax.fori_loop(0, rows_ref[i], drain, 0)
        o_ref[...] = buf_ref[slot].astype(o_ref.dtype)


def _dispatch(tok, rows, nv, h2, n_tiles):
    t, d = h2.shape
    return pl.pallas_call(
        _dispatch_kernel,
        grid_spec=pltpu.PrefetchScalarGridSpec(
            num_scalar_prefetch=3,
            grid=(n_tiles,),
            in_specs=[pl.BlockSpec(memory_space=pl.ANY)],
            out_specs=pl.BlockSpec((MOE_TM, d), lambda i, tok, rows, nv: (jnp.minimum(i, nv[0] - 1), 0)),
            scratch_shapes=[pltpu.VMEM((2, MOE_TM, d), F32), pltpu.SemaphoreType.DMA((2,))],
        ),
        out_shape=jax.ShapeDtypeStruct((n_tiles * MOE_TM, d), BF16),
        compiler_params=_cp("arbitrary"),
    )(tok, rows, nv, h2)


def _ffn_kernel(te_ref, nv_ref, x_ref, wg_ref, wu_ref, wd_ref, o_ref):
    i = pl.program_id(0)
    f = pl.program_id(1)

    @pl.when(i < nv_ref[0])
    def _():
        x = x_ref[...]
        g = _dot(x, wg_ref[0].astype(BF16))
        u = _dot(x, wu_ref[0].astype(BF16))
        act = (g * _sigmoid(g)) * u
        y = _dot(act.astype(BF16), wd_ref[0].astype(BF16))

        @pl.when(f == 0)
        def _():
            o_ref[...] = y

        @pl.when(f > 0)
        def _():
            o_ref[...] = o_ref[...] + y


def _ffn(te, nv, xs, w_gate, w_up, w_down):
    n_slots, d = xs.shape
    n_tiles = n_slots // MOE_TM
    fb = D_FF // MOE_NF

    def tile(i, nv):
        return jnp.minimum(i, nv[0] - 1)

    def fblk(i, f, nv):
        return jnp.where(i < nv[0], f, MOE_NF - 1)

    return pl.pallas_call(
        _ffn_kernel,
        grid_spec=pltpu.PrefetchScalarGridSpec(
            num_scalar_prefetch=2,
            grid=(n_tiles, MOE_NF),
            in_specs=[
                pl.BlockSpec((MOE_TM, d), lambda i, f, te, nv: (tile(i, nv), 0)),
                pl.BlockSpec((1, d, fb), lambda i, f, te, nv: (te[tile(i, nv)], 0, fblk(i, f, nv))),
                pl.BlockSpec((1, d, fb), lambda i, f, te, nv: (te[tile(i, nv)], 0, fblk(i, f, nv))),
                pl.BlockSpec((1, fb, d), lambda i, f, te, nv: (te[tile(i, nv)], fblk(i, f, nv), 0)),
            ],
            out_specs=pl.BlockSpec((MOE_TM, d), lambda i, f, te, nv: (tile(i, nv), 0)),
        ),
        out_shape=jax.ShapeDtypeStruct((n_slots, d), F32),
        compiler_params=_cp("arbitrary", "arbitrary"),
    )(te, nv, xs, w_gate, w_up, w_down)


def _combine_kernel(pos_ref, y_ref, x_ref, r_ref, o_ref, buf_ref, sem):
    i = pl.program_id(0)
    tb = x_ref.shape[0]

    def issue(r, _):
        a = 2 * (i * tb + r)
        _row_copy(y_ref, pos_ref[a], buf_ref.at[0], r, sem).start()
        _row_copy(y_ref, pos_ref[a + 1], buf_ref.at[1], r, sem).start()
        return 0

    lax.fori_loop(0, tb, issue, 0)

    def drain(r, _):
        _row_copy(y_ref, 0, buf_ref.at[0], r, sem).wait()
        _row_copy(y_ref, 0, buf_ref.at[1], r, sem).wait()
        return 0

    lax.fori_loop(0, tb, drain, 0)
    rec = r_ref[...]
    o_ref[...] = x_ref[...] + rec[:, _R_W1:_R_W1 + 1] * buf_ref[0] + rec[:, _R_W2:_R_W2 + 1] * buf_ref[1]


def _combine(pos, ys, x1, rec):
    t, d = x1.shape
    tb = min(t, 256)
    return pl.pallas_call(
        _combine_kernel,
        grid_spec=pltpu.PrefetchScalarGridSpec(
            num_scalar_prefetch=1,
            grid=(t // tb,),
            in_specs=[pl.BlockSpec(memory_space=pl.ANY), pl.BlockSpec((tb, d), lambda i, pos: (i, 0)),
                      pl.BlockSpec((tb, LANES), lambda i, pos: (i, 0))],
            out_specs=pl.BlockSpec((tb, d), lambda i, pos: (i, 0)),
            scratch_shapes=[pltpu.VMEM((2, tb, d), F32), pltpu.SemaphoreType.DMA(())],
        ),
        out_shape=jax.ShapeDtypeStruct((t, d), F32),
        compiler_params=_cp("arbitrary"),
    )(pos, ys, x1, rec)


def _moe(x1, h2, w_rg, b_rg, w_re, b_re, w_gate, w_up, w_down):
    t, d = x1.shape
    pad_w = jnp.zeros((d, LANES - N_GROUPS), F32)
    w_router = jnp.concatenate([w_re, w_rg, pad_w], axis=1).astype(BF16)
    b_router = jnp.concatenate([b_re, b_rg, jnp.zeros((LANES - N_GROUPS,), F32)]).reshape(1, 2 * LANES)
    n_tiles = 2 * t // MOE_TM + N_EXPERTS
    rec, cnt = _router(h2, w_router, b_router)
    pos2d, te2d, nv2d = _positions(rec, cnt, n_tiles)
    pos = pos2d[:, :2].reshape(2 * t)
    te = te2d[:, 0]
    rows = te2d[:, 1]
    nv = nv2d[0, :1]
    tok = _invert(pos, n_tiles * MOE_TM)
    xs = _dispatch(tok, rows, nv, h2, n_tiles)
    ys = _ffn(te, nv, xs, w_gate, w_up, w_down)
    return _combine(pos, ys, x1, rec)


def kernel(x, positions, attn_norm, w_in, q_norm, k_norm, idx_k_norm, conv_w, a_log, dt_bias, gdn_out_norm,
           w_o_a, w_o_b, w_out, ffn_norm, w_router_group, b_router_group, w_router_expert, b_router_expert,
           w_gate, w_up, w_down):
    b, t, d = x.shape
    x2 = x.reshape(t, d)
    layer = 0
    h = _rmsnorm(x2, attn_norm[layer])
    w_small = _small_weights(w_in[layer])
    (ps,) = _matmul(h, w_small, [F32], 1024, w_small.shape[1])
    pst = _matmul_nt(_ba_weights_t(w_in[layer]), h, 1024)
    out_a = _dsa_branch(h, w_in[layer], ps, positions, q_norm[layer], k_norm[layer], idx_k_norm[layer])
    out_b = _gdn_branch(h, w_in[layer], ps, pst, conv_w[layer], a_log[layer], dt_bias[layer], gdn_out_norm[layer])
    x1, h2 = _mix(x2, h, w_in[layer], out_a, out_b, w_o_a[layer], w_o_b[layer], w_out[layer], ffn_norm[layer])
    out = _moe(x1, h2, w_router_group[layer], b_router_group[layer], w_router_expert[layer], b_router_expert[layer],
               w_gate[layer], w_up[layer], w_down[layer])
    return out.reshape(b, t, d)
```

```python
import functools

import jax
import jax.numpy as jnp
from jax import lax
from jax.experimental import pallas as pl
from jax.experimental.pallas import tpu as pltpu

F32 = jnp.float32
BF16 = jnp.bfloat16
I32 = jnp.int32

EPS = 1e-6
NEG = -1e30
BIG = 1e30
LOG2_E = 1.4426950408889634
ROPE_THETA = 500000.0
CHUNK = 64
A_HEADS = 16
HEAD_DIM = 128
IDX_HEADS = 16
IDX_DIM = 64
TOPK_MAX = 256
Q_BLOCK = 128
B_QK_HEADS = 16
B_V_HEADS = 32
CONV_WIDTH = 4
N_GROUPS = 8
EXPERTS_PER_GROUP = 16
N_EXPERTS = 128
D_FF = 768
LANES = 128
VMEM_LIMIT = 56 * 1024 * 1024

NT_DIMS = (((1,), (1,)), ((), ()))


def _cp(*sem):
    return pltpu.CompilerParams(dimension_semantics=sem, vmem_limit_bytes=VMEM_LIMIT)


def _dot(a, b):
    return jnp.dot(a, b, preferred_element_type=F32)


def _dot_nt(a, b):
    return lax.dot_general(a, b, NT_DIMS, preferred_element_type=F32)


def _dot_hi(a, b):
    return jnp.dot(a, b, preferred_element_type=F32, precision=lax.Precision.HIGHEST)


def _sigmoid(x):
    return 1.0 / (1.0 + jnp.exp(-x))


def _softplus(x):
    return jnp.maximum(x, 0.0) + jnp.log(1.0 + jnp.exp(-jnp.abs(x)))


def _rmsnorm_kernel(x_ref, g_ref, o_ref):
    x = x_ref[...]
    ms = jnp.mean(x * x, axis=-1, keepdims=True)
    o_ref[...] = (x * lax.rsqrt(ms + EPS) * g_ref[...]).astype(o_ref.dtype)


def _rmsnorm(x, gain, out_dtype=BF16):
    t, d = x.shape
    tm = min(t, 512)
    return pl.pallas_call(
        _rmsnorm_kernel,
        grid=(t // tm,),
        in_specs=[pl.BlockSpec((tm, d), lambda i: (i, 0)), pl.BlockSpec((1, d), lambda i: (0, 0))],
        out_specs=pl.BlockSpec((tm, d), lambda i: (i, 0)),
        out_shape=jax.ShapeDtypeStruct((t, d), out_dtype),
        compiler_params=_cp("parallel"),
    )(x, gain.reshape(1, d))


def _mm_kernel(*refs, n_extra, epilogue):
    a_ref, b_ref = refs[:2]
    extra = refs[2:2 + n_extra]
    outs = refs[2 + n_extra:]
    acc = _dot(a_ref[...], b_ref[...])
    res = epilogue(acc, *[e[...] for e in extra]) if epilogue is not None else (acc,)
    for o_ref, r in zip(outs, res):
        o_ref[...] = r.astype(o_ref.dtype)


def _matmul(a, b, out_dtypes, tm, tn, epilogue=None, extras=()):
    m, k = a.shape
    n = b.shape[1]
    tm, tn = min(tm, m), min(tn, n)
    in_specs = [pl.BlockSpec((tm, k), lambda i, j: (i, 0)), pl.BlockSpec((k, tn), lambda i, j: (0, j))]
    args = [a, b]
    for arr, kind in extras:
        if kind == "tile":
            in_specs.append(pl.BlockSpec((tm, tn), lambda i, j: (i, j)))
        else:
            in_specs.append(pl.BlockSpec((1, tn), lambda i, j: (0, j)))
        args.append(arr)
    out = pl.pallas_call(
        functools.partial(_mm_kernel, n_extra=len(extras), epilogue=epilogue),
        grid=(m // tm, n // tn),
        in_specs=in_specs,
        out_specs=[pl.BlockSpec((tm, tn), lambda i, j: (i, j)) for _ in out_dtypes],
        out_shape=[jax.ShapeDtypeStruct((m, n), dt) for dt in out_dtypes],
        compiler_params=_cp("parallel", "arbitrary"),
    )(*args)
    return out


def _mm_nt_kernel(a_ref, b_ref, o_ref):
    res = _dot_nt(a_ref[...], b_ref[...])
    half = res.shape[0] // 2
    for c in range(o_ref.shape[0]):
        cols = slice(c * CHUNK, (c + 1) * CHUNK)
        o_ref[c] = jnp.concatenate([res[:half, cols], res[half:, cols]], axis=1)


def _matmul_nt(a, b, tm):
    n, k = a.shape
    m = b.shape[0]
    tm = min(tm, m)
    return pl.pallas_call(
        _mm_nt_kernel,
        grid=(m // tm,),
        in_specs=[pl.BlockSpec((n, k), lambda i: (0, 0)), pl.BlockSpec((tm, k), lambda i: (i, 0))],
        out_specs=pl.BlockSpec((tm // CHUNK, n // 2, 2 * CHUNK), lambda i: (i, 0, 0)),
        out_shape=jax.ShapeDtypeStruct((m // CHUNK, n // 2, 2 * CHUNK), F32),
        compiler_params=_cp("parallel"),
    )(a, b)


def _dsa_prep_kernel(p1_ref, p3_ref, ps_ref, pos_ref, ifa_ref, ifi_ref, qn_ref, kn_ref, ikn_ref,
                     q_ref, k_ref, qi_ref, ki_ref, wi_ref):
    tm = p1_ref.shape[0]
    pos = pos_ref[...].astype(F32)
    lane = lax.broadcasted_iota(I32, (tm, LANES), 1)
    ang_a = pos * ifa_ref[...]
    cos_a = jnp.cos(ang_a)
    sin_a = jnp.sin(ang_a)
    sin_a = jnp.where(lane < 16, -sin_a, sin_a)
    low_a = lane < 16

    def rope_a(x):
        partner = jnp.where(low_a, pltpu.roll(x, LANES - 16, 1), pltpu.roll(x, 16, 1))
        return x * cos_a + partner * sin_a

    d64 = lane & 63
    ang_i = pos * ifi_ref[...]
    cos_i = jnp.cos(ang_i)
    sin_i = jnp.sin(ang_i)
    low_i = d64 < 8
    sin_i = jnp.where(low_i, -sin_i, sin_i)

    def rope_i(x):
        partner = jnp.where(low_i, pltpu.roll(x, LANES - 8, 1), pltpu.roll(x, 8, 1))
        return x * cos_i + partner * sin_i

    qn = qn_ref[...]
    kn = kn_ref[...]
    scale = HEAD_DIM ** -0.5 * LOG2_E
    for h in range(A_HEADS):
        sl = slice(h * HEAD_DIM, (h + 1) * HEAD_DIM)
        xq = p1_ref[:, sl]
        yq = xq * lax.rsqrt(jnp.mean(xq * xq, axis=-1, keepdims=True) + EPS) * qn
        q_ref[:, sl] = (rope_a(yq) * scale).astype(q_ref.dtype)
        xk = p1_ref[:, A_HEADS * HEAD_DIM + h * HEAD_DIM:A_HEADS * HEAD_DIM + (h + 1) * HEAD_DIM]
        yk = xk * lax.rsqrt(jnp.mean(xk * xk, axis=-1, keepdims=True) + EPS) * kn
        k_ref[:, sl] = rope_a(yk).astype(k_ref.dtype)
    for j in range(IDX_HEADS * IDX_DIM // LANES):
        sl = slice(j * LANES, (j + 1) * LANES)
        qi_ref[:, sl] = rope_i(p3_ref[:, sl]).astype(qi_ref.dtype)
    xk = ps_ref[:, 0:LANES]
    ms = jnp.sum(xk * xk, axis=-1, keepdims=True) * (1.0 / IDX_DIM)
    yk = xk * lax.rsqrt(ms + EPS) * ikn_ref[...]
    ki_ref[...] = rope_i(yk).T[:IDX_DIM, :].astype(ki_ref.dtype)
    wi_ref[...] = ps_ref[:, LANES:LANES + IDX_HEADS] * (IDX_HEADS ** -0.5 * IDX_DIM ** -0.5)


def _dsa_prep(p1, p3, ps, positions, ifa, ifi, qn, kn, ikn):
    t = p1.shape[0]
    tm = min(t, 256)
    aw = A_HEADS * HEAD_DIM
    iq = IDX_HEADS * IDX_DIM
    row = lambda i: (i, 0)
    fix = lambda i: (0, 0)
    return pl.pallas_call(
        _dsa_prep_kernel,
        grid=(t // tm,),
        in_specs=[
            pl.BlockSpec((tm, 2 * aw), row), pl.BlockSpec((tm, iq), row), pl.BlockSpec((tm, ps.shape[1]), row),
            pl.BlockSpec((tm, 1), row), pl.BlockSpec((1, LANES), fix), pl.BlockSpec((1, LANES), fix),
            pl.BlockSpec((1, LANES), fix), pl.BlockSpec((1, LANES), fix), pl.BlockSpec((1, LANES), fix),
        ],
        out_specs=[
            pl.BlockSpec((tm, aw), row), pl.BlockSpec((tm, aw), row), pl.BlockSpec((tm, iq), row),
            pl.BlockSpec((IDX_DIM, tm), lambda i: (0, i)), pl.BlockSpec((tm, IDX_HEADS), row),
        ],
        out_shape=[
            jax.ShapeDtypeStruct((t, aw), BF16), jax.ShapeDtypeStruct((t, aw), BF16),
            jax.ShapeDtypeStruct((t, iq), BF16), jax.ShapeDtypeStruct((IDX_DIM, t), BF16),
            jax.ShapeDtypeStruct((t, IDX_HEADS), F32),
        ],
        compiler_params=_cp("parallel"),
    )(p1, p3, ps, positions, ifa, ifi, qn, kn, ikn)


def _indexer_kernel(qi_ref, w_ref, ki_ref, bias_ref, s_ref, wb_ref, qs_ref, *, kc, topk, maxit):
    i = pl.program_id(0)
    qb = Q_BLOCK
    n_ch = ((i + 1) * qb + kc - 1) // kc
    nslab = kc // LANES
    sub_kc = min(kc, 2 * LANES)
    for h in range(IDX_HEADS):
        wb_ref[h] = jnp.broadcast_to(w_ref[:, h:h + 1], (qb, LANES))
        qs_ref[h * qb:(h + 1) * qb, :] = qi_ref[:, h * IDX_DIM:(h + 1) * IDX_DIM]
    row = lax.broadcasted_iota(I32, (qb, LANES), 0)
    lane = lax.broadcasted_iota(I32, (qb, LANES), 1)
    limit = jnp.where(row < CHUNK, i * qb + CHUNK, (i + 1) * qb)

    def score_body(c, carry):
        mn, mx = carry
        off = pl.multiple_of(c * kc, kc)
        accs = []
        for sub in range(kc // sub_kc):
            kblk = ki_ref[:, pl.ds(pl.multiple_of(off + sub * sub_kc, sub_kc), sub_kc)]
            part = [jnp.zeros((qb, LANES), F32) for _ in range(sub_kc // LANES)]
            s_all = _dot(qs_ref[...], kblk)
            for h in range(IDX_HEADS):
                wbh = wb_ref[h]
                for j in range(sub_kc // LANES):
                    s = s_all[h * qb:(h + 1) * qb, j * LANES:(j + 1) * LANES]
                    part[j] = part[j] + wbh * jnp.maximum(s, 0.0)
            accs += part
        for j in range(nslab):
            adm = (lane + (off + j * LANES)) < limit
            val = jnp.where(adm, accs[j], NEG)
            s_ref[:, pl.ds(pl.multiple_of(off + j * LANES, LANES), LANES)] = val
            mn = jnp.minimum(mn, jnp.where(adm, accs[j], BIG))
            mx = jnp.maximum(mx, val)
        return mn, mx

    mn, mx = lax.fori_loop(0, n_ch, score_body,
                           (jnp.full((qb, LANES), BIG, F32), jnp.full((qb, LANES), NEG, F32)))
    lo0 = jnp.min(mn, axis=-1, keepdims=True)
    hi0 = jnp.max(mx, axis=-1, keepdims=True)
    kf = float(topk)

    def count_ge(thr):
        thr_b = jnp.broadcast_to(thr, (qb, LANES))

        def body(c, acc):
            off = pl.multiple_of(c * kc, kc)
            for j in range(nslab):
                sc = s_ref[:, pl.ds(pl.multiple_of(off + j * LANES, LANES), LANES)]
                acc = acc + jnp.where(sc >= thr_b, 1.0, 0.0)
            return acc

        acc = lax.fori_loop(0, n_ch, body, jnp.zeros((qb, LANES), F32))
        return jnp.sum(acc, axis=-1, keepdims=True)

    done0 = jnp.where(limit[:, 0:1] <= topk, 1.0, 0.0)

    def cond(st):
        it, _, _, done = st
        return jnp.logical_and(it < maxit, jnp.min(done) < 0.5)

    def body(st):
        it, lo, hi, done = st
        mid = lo + (hi - lo) * 0.5
        stuck = jnp.logical_or(mid <= lo, mid >= hi)
        c = count_ge(mid)
        ge = c >= kf
        lo = jnp.where(ge, mid, lo)
        hi = jnp.where(ge, hi, mid)
        done = jnp.maximum(done, jnp.where(jnp.logical_or(c == kf, stuck), 1.0, 0.0))
        return it + 1, lo, hi, done

    _, lo, hi, _ = lax.while_loop(cond, body, (jnp.int32(0), lo0, hi0, done0))
    c_hi = count_ge(hi)
    lo = jnp.where(c_hi >= kf, hi, lo)
    lo_b = jnp.broadcast_to(lo, (qb, LANES))

    bias_ref[...] = jnp.full(bias_ref.shape, NEG, bias_ref.dtype)

    def write_body(c, _):
        off = pl.multiple_of(c * kc, kc)
        for j in range(nslab):
            o = pl.multiple_of(off + j * LANES, LANES)
            sc = s_ref[:, pl.ds(o, LANES)]
            bias_ref[:, pl.ds(o, LANES)] = jnp.where(sc >= lo_b, 0.0, NEG).astype(bias_ref.dtype)
        return 0

    lax.fori_loop(0, n_ch, write_body, 0)


def _indexer(q_idx, w_idx, k_idx, topk):
    t = q_idx.shape[0]
    kc = min(t, 512)
    return pl.pallas_call(
        functools.partial(_indexer_kernel, kc=kc, topk=topk, maxit=64),
        grid=(t // Q_BLOCK,),
        in_specs=[
            pl.BlockSpec((Q_BLOCK, IDX_HEADS * IDX_DIM), lambda i: (i, 0)),
            pl.BlockSpec((Q_BLOCK, IDX_HEADS), lambda i: (i, 0)),
            pl.BlockSpec((IDX_DIM, t), lambda i: (0, 0)),
        ],
        out_specs=pl.BlockSpec((Q_BLOCK, t), lambda i: (i, 0)),
        out_shape=jax.ShapeDtypeStruct((t, t), BF16),
        scratch_shapes=[pltpu.VMEM((Q_BLOCK, t), F32), pltpu.VMEM((IDX_HEADS, Q_BLOCK, LANES), F32),
                        pltpu.VMEM((IDX_HEADS * Q_BLOCK, IDX_DIM), BF16)],
        compiler_params=_cp("parallel"),
    )(q_idx, w_idx, k_idx)


def _last_key_block(qi, bq, bk):
    return ((qi + 1) * bq - 1) // bk


def _attn_kernel(q_ref, k_ref, v_ref, b_ref, o_ref, acc_ref, m_ref, l_ref):
    qi = pl.program_id(0)
    ki = pl.program_id(1)
    last = _last_key_block(qi, q_ref.shape[0], k_ref.shape[0])

    @pl.when(ki == 0)
    def _():
        acc_ref[...] = jnp.zeros_like(acc_ref)
        m_ref[...] = jnp.full(m_ref.shape, NEG, F32)
        l_ref[...] = jnp.zeros_like(l_ref)

    @pl.when(ki <= last)
    def _():
        bias = b_ref[...].astype(F32)
        ones = jnp.ones((k_ref.shape[0], HEAD_DIM), BF16)
        for h in range(A_HEADS):
            sl = slice(h * HEAD_DIM, (h + 1) * HEAD_DIM)
            s = _dot_nt(q_ref[:, sl], k_ref[:, sl]) + bias
            m_prev = m_ref[h]
            m_new = jnp.maximum(m_prev, jnp.max(s, axis=-1, keepdims=True))
            alpha = jnp.exp2(m_prev - m_new)
            p = jnp.exp2(s - m_new[:, 0:1]).astype(BF16)
            pv = _dot(p, jnp.concatenate([v_ref[:, sl], ones], axis=1))
            l_ref[h] = alpha * l_ref[h] + pv[:, HEAD_DIM:]
            acc_ref[:, sl] = alpha * acc_ref[:, sl] + pv[:, :HEAD_DIM]
            m_ref[h] = m_new

    @pl.when(ki == last)
    def _():
        for h in range(A_HEADS):
            sl = slice(h * HEAD_DIM, (h + 1) * HEAD_DIM)
            o_ref[:, sl] = (acc_ref[:, sl] / l_ref[h]).astype(o_ref.dtype)


def _attention(q, k, v, bias):
    t, aw = q.shape
    bq = min(t, 512)
    bk = min(t, 1024)
    kv_map = lambda i, j: (jnp.minimum(j, _last_key_block(i, bq, bk)), 0)
    return pl.pallas_call(
        _attn_kernel,
        grid=(t // bq, t // bk),
        in_specs=[
            pl.BlockSpec((bq, aw), lambda i, j: (i, 0)),
            pl.BlockSpec((bk, aw), kv_map),
            pl.BlockSpec((bk, aw), kv_map),
            pl.BlockSpec((bq, bk), lambda i, j: (i, jnp.minimum(j, _last_key_block(i, bq, bk)))),
        ],
        out_specs=pl.BlockSpec((bq, aw), lambda i, j: (i, 0)),
        out_shape=jax.ShapeDtypeStruct((t, aw), BF16),
        scratch_shapes=[pltpu.VMEM((bq, aw), F32), pltpu.VMEM((A_HEADS, bq, LANES), F32),
                        pltpu.VMEM((A_HEADS, bq, LANES), F32)],
        compiler_params=_cp("parallel", "arbitrary"),
    )(q, k, v, bias)


def _gdn_prep_kernel(cur_ref, prev_ref, w_ref, o_ref, *, qk_blocks):
    i = pl.program_id(0)
    j = pl.program_id(1)
    tm = cur_ref.shape[0]
    cur = cur_ref[...]
    prev = prev_ref[...] * jnp.where(i > 0, 1.0, 0.0)
    xcat = jnp.concatenate([prev, cur], axis=0)
    y = cur * w_ref[CONV_WIDTH - 1:CONV_WIDTH, :]
    for tap in range(CONV_WIDTH - 1):
        y = y + xcat[8 - (CONV_WIDTH - 1) + tap:8 - (CONV_WIDTH - 1) + tap + tm, :] * w_ref[tap:tap + 1, :]
    y = y * _sigmoid(y)
    is_qk = j < 2 * qk_blocks
    qscale = jnp.where(j < qk_blocks, HEAD_DIM ** -0.5, 1.0)
    for h in range(cur.shape[1] // HEAD_DIM):
        yh = y[:, h * HEAD_DIM:(h + 1) * HEAD_DIM]
        nrm = lax.rsqrt(jnp.sum(yh * yh, axis=-1, keepdims=True) + EPS) * qscale
        o_ref[h] = jnp.where(is_qk, yh * nrm, yh).astype(o_ref.dtype)


def _gdn_prep(p4, conv_w):
    t, ch = p4.shape
    tm = min(t, 256)
    cb = 1024
    hb = cb // HEAD_DIM
    qk_blocks = B_QK_HEADS * HEAD_DIM // cb
    return pl.pallas_call(
        functools.partial(_gdn_prep_kernel, qk_blocks=qk_blocks),
        grid=(t // tm, ch // cb),
        in_specs=[
            pl.BlockSpec((tm, cb), lambda i, j: (i, j)),
            pl.BlockSpec((8, cb), lambda i, j: (jnp.maximum(i * (tm // 8) - 1, 0), j)),
            pl.BlockSpec((CONV_WIDTH, cb), lambda i, j: (0, j)),
        ],
        out_specs=pl.BlockSpec((hb, tm, HEAD_DIM), lambda i, j: (j, i, 0)),
        out_shape=jax.ShapeDtypeStruct((ch // HEAD_DIM, t, HEAD_DIM), BF16),
        compiler_params=_cp("parallel", "parallel"),
    )(p4, p4, conv_w)


N_PAIRS = B_V_HEADS // 2


def _tri_inverse_minus_eye(a, row, col, lane_lo):
    n = len(a)
    lane_hi = jnp.logical_not(lane_lo)

    def mm(x, y):
        bd = jnp.concatenate([jnp.where(lane_lo, y, 0.0), jnp.where(lane_hi, y, 0.0)], axis=0)
        return _dot(x.astype(BF16), bd.astype(BF16))

    blk16 = (row // 16) == (col // 16)
    blk32 = (row // 32) == (col // 32)
    d1 = [jnp.where(blk16, a[p], 0.0) for p in range(n)]
    e = [-d1[p] for p in range(n)]
    dk = d1
    for _ in range(3):
        dk = [mm(dk[p], dk[p]) for p in range(n)]
        de = [mm(dk[p], e[p]) for p in range(n)]
        e = [e[p] + dk[p] + de[p] for p in range(n)]
    for sel in (jnp.logical_and(blk32, jnp.logical_not(blk16)), jnp.logical_not(blk32)):
        lo = [jnp.where(sel, a[p], 0.0) for p in range(n)]
        m1 = [lo[p] + mm(e[p], lo[p]) for p in range(n)]
        m2 = [m1[p] + mm(m1[p], e[p]) for p in range(n)]
        e = [e[p] - m2[p] for p in range(n)]
    return e


def _gdn_a_kernel(qkv_ref, ps_ref, pst_ref, alr_ref, alc_ref, dtr_ref, dtc_ref,
                  wq_ref, u_ref, ak_ref, egl_ref):
    c = CHUNK
    pairs = range(N_PAIRS)
    row = lax.broadcasted_iota(I32, (c, LANES), 0)
    lane = lax.broadcasted_iota(I32, (c, LANES), 1)
    col = lane & (c - 1)
    lane_lo = lane < c
    lane_hi = jnp.logical_not(lane_lo)
    incl = row >= col
    strict = row > col
    eye2 = jnp.where(row == col, 1.0, 0.0)
    r64 = lax.broadcasted_iota(I32, (c, c), 0)
    c64 = lax.broadcasted_iota(I32, (c, c), 1)
    lower = jnp.where(r64 >= c64, 1.0, 0.0)
    r128 = lax.broadcasted_iota(I32, (LANES, LANES), 0)
    c128 = lax.broadcasted_iota(I32, (LANES, LANES), 1)
    upper2 = jnp.where(jnp.logical_and(r128 // c == c128 // c, r128 <= c128), 1.0, 0.0)

    beta_col = _sigmoid(ps_ref[:, 2 * LANES:2 * LANES + B_V_HEADS])
    g_col = -jnp.exp(alr_ref[...]) * _softplus(ps_ref[:, 3 * LANES:3 * LANES + B_V_HEADS] + dtr_ref[...])
    g_row2 = -jnp.exp(alc_ref[...]) * _softplus(pst_ref[0] + dtc_ref[...])
    gc_col = _dot_hi(lower, g_col)
    gc_row2 = _dot_hi(g_row2, upper2)
    eg_col = jnp.exp(gc_col)
    egl_ref[0] = eg_col[c - 1:c, :]

    def col_pair(m, p):
        return jnp.where(lane_lo, m[:, 2 * p:2 * p + 1], m[:, 2 * p + 1:2 * p + 2])

    qbf = [qkv_ref[p] for p in pairs]
    kbf = [qkv_ref[B_QK_HEADS + p] for p in pairs]
    q = [qbf[p].astype(F32) for p in pairs]
    k = [kbf[p].astype(F32) for p in pairs]
    g = [_dot_nt(jnp.concatenate([qbf[p], kbf[p]], axis=0),
                 jnp.concatenate([kbf[p], kbf[p]], axis=0)) for p in pairs]
    decay = [jnp.exp(jnp.where(incl, col_pair(gc_col, p) - gc_row2[p:p + 1, :], NEG)) for p in pairs]
    a_mat = [jnp.where(strict, col_pair(beta_col, p) * g[p][c:] * decay[p], 0.0) for p in pairs]
    attn = [g[p][:c] * decay[p] for p in pairs]
    e = _tri_inverse_minus_eye(a_mat, row, col, lane_lo)

    def rhs_of(p, h):
        bc = beta_col[:, h:h + 1]
        v = qkv_ref[2 * B_QK_HEADS + h].astype(F32)
        return jnp.concatenate([v * bc, k[p] * (bc * eg_col[:, h:h + 1])], axis=1)

    rhs = [jnp.concatenate([rhs_of(p, 2 * p), rhs_of(p, 2 * p + 1)], axis=0).astype(BF16) for p in pairs]
    t_mat = [eye2 + e[p] for p in pairs]
    uw = [[_dot(jnp.where(sel, t_mat[p], 0.0).astype(BF16), rhs[p]) for sel in (lane_lo, lane_hi)] for p in pairs]
    for p in pairs:
        kd = []
        for r in range(2):
            h = 2 * p + r
            u_ref[h] = uw[p][r][:, :HEAD_DIM]
            qe = q[p] * eg_col[:, h:h + 1]
            wq_ref[h] = jnp.concatenate([uw[p][r][:, HEAD_DIM:], qe], axis=0).astype(wq_ref.dtype)
            kd.append(k[p] * jnp.exp(gc_col[c - 1:c, h:h + 1] - gc_col[:, h:h + 1]))
        kdt = jnp.concatenate(kd, axis=0).T
        ak_ref[p] = jnp.concatenate([attn[p], kdt], axis=0).astype(ak_ref.dtype)


def _gdn_phase_a(qkv, ps, pst, a_log, dt_bias):
    nh, t, _ = qkv.shape
    n = t // CHUNK
    hv = B_V_HEADS
    fix2 = lambda i: (0, 0)
    return pl.pallas_call(
        _gdn_a_kernel,
        grid=(n,),
        in_specs=[
            pl.BlockSpec((nh, CHUNK, HEAD_DIM), lambda i: (0, i, 0)),
            pl.BlockSpec((CHUNK, ps.shape[1]), lambda i: (i, 0)),
            pl.BlockSpec((1, N_PAIRS, LANES), lambda i: (i, 0, 0)),
            pl.BlockSpec((1, hv), fix2), pl.BlockSpec((N_PAIRS, LANES), fix2),
            pl.BlockSpec((1, hv), fix2), pl.BlockSpec((N_PAIRS, LANES), fix2),
        ],
        out_specs=[
            pl.BlockSpec((hv, 2 * CHUNK, HEAD_DIM), lambda i: (0, i, 0)),
            pl.BlockSpec((hv, CHUNK, HEAD_DIM), lambda i: (0, i, 0)),
            pl.BlockSpec((N_PAIRS, CHUNK + HEAD_DIM, LANES), lambda i: (0, i, 0)),
            pl.BlockSpec((1, 1, hv), lambda i: (i, 0, 0)),
        ],
        out_shape=[
            jax.ShapeDtypeStruct((hv, 2 * t, HEAD_DIM), BF16),
            jax.ShapeDtypeStruct((hv, t, HEAD_DIM), F32),
            jax.ShapeDtypeStruct((N_PAIRS, n * (CHUNK + HEAD_DIM), LANES), BF16),
            jax.ShapeDtypeStruct((n, 1, hv), F32),
        ],
        compiler_params=_cp("parallel"),
    )(qkv, ps, pst, a_log.reshape(1, hv), _pair_lanes(a_log), dt_bias.reshape(1, hv), _pair_lanes(dt_bias))


def _pair_lanes(per_head):
    return jnp.repeat(per_head.reshape(N_PAIRS, 2), CHUNK, axis=1)


def _gdn_b_kernel(wq_ref, u_ref, ak_ref, egl_ref, z_ref, gn_ref, o_ref, s_ref):
    c = CHUNK

    @pl.when(pl.program_id(0) == 0)
    def _():
        s_ref[...] = jnp.zeros_like(s_ref)

    gn = gn_ref[...]
    heads = range(B_V_HEADS)
    lane_lo = lax.broadcasted_iota(I32, (c + HEAD_DIM, LANES), 1) < c
    lane_hi = jnp.logical_not(lane_lo)
    zero = jnp.zeros((), ak_ref.dtype)
    s = [s_ref[h] for h in heads]
    r1 = [_dot(wq_ref[h], s[h].astype(BF16)) for h in heads]
    vn = [u_ref[h] - r1[h][:c] for h in heads]
    r2 = []
    for p in range(N_PAIRS):
        vst = jnp.concatenate([vn[2 * p], vn[2 * p + 1]], axis=0).astype(BF16)
        akp = ak_ref[p]
        r2.append(_dot(jnp.where(lane_lo, akp, zero), vst))
        r2.append(_dot(jnp.where(lane_hi, akp, zero), vst))
    for h in heads:
        sl = slice(h * HEAD_DIM, (h + 1) * HEAD_DIM)
        o = r1[h][c:] + r2[h][:c]
        s_ref[h] = s[h] * egl_ref[0, :, h:h + 1] + r2[h][c:]
        on = o * lax.rsqrt(jnp.mean(o * o, axis=-1, keepdims=True) + EPS) * gn
        z = z_ref[:, sl].astype(F32)
        o_ref[:, sl] = (on * (z * _sigmoid(z))).astype(o_ref.dtype)


def _gdn_phase_b(wq, u, ak, egl, z, out_norm):
    hv, t, _ = u.shape
    n = t // CHUNK
    return pl.pallas_call(
        _gdn_b_kernel,
        grid=(n,),
        in_specs=[
            pl.BlockSpec((hv, 2 * CHUNK, HEAD_DIM), lambda i: (0, i, 0)),
            pl.BlockSpec((hv, CHUNK, HEAD_DIM), lambda i: (0, i, 0)),
            pl.BlockSpec((N_PAIRS, CHUNK + HEAD_DIM, LANES), lambda i: (0, i, 0)),
            pl.BlockSpec((1, 1, hv), lambda i: (i, 0, 0)),
            pl.BlockSpec((CHUNK, hv * HEAD_DIM), lambda i: (i, 0)),
            pl.BlockSpec((1, HEAD_DIM), lambda i: (0, 0)),
        ],
        out_specs=pl.BlockSpec((CHUNK, hv * HEAD_DIM), lambda i: (i, 0)),
        out_shape=jax.ShapeDtypeStruct((t, hv * HEAD_DIM), BF16),
        scratch_shapes=[pltpu.VMEM((hv, HEAD_DIM, HEAD_DIM), F32)],
        compiler_params=_cp("arbitrary"),
    )(wq, u, ak, egl, z, out_norm.reshape(1, HEAD_DIM))


_A_W = A_HEADS * HEAD_DIM
_A_IQ = IDX_HEADS * IDX_DIM
_B_QK = B_QK_HEADS * HEAD_DIM
_B_V = B_V_HEADS * HEAD_DIM
_OFF_AIK = 3 * _A_W + _A_IQ
_OFF_AIW = _OFF_AIK + IDX_DIM
_OFF_BQ = _OFF_AIW + IDX_HEADS
_OFF_BZ = _OFF_BQ + 2 * _B_QK + _B_V
_OFF_BB = _OFF_BZ + _B_V
_OFF_BA = _OFF_BB + B_V_HEADS
_OFF_GATE = _OFF_BA + B_V_HEADS


def _small_weights(w_in):
    d = w_in.shape[0]
    pad = lambda a: jnp.pad(a, ((0, 0), (0, LANES - a.shape[1])))
    return jnp.concatenate([
        pad(w_in[:, _OFF_AIK:_OFF_AIK + IDX_DIM]), pad(w_in[:, _OFF_AIW:_OFF_AIW + IDX_HEADS]),
        pad(w_in[:, _OFF_BB:_OFF_BB + B_V_HEADS]), pad(w_in[:, _OFF_BA:_OFF_BA + B_V_HEADS]),
    ], axis=1).astype(BF16)


def _ba_weights_t(w_in):
    w = w_in[:, _OFF_BA:_OFF_BA + B_V_HEADS]
    return jnp.concatenate([w[:, 0::2], w[:, 1::2]], axis=1).T.astype(BF16)


def _rope_freqs(rot, width):
    half = rot // 2
    inv = jnp.power(ROPE_THETA, -jnp.arange(half, dtype=F32) * 2.0 / rot)
    pat = jnp.concatenate([inv, inv, jnp.zeros((width - rot,), F32)])
    return jnp.tile(pat, LANES // width).reshape(1, LANES)


def _dsa_branch(h, w_in, ps, positions, q_norm, k_norm, idx_k_norm):
    t = h.shape[0]
    tm = 1024
    (p1,) = _matmul(h, w_in[:, 0:2 * _A_W].astype(BF16), [F32], tm, 1024)
    (v,) = _matmul(h, w_in[:, 2 * _A_W:3 * _A_W].astype(BF16), [BF16], tm, 1024)
    (p3,) = _matmul(h, w_in[:, 3 * _A_W:3 * _A_W + _A_IQ].astype(BF16), [F32], tm, 1024)
    ikn = jnp.pad(idx_k_norm.reshape(1, IDX_DIM), ((0, 0), (0, LANES - IDX_DIM)))
    q, k, q_idx, k_idx, w_idx = _dsa_prep(
        p1, p3, ps, positions.reshape(t, 1), _rope_freqs(HEAD_DIM // 4, HEAD_DIM), _rope_freqs(IDX_DIM // 4, IDX_DIM),
        q_norm.reshape(1, HEAD_DIM), k_norm.reshape(1, HEAD_DIM), ikn)
    bias = _indexer(q_idx, w_idx, k_idx, min(TOPK_MAX, t // 4))
    return _attention(q, k, v, bias)


def _gdn_branch(h, w_in, ps, pst, conv_w, a_log, dt_bias, out_norm):
    tm = 1024
    (p4,) = _matmul(h, w_in[:, _OFF_BQ:_OFF_BZ].astype(BF16), [F32], tm, 1024)
    (z,) = _matmul(h, w_in[:, _OFF_BZ:_OFF_BB].astype(BF16), [BF16], tm, 1024)
    qkv = _gdn_prep(p4, conv_w)
    wq, u, ak, egl = _gdn_phase_a(qkv, ps, pst, a_log, dt_bias)
    return _gdn_phase_b(wq, u, ak, egl, z, out_norm)


def _gate_a_epilogue(acc, g):
    return (_sigmoid(g.astype(F32)) * acc,)


def _gate_b_epilogue(acc, g, m):
    return (m + _sigmoid(g.astype(F32)) * acc,)


def _residual_norm_epilogue(acc, x, gain):
    x1 = x + acc
    h2 = x1 * lax.rsqrt(jnp.mean(x1 * x1, axis=-1, keepdims=True) + EPS) * gain
    return x1, h2


def _mix(x, h, w_in, out_a, out_b, w_o_a, w_o_b, w_out, ffn_norm):
    d = x.shape[1]
    (ga,) = _matmul(h, w_in[:, _OFF_GATE:_OFF_GATE + d].astype(BF16), [BF16], 1024, 1024)
    (gb,) = _matmul(h, w_in[:, _OFF_GATE + d:_OFF_GATE + 2 * d].astype(BF16), [BF16], 1024, 1024)
    (mixa,) = _matmul(out_a, w_o_a.astype(BF16), [F32], 1024, 1024, _gate_a_epilogue, [(ga, "tile")])
    (mix,) = _matmul(out_b, w_o_b.astype(BF16), [BF16], 1024, 512, _gate_b_epilogue, [(gb, "tile"), (mixa, "tile")])
    return _matmul(mix, w_out.astype(BF16), [F32, F32], 512, d, _residual_norm_epilogue,
                   [(x, "tile"), (ffn_norm.reshape(1, d), "row")])


MOE_TM = 256
MOE_NF = 2
_R_E1, _R_E2, _R_RANK1, _R_RANK2, _R_W1, _R_W2 = 0, 1, 2, 3, 4, 5


def _router_kernel(h_ref, w_ref, b_ref, r_ref, cnt_ref, carry_ref):
    i = pl.program_id(0)

    @pl.when(i == 0)
    def _():
        carry_ref[...] = jnp.zeros_like(carry_ref)

    tb = h_ref.shape[0]
    logits = _dot(h_ref[...].astype(BF16), w_ref[...]) + b_ref[...]
    le = logits[:, :N_EXPERTS]
    lg = logits[:, N_EXPERTS:]
    lane = lax.broadcasted_iota(I32, (tb, LANES), 1)
    lanef = lane.astype(F32)
    far = float(4 * LANES)

    lgm = jnp.where(lane < N_GROUPS, lg, NEG)
    gmax = jnp.max(lgm, axis=-1, keepdims=True)
    g_idx = jnp.min(jnp.where(lgm == gmax, lanef, far), axis=-1, keepdims=True)
    g_top = 1.0 / jnp.sum(jnp.exp(lgm - gmax), axis=-1, keepdims=True)

    in_grp = (lane // EXPERTS_PER_GROUP).astype(F32) == g_idx
    lem = jnp.where(in_grp, le, NEG)
    e1 = jnp.max(lem, axis=-1, keepdims=True)
    i1 = jnp.min(jnp.where(lem == e1, lanef, far), axis=-1, keepdims=True)
    lem2 = jnp.where(lanef == i1, NEG, lem)
    e2 = jnp.max(lem2, axis=-1, keepdims=True)
    i2 = jnp.min(jnp.where(lem2 == e2, lanef, far), axis=-1, keepdims=True)
    se = jnp.sum(jnp.exp(lem - e1), axis=-1, keepdims=True)
    p1 = 1.0 / se
    p2 = jnp.exp(e2 - e1) / se
    w1 = g_top * p1 / (p1 + p2)
    w2 = g_top * p2 / (p1 + p2)

    o1 = jnp.where(lanef == i1, 1.0, 0.0)
    o2 = jnp.where(lanef == i2, 1.0, 0.0)
    osum = o1 + o2
    rr = lax.broadcasted_iota(I32, (tb, tb), 0)
    cc = lax.broadcasted_iota(I32, (tb, tb), 1)
    before = jnp.where(cc < rr, 1.0, 0.0).astype(BF16)
    prefix = _dot(before, osum.astype(BF16)) + carry_ref[0:1, :]
    rank1 = jnp.sum(prefix * o1, axis=-1, keepdims=True)
    rank2 = jnp.sum(prefix * o2, axis=-1, keepdims=True)
    carry_ref[...] = carry_ref[...] + jnp.sum(osum, axis=0, keepdims=True)
    cnt_ref[...] = carry_ref[...]

    rec = jnp.zeros((tb, LANES), F32)
    for idx, val in ((_R_E1, i1), (_R_E2, i2), (_R_RANK1, rank1), (_R_RANK2, rank2), (_R_W1, w1), (_R_W2, w2)):
        rec = jnp.where(lane == idx, val, rec)
    r_ref[...] = rec


def _router(h2, w_router, b_router):
    t, d = h2.shape
    tb = min(t, 512)
    return pl.pallas_call(
        _router_kernel,
        grid=(t // tb,),
        in_specs=[pl.BlockSpec((tb, d), lambda i: (i, 0)), pl.BlockSpec((d, 2 * LANES), lambda i: (0, 0)),
                  pl.BlockSpec((1, 2 * LANES), lambda i: (0, 0))],
        out_specs=[pl.BlockSpec((tb, LANES), lambda i: (i, 0)), pl.BlockSpec((8, LANES), lambda i: (0, 0))],
        out_shape=[jax.ShapeDtypeStruct((t, LANES), F32), jax.ShapeDtypeStruct((8, LANES), F32)],
        scratch_shapes=[pltpu.VMEM((8, LANES), F32)],
        compiler_params=_cp("arbitrary"),
    )(h2, w_router, b_router)


def _positions_kernel(r_ref, cnt_ref, pos_ref, te_ref, nv_ref):
    t = r_ref.shape[0]
    nt = te_ref.shape[0]
    cnt = cnt_ref[...]
    tiles = jnp.floor((cnt + (MOE_TM - 1)) * (1.0 / MOE_TM))
    rr = lax.broadcasted_iota(I32, (LANES, LANES), 0)
    cc = lax.broadcasted_iota(I32, (LANES, LANES), 1)
    start_tiles = _dot(tiles.astype(BF16), jnp.where(rr < cc, 1.0, 0.0).astype(BF16))
    start = start_tiles[0:1, :] * float(MOE_TM)
    rec = r_ref[...]
    lane = lax.broadcasted_iota(I32, (t, LANES), 1)
    lanef = lane.astype(F32)
    pos1 = jnp.sum(jnp.where(lanef == rec[:, _R_E1:_R_E1 + 1], start, 0.0), axis=-1, keepdims=True) \
        + rec[:, _R_RANK1:_R_RANK1 + 1]
    pos2 = jnp.sum(jnp.where(lanef == rec[:, _R_E2:_R_E2 + 1], start, 0.0), axis=-1, keepdims=True) \
        + rec[:, _R_RANK2:_R_RANK2 + 1]
    pos_ref[...] = jnp.where(lane == 0, pos1, jnp.where(lane == 1, pos2, 0.0)).astype(I32)
    tile_id = lax.broadcasted_iota(I32, (nt, LANES), 0).astype(F32)
    tlane = lax.broadcasted_iota(I32, (nt, LANES), 1)
    owner = jnp.sum(jnp.where(start_tiles[0:1, :] <= tile_id, 1.0, 0.0), axis=-1, keepdims=True) - 1.0
    is_owner = tlane.astype(F32) == owner
    own_cnt = jnp.sum(jnp.where(is_owner, cnt[0:1, :], 0.0), axis=-1, keepdims=True)
    own_start = jnp.sum(jnp.where(is_owner, start_tiles[0:1, :], 0.0), axis=-1, keepdims=True)
    rows = jnp.clip(own_cnt - (tile_id[:, 0:1] - own_start) * float(MOE_TM), 0.0, float(MOE_TM))
    te_ref[...] = jnp.where(tlane == 0, owner, jnp.where(tlane == 1, rows, 0.0)).astype(I32)
    nv_ref[...] = jnp.broadcast_to(jnp.sum(tiles[0:1, :], axis=-1, keepdims=True), (8, LANES)).astype(I32)


def _positions(rec, cnt, n_tiles):
    t = rec.shape[0]
    full = lambda shape: pl.BlockSpec(shape, lambda i: (0, 0))
    return pl.pallas_call(
        _positions_kernel,
        grid=(1,),
        in_specs=[full((t, LANES)), full((8, LANES))],
        out_specs=[full((t, LANES)), full((n_tiles, LANES)), full((8, LANES))],
        out_shape=[jax.ShapeDtypeStruct((t, LANES), I32), jax.ShapeDtypeStruct((n_tiles, LANES), I32),
                   jax.ShapeDtypeStruct((8, LANES), I32)],
        compiler_params=_cp("arbitrary"),
    )(rec, cnt)


def _invert_kernel(pos_ref, asg_ref):
    def scatter(a, _):
        asg_ref[pos_ref[a]] = a
        return 0

    lax.fori_loop(0, pos_ref.shape[0], scatter, 0, unroll=8)


def _invert(pos, n_slots):
    return pl.pallas_call(
        _invert_kernel,
        in_specs=[pl.BlockSpec(memory_space=pltpu.SMEM)],
        out_specs=pl.BlockSpec(memory_space=pltpu.SMEM),
        out_shape=jax.ShapeDtypeStruct((n_slots,), I32),
    )(pos)


def _row_copy(src_ref, src_row, dst_ref, dst_row, sem):
    return pltpu.make_async_copy(src_ref.at[pl.ds(src_row, 1)], dst_ref.at[pl.ds(dst_row, 1)], sem)


def _ffn_kernel(te_ref, rows_ref, nv_ref, asg_ref, h_ref, wg_ref, wu_ref, wd_ref, y_ref,
                xbuf_ref, x16_ref, obuf_ref, gsem, ssem):
    i = pl.program_id(0)
    f = pl.program_id(1)
    tm = x16_ref.shape[0]
    nv = nv_ref[0]
    last_f = pl.num_programs(1) - 1

    def gather(tile, slot):
        def body(r, _):
            _row_copy(h_ref, asg_ref[tile * tm + r] // 2, xbuf_ref.at[slot], r, gsem.at[slot]).start()
            return 0

        lax.fori_loop(0, rows_ref[tile], body, 0)

    def scatter_wait(tile):
        def body(r, _):
            _row_copy(obuf_ref, r, y_ref, 0, ssem).wait()
            return 0

        lax.fori_loop(0, rows_ref[tile], body, 0)

    @pl.when(jnp.logical_and(i == 0, f == 0))
    def _():
        xbuf_ref[...] = jnp.zeros_like(xbuf_ref)
        gather(0, 0)

    @pl.when(jnp.logical_and(f == 0, i + 1 < nv))
    def _():
        gather(i + 1, (i + 1) % 2)

    @pl.when(jnp.logical_and(f == 0, i < nv))
    def _():
        slot = i % 2

        def drain(r, _):
            _row_copy(h_ref, 0, xbuf_ref.at[slot], r, gsem.at[slot]).wait()
            return 0

        lax.fori_loop(0, rows_ref[i], drain, 0)
        x16_ref[...] = xbuf_ref[slot].astype(x16_ref.dtype)

    @pl.when(i < nv)
    def _():
        x = x16_ref[...]
        g = _dot(x, wg_ref[0].astype(BF16))
        u = _dot(x, wu_ref[0].astype(BF16))
        act = (g * _sigmoid(g)) * u
        y = _dot(act.astype(BF16), wd_ref[0].astype(BF16))

        @pl.when(f == 0)
        def _():
            @pl.when(i > 0)
            def _():
                scatter_wait(i - 1)

            obuf_ref[...] = y

        @pl.when(f > 0)
        def _():
            obuf_ref[...] = obuf_ref[...] + y

        @pl.when(f == last_f)
        def _():
            def body(r, _):
                _row_copy(obuf_ref, r, y_ref, asg_ref[i * tm + r], ssem).start()
                return 0

            lax.fori_loop(0, rows_ref[i], body, 0)

            @pl.when(i == nv - 1)
            def _():
                scatter_wait(i)


def _ffn(te, rows, nv, asg, h2, w_gate, w_up, w_down, n_tiles):
    t, d = h2.shape
    fb = D_FF // MOE_NF

    def tile(i, nv):
        return jnp.minimum(i, nv[0] - 1)

    def fblk(i, f, nv):
        return jnp.where(i < nv[0], f, MOE_NF - 1)

    any_spec = pl.BlockSpec(memory_space=pl.ANY)
    return pl.pallas_call(
        _ffn_kernel,
        grid_spec=pltpu.PrefetchScalarGridSpec(
            num_scalar_prefetch=4,
            grid=(n_tiles, MOE_NF),
            in_specs=[
                any_spec,
                pl.BlockSpec((1, d, fb), lambda i, f, te, rows, nv, asg: (te[tile(i, nv)], 0, fblk(i, f, nv))),
                pl.BlockSpec((1, d, fb), lambda i, f, te, rows, nv, asg: (te[tile(i, nv)], 0, fblk(i, f, nv))),
                pl.BlockSpec((1, fb, d), lambda i, f, te, rows, nv, asg: (te[tile(i, nv)], fblk(i, f, nv), 0)),
            ],
            out_specs=any_spec,
            scratch_shapes=[pltpu.VMEM((2, MOE_TM, d), F32), pltpu.VMEM((MOE_TM, d), BF16),
                            pltpu.VMEM((MOE_TM, d), F32), pltpu.SemaphoreType.DMA((2,)),
                            pltpu.SemaphoreType.DMA(())],
        ),
        out_shape=jax.ShapeDtypeStruct((2 * t, d), F32),
        compiler_params=_cp("arbitrary", "arbitrary"),
    )(te, rows, nv, asg, h2, w_gate, w_up, w_down)


def _combine_kernel(y_ref, x_ref, r_ref, o_ref):
    d = x_ref.shape[1]
    rec = r_ref[...]
    o_ref[...] = x_ref[...] + rec[:, _R_W1:_R_W1 + 1] * y_ref[:, :d] + rec[:, _R_W2:_R_W2 + 1] * y_ref[:, d:]


def _combine(y, x1, rec):
    t, d = x1.shape
    tb = min(t, 256)
    return pl.pallas_call(
        _combine_kernel,
        grid=(t // tb,),
        in_specs=[pl.BlockSpec((tb, 2 * d), lambda i: (i, 0)), pl.BlockSpec((tb, d), lambda i: (i, 0)),
                  pl.BlockSpec((tb, LANES), lambda i: (i, 0))],
        out_specs=pl.BlockSpec((tb, d), lambda i: (i, 0)),
        out_shape=jax.ShapeDtypeStruct((t, d), F32),
        compiler_params=_cp("parallel"),
    )(y.reshape(t, 2 * d), x1, rec)


def _moe(x1, h2, w_rg, b_rg, w_re, b_re, w_gate, w_up, w_down):
    t, d = x1.shape
    pad_w = jnp.zeros((d, LANES - N_GROUPS), F32)
    w_router = jnp.concatenate([w_re, w_rg, pad_w], axis=1).astype(BF16)
    b_router = jnp.concatenate([b_re, b_rg, jnp.zeros((LANES - N_GROUPS,), F32)]).reshape(1, 2 * LANES)
    n_tiles = 2 * t // MOE_TM + N_EXPERTS
    rec, cnt = _router(h2, w_router, b_router)
    pos2d, te2d, nv2d = _positions(rec, cnt, n_tiles)
    pos = pos2d[:, :2].reshape(2 * t)
    te = te2d[:, 0]
    rows = te2d[:, 1]
    nv = nv2d[0, :1]
    asg = _invert(pos, n_tiles * MOE_TM)
    y = _ffn(te, rows, nv, asg, h2, w_gate, w_up, w_down, n_tiles)
    return _combine(y, x1, rec)


def kernel(x, positions, attn_norm, w_in, q_norm, k_norm, idx_k_norm, conv_w, a_log, dt_bias, gdn_out_norm,
           w_o_a, w_o_b, w_out, ffn_norm, w_router_group, b_router_group, w_router_expert, b_router_expert,
           w_gate, w_up, w_down):
    b, t, d = x.shape
    x2 = x.reshape(t, d)
    layer = 0
    h = _rmsnorm(x2, attn_norm[layer])
    w_small = _small_weights(w_in[layer])
    (ps,) = _matmul(h, w_small, [F32], 1024, w_small.shape[1])
    pst = _matmul_nt(_ba_weights_t(w_in[layer]), h, 1024)
    out_a = _dsa_branch(h, w_in[layer], ps, positions, q_norm[layer], k_norm[layer], idx_k_norm[layer])
    out_b = _gdn_branch(h, w_in[layer], ps, pst, conv_w[layer], a_log[layer], dt_bias[layer], gdn_out_norm[layer])
    x1, h2 = _mix(x2, h, w_in[layer], out_a, out_b, w_o_a[layer], w_o_b[layer], w_out[layer], ffn_norm[layer])
    out = _moe(x1, h2, w_router_group[layer], b_router_group[layer], w_router_expert[layer], b_router_expert[layer],
               w_gate[layer], w_up[layer], w_down[layer])
    return out.reshape(b, t, d)
```

```python
import functools

import jax
import jax.numpy as jnp
from jax import lax
from jax.experimental import pallas as pl
from jax.experimental.pallas import tpu as pltpu

F32 = jnp.float32
BF16 = jnp.bfloat16
I32 = jnp.int32

EPS = 1e-6
NEG = -1e30
BIG = 1e30
LOG2_E = 1.4426950408889634
ROPE_THETA = 500000.0
CHUNK = 64
A_HEADS = 16
HEAD_DIM = 128
IDX_HEADS = 16
IDX_DIM = 64
TOPK_MAX = 256
Q_BLOCK = 128
B_QK_HEADS = 16
B_V_HEADS = 32
CONV_WIDTH = 4
N_GROUPS = 8
EXPERTS_PER_GROUP = 16
N_EXPERTS = 128
D_FF = 768
LANES = 128
VMEM_LIMIT = 56 * 1024 * 1024

NT_DIMS = (((1,), (1,)), ((), ()))


def _cp(*sem):
    return pltpu.CompilerParams(dimension_semantics=sem, vmem_limit_bytes=VMEM_LIMIT)


def _dot(a, b):
    return jnp.dot(a, b, preferred_element_type=F32)


def _dot_nt(a, b):
    return lax.dot_general(a, b, NT_DIMS, preferred_element_type=F32)


def _dot_hi(a, b):
    return jnp.dot(a, b, preferred_element_type=F32, precision=lax.Precision.HIGHEST)


def _sigmoid(x):
    return 1.0 / (1.0 + jnp.exp(-x))


def _softplus(x):
    return jnp.maximum(x, 0.0) + jnp.log(1.0 + jnp.exp(-jnp.abs(x)))


def _rmsnorm_kernel(x_ref, g_ref, o_ref):
    x = x_ref[...]
    ms = jnp.mean(x * x, axis=-1, keepdims=True)
    o_ref[...] = (x * lax.rsqrt(ms + EPS) * g_ref[...]).astype(o_ref.dtype)


def _rmsnorm(x, gain, out_dtype=BF16):
    t, d = x.shape
    tm = min(t, 512)
    return pl.pallas_call(
        _rmsnorm_kernel,
        grid=(t // tm,),
        in_specs=[pl.BlockSpec((tm, d), lambda i: (i, 0)), pl.BlockSpec((1, d), lambda i: (0, 0))],
        out_specs=pl.BlockSpec((tm, d), lambda i: (i, 0)),
        out_shape=jax.ShapeDtypeStruct((t, d), out_dtype),
        compiler_params=_cp("parallel"),
    )(x, gain.reshape(1, d))


def _mm_kernel(*refs, n_extra, epilogue):
    a_ref, b_ref = refs[:2]
    extra = refs[2:2 + n_extra]
    outs = refs[2 + n_extra:]
    acc = _dot(a_ref[...], b_ref[...])
    res = epilogue(acc, *[e[...] for e in extra]) if epilogue is not None else (acc,)
    for o_ref, r in zip(outs, res):
        o_ref[...] = r.astype(o_ref.dtype)


def _matmul(a, b, out_dtypes, tm, tn, epilogue=None, extras=()):
    m, k = a.shape
    n = b.shape[1]
    tm, tn = min(tm, m), min(tn, n)
    in_specs = [pl.BlockSpec((tm, k), lambda i, j: (i, 0)), pl.BlockSpec((k, tn), lambda i, j: (0, j))]
    args = [a, b]
    for arr, kind in extras:
        if kind == "tile":
            in_specs.append(pl.BlockSpec((tm, tn), lambda i, j: (i, j)))
        else:
            in_specs.append(pl.BlockSpec((1, tn), lambda i, j: (0, j)))
        args.append(arr)
    out = pl.pallas_call(
        functools.partial(_mm_kernel, n_extra=len(extras), epilogue=epilogue),
        grid=(m // tm, n // tn),
        in_specs=in_specs,
        out_specs=[pl.BlockSpec((tm, tn), lambda i, j: (i, j)) for _ in out_dtypes],
        out_shape=[jax.ShapeDtypeStruct((m, n), dt) for dt in out_dtypes],
        compiler_params=_cp("parallel", "arbitrary"),
    )(*args)
    return out


def _mm_nt_kernel(a_ref, b_ref, o_ref):
    res = _dot_nt(a_ref[...], b_ref[...])
    half = res.shape[0] // 2
    for c in range(o_ref.shape[0]):
        cols = slice(c * CHUNK, (c + 1) * CHUNK)
        o_ref[c] = jnp.concatenate([res[:half, cols], res[half:, cols]], axis=1)


def _matmul_nt(a, b, tm):
    n, k = a.shape
    m = b.shape[0]
    tm = min(tm, m)
    return pl.pallas_call(
        _mm_nt_kernel,
        grid=(m // tm,),
        in_specs=[pl.BlockSpec((n, k), lambda i: (0, 0)), pl.BlockSpec((tm, k), lambda i: (i, 0))],
        out_specs=pl.BlockSpec((tm // CHUNK, n // 2, 2 * CHUNK), lambda i: (i, 0, 0)),
        out_shape=jax.ShapeDtypeStruct((m // CHUNK, n // 2, 2 * CHUNK), F32),
        compiler_params=_cp("parallel"),
    )(a, b)


def _dsa_prep_kernel(p1_ref, p3_ref, ps_ref, pos_ref, ifa_ref, ifi_ref, qn_ref, kn_ref, ikn_ref,
                     q_ref, k_ref, qi_ref, ki_ref, wi_ref):
    tm = p1_ref.shape[0]
    pos = pos_ref[...].astype(F32)
    lane = lax.broadcasted_iota(I32, (tm, LANES), 1)
    ang_a = pos * ifa_ref[...]
    cos_a = jnp.cos(ang_a)
    sin_a = jnp.sin(ang_a)
    sin_a = jnp.where(lane < 16, -sin_a, sin_a)
    low_a = lane < 16

    def rope_a(x):
        partner = jnp.where(low_a, pltpu.roll(x, LANES - 16, 1), pltpu.roll(x, 16, 1))
        return x * cos_a + partner * sin_a

    d64 = lane & 63
    ang_i = pos * ifi_ref[...]
    cos_i = jnp.cos(ang_i)
    sin_i = jnp.sin(ang_i)
    low_i = d64 < 8
    sin_i = jnp.where(low_i, -sin_i, sin_i)

    def rope_i(x):
        partner = jnp.where(low_i, pltpu.roll(x, LANES - 8, 1), pltpu.roll(x, 8, 1))
        return x * cos_i + partner * sin_i

    qn = qn_ref[...]
    kn = kn_ref[...]
    scale = HEAD_DIM ** -0.5 * LOG2_E
    for h in range(A_HEADS):
        sl = slice(h * HEAD_DIM, (h + 1) * HEAD_DIM)
        xq = p1_ref[:, sl]
        yq = xq * lax.rsqrt(jnp.mean(xq * xq, axis=-1, keepdims=True) + EPS) * qn
        q_ref[:, sl] = (rope_a(yq) * scale).astype(q_ref.dtype)
        xk = p1_ref[:, A_HEADS * HEAD_DIM + h * HEAD_DIM:A_HEADS * HEAD_DIM + (h + 1) * HEAD_DIM]
        yk = xk * lax.rsqrt(jnp.mean(xk * xk, axis=-1, keepdims=True) + EPS) * kn
        k_ref[:, sl] = rope_a(yk).astype(k_ref.dtype)
    for j in range(IDX_HEADS * IDX_DIM // LANES):
        sl = slice(j * LANES, (j + 1) * LANES)
        qi_ref[:, sl] = rope_i(p3_ref[:, sl]).astype(qi_ref.dtype)
    xk = ps_ref[:, 0:LANES]
    ms = jnp.sum(xk * xk, axis=-1, keepdims=True) * (1.0 / IDX_DIM)
    yk = xk * lax.rsqrt(ms + EPS) * ikn_ref[...]
    ki_ref[...] = rope_i(yk).T[:IDX_DIM, :].astype(ki_ref.dtype)
    wi_ref[...] = ps_ref[:, LANES:LANES + IDX_HEADS] * (IDX_HEADS ** -0.5 * IDX_DIM ** -0.5)


def _dsa_prep(p1, p3, ps, positions, ifa, ifi, qn, kn, ikn):
    t = p1.shape[0]
    tm = min(t, 256)
    aw = A_HEADS * HEAD_DIM
    iq = IDX_HEADS * IDX_DIM
    row = lambda i: (i, 0)
    fix = lambda i: (0, 0)
    return pl.pallas_call(
        _dsa_prep_kernel,
        grid=(t // tm,),
        in_specs=[
            pl.BlockSpec((tm, 2 * aw), row), pl.BlockSpec((tm, iq), row), pl.BlockSpec((tm, ps.shape[1]), row),
            pl.BlockSpec((tm, 1), row), pl.BlockSpec((1, LANES), fix), pl.BlockSpec((1, LANES), fix),
            pl.BlockSpec((1, LANES), fix), pl.BlockSpec((1, LANES), fix), pl.BlockSpec((1, LANES), fix),
        ],
        out_specs=[
            pl.BlockSpec((tm, aw), row), pl.BlockSpec((tm, aw), row), pl.BlockSpec((tm, iq), row),
            pl.BlockSpec((IDX_DIM, tm), lambda i: (0, i)), pl.BlockSpec((tm, IDX_HEADS), row),
        ],
        out_shape=[
            jax.ShapeDtypeStruct((t, aw), BF16), jax.ShapeDtypeStruct((t, aw), BF16),
            jax.ShapeDtypeStruct((t, iq), BF16), jax.ShapeDtypeStruct((IDX_DIM, t), BF16),
            jax.ShapeDtypeStruct((t, IDX_HEADS), F32),
        ],
        compiler_params=_cp("parallel"),
    )(p1, p3, ps, positions, ifa, ifi, qn, kn, ikn)


def _indexer_kernel(qi_ref, w_ref, ki_ref, bias_ref, s_ref, wb_ref, qs_ref, *, kc, topk, maxit):
    i = pl.program_id(0)
    qb = Q_BLOCK
    n_ch = ((i + 1) * qb + kc - 1) // kc
    nslab = kc // LANES
    sub_kc = min(kc, 2 * LANES)
    for h in range(IDX_HEADS):
        wb_ref[h] = jnp.broadcast_to(w_ref[:, h:h + 1], (qb, LANES))
        qs_ref[h * qb:(h + 1) * qb, :] = qi_ref[:, h * IDX_DIM:(h + 1) * IDX_DIM]
    row = lax.broadcasted_iota(I32, (qb, LANES), 0)
    lane = lax.broadcasted_iota(I32, (qb, LANES), 1)
    limit = jnp.where(row < CHUNK, i * qb + CHUNK, (i + 1) * qb)

    def score_body(c, carry):
        mn, mx = carry
        off = pl.multiple_of(c * kc, kc)
        accs = []
        for sub in range(kc // sub_kc):
            kblk = ki_ref[:, pl.ds(pl.multiple_of(off + sub * sub_kc, sub_kc), sub_kc)]
            part = [jnp.zeros((qb, LANES), F32) for _ in range(sub_kc // LANES)]
            s_all = _dot(qs_ref[...], kblk)
            for h in range(IDX_HEADS):
                wbh = wb_ref[h]
                for j in range(sub_kc // LANES):
                    s = s_all[h * qb:(h + 1) * qb, j * LANES:(j + 1) * LANES]
                    part[j] = part[j] + wbh * jnp.maximum(s, 0.0)
            accs += part
        for j in range(nslab):
            adm = (lane + (off + j * LANES)) < limit
            val = jnp.where(adm, accs[j], NEG)
            s_ref[:, pl.ds(pl.multiple_of(off + j * LANES, LANES), LANES)] = val
            mn = jnp.minimum(mn, jnp.where(adm, accs[j], BIG))
            mx = jnp.maximum(mx, val)
        return mn, mx

    mn, mx = lax.fori_loop(0, n_ch, score_body,
                           (jnp.full((qb, LANES), BIG, F32), jnp.full((qb, LANES), NEG, F32)))
    lo0 = jnp.min(mn, axis=-1, keepdims=True)
    hi0 = jnp.max(mx, axis=-1, keepdims=True)
    kf = float(topk)

    def count_ge(thr):
        thr_b = jnp.broadcast_to(thr, (qb, LANES))

        def body(c, acc):
            off = pl.multiple_of(c * kc, kc)
            for j in range(nslab):
                sc = s_ref[:, pl.ds(pl.multiple_of(off + j * LANES, LANES), LANES)]
                acc = acc + jnp.where(sc >= thr_b, 1.0, 0.0)
            return acc

        acc = lax.fori_loop(0, n_ch, body, jnp.zeros((qb, LANES), F32))
        return jnp.sum(acc, axis=-1, keepdims=True)

    done0 = jnp.where(limit[:, 0:1] <= topk, 1.0, 0.0)

    def cond(st):
        it, _, _, done = st
        return jnp.logical_and(it < maxit, jnp.min(done) < 0.5)

    def body(st):
        it, lo, hi, done = st
        mid = lo + (hi - lo) * 0.5
        stuck = jnp.logical_or(mid <= lo, mid >= hi)
        c = count_ge(mid)
        ge = c >= kf
        lo = jnp.where(ge, mid, lo)
        hi = jnp.where(ge, hi, mid)
        done = jnp.maximum(done, jnp.where(jnp.logical_or(c == kf, stuck), 1.0, 0.0))
        return it + 1, lo, hi, done

    _, lo, hi, _ = lax.while_loop(cond, body, (jnp.int32(0), lo0, hi0, done0))
    c_hi = count_ge(hi)
    lo = jnp.where(c_hi >= kf, hi, lo)
    lo_b = jnp.broadcast_to(lo, (qb, LANES))

    bias_ref[...] = jnp.full(bias_ref.shape, NEG, bias_ref.dtype)

    def write_body(c, _):
        off = pl.multiple_of(c * kc, kc)
        for j in range(nslab):
            o = pl.multiple_of(off + j * LANES, LANES)
            sc = s_ref[:, pl.ds(o, LANES)]
            bias_ref[:, pl.ds(o, LANES)] = jnp.where(sc >= lo_b, 0.0, NEG).astype(bias_ref.dtype)
        return 0

    lax.fori_loop(0, n_ch, write_body, 0)


def _indexer(q_idx, w_idx, k_idx, topk):
    t = q_idx.shape[0]
    kc = min(t, 512)
    return pl.pallas_call(
        functools.partial(_indexer_kernel, kc=kc, topk=topk, maxit=64),
        grid=(t // Q_BLOCK,),
        in_specs=[
            pl.BlockSpec((Q_BLOCK, IDX_HEADS * IDX_DIM), lambda i: (i, 0)),
            pl.BlockSpec((Q_BLOCK, IDX_HEADS), lambda i: (i, 0)),
            pl.BlockSpec((IDX_DIM, t), lambda i: (0, 0)),
        ],
        out_specs=pl.BlockSpec((Q_BLOCK, t), lambda i: (i, 0)),
        out_shape=jax.ShapeDtypeStruct((t, t), BF16),
        scratch_shapes=[pltpu.VMEM((Q_BLOCK, t), F32), pltpu.VMEM((IDX_HEADS, Q_BLOCK, LANES), F32),
                        pltpu.VMEM((IDX_HEADS * Q_BLOCK, IDX_DIM), BF16)],
        compiler_params=_cp("parallel"),
    )(q_idx, w_idx, k_idx)


def _last_key_block(qi, bq, bk):
    return ((qi + 1) * bq - 1) // bk


def _attn_kernel(q_ref, k_ref, v_ref, b_ref, o_ref, acc_ref, m_ref, l_ref):
    qi = pl.program_id(0)
    ki = pl.program_id(1)
    last = _last_key_block(qi, q_ref.shape[0], k_ref.shape[0])

    @pl.when(ki == 0)
    def _():
        acc_ref[...] = jnp.zeros_like(acc_ref)
        m_ref[...] = jnp.full(m_ref.shape, NEG, F32)
        l_ref[...] = jnp.zeros_like(l_ref)

    @pl.when(ki <= last)
    def _():
        bias = b_ref[...]
        ones = jnp.ones((k_ref.shape[0], HEAD_DIM), BF16)
        for h in range(A_HEADS):
            sl = slice(h * HEAD_DIM, (h + 1) * HEAD_DIM)
            s = _dot_nt(q_ref[:, sl], k_ref[:, sl]).astype(BF16) + bias
            m_prev = m_ref[h]
            m_new = jnp.maximum(m_prev, jnp.max(s, axis=-1, keepdims=True).astype(F32))
            alpha = jnp.exp2(m_prev - m_new)
            p = jnp.exp2(s - m_new[:, 0:1].astype(BF16))
            pv = _dot(p, jnp.concatenate([v_ref[:, sl], ones], axis=1))
            l_ref[h] = alpha * l_ref[h] + pv[:, HEAD_DIM:]
            acc_ref[:, sl] = alpha * acc_ref[:, sl] + pv[:, :HEAD_DIM]
            m_ref[h] = m_new

    @pl.when(ki == last)
    def _():
        for h in range(A_HEADS):
            sl = slice(h * HEAD_DIM, (h + 1) * HEAD_DIM)
            o_ref[:, sl] = (acc_ref[:, sl] / l_ref[h]).astype(o_ref.dtype)


def _attention(q, k, v, bias):
    t, aw = q.shape
    bq = min(t, 512)
    bk = min(t, 1024)
    kv_map = lambda i, j: (jnp.minimum(j, _last_key_block(i, bq, bk)), 0)
    return pl.pallas_call(
        _attn_kernel,
        grid=(t // bq, t // bk),
        in_specs=[
            pl.BlockSpec((bq, aw), lambda i, j: (i, 0)),
            pl.BlockSpec((bk, aw), kv_map),
            pl.BlockSpec((bk, aw), kv_map),
            pl.BlockSpec((bq, bk), lambda i, j: (i, jnp.minimum(j, _last_key_block(i, bq, bk)))),
        ],
        out_specs=pl.BlockSpec((bq, aw), lambda i, j: (i, 0)),
        out_shape=jax.ShapeDtypeStruct((t, aw), BF16),
        scratch_shapes=[pltpu.VMEM((bq, aw), F32), pltpu.VMEM((A_HEADS, bq, LANES), F32),
                        pltpu.VMEM((A_HEADS, bq, LANES), F32)],
        compiler_params=_cp("parallel", "arbitrary"),
    )(q, k, v, bias)


def _gdn_prep_kernel(cur_ref, prev_ref, w_ref, o_ref, *, qk_blocks):
    i = pl.program_id(0)
    j = pl.program_id(1)
    tm = cur_ref.shape[0]
    cur = cur_ref[...]
    prev = prev_ref[...] * jnp.where(i > 0, 1.0, 0.0)
    xcat = jnp.concatenate([prev, cur], axis=0)
    y = cur * w_ref[CONV_WIDTH - 1:CONV_WIDTH, :]
    for tap in range(CONV_WIDTH - 1):
        y = y + xcat[8 - (CONV_WIDTH - 1) + tap:8 - (CONV_WIDTH - 1) + tap + tm, :] * w_ref[tap:tap + 1, :]
    y = y * _sigmoid(y)
    is_qk = j < 2 * qk_blocks
    qscale = jnp.where(j < qk_blocks, HEAD_DIM ** -0.5, 1.0)
    for h in range(cur.shape[1] // HEAD_DIM):
        yh = y[:, h * HEAD_DIM:(h + 1) * HEAD_DIM]
        nrm = lax.rsqrt(jnp.sum(yh * yh, axis=-1, keepdims=True) + EPS) * qscale
        o_ref[h] = jnp.where(is_qk, yh * nrm, yh).astype(o_ref.dtype)


def _gdn_prep(p4, conv_w):
    t, ch = p4.shape
    tm = min(t, 256)
    cb = 1024
    hb = cb // HEAD_DIM
    qk_blocks = B_QK_HEADS * HEAD_DIM // cb
    return pl.pallas_call(
        functools.partial(_gdn_prep_kernel, qk_blocks=qk_blocks),
        grid=(t // tm, ch // cb),
        in_specs=[
            pl.BlockSpec((tm, cb), lambda i, j: (i, j)),
            pl.BlockSpec((8, cb), lambda i, j: (jnp.maximum(i * (tm // 8) - 1, 0), j)),
            pl.BlockSpec((CONV_WIDTH, cb), lambda i, j: (0, j)),
        ],
        out_specs=pl.BlockSpec((hb, tm, HEAD_DIM), lambda i, j: (j, i, 0)),
        out_shape=jax.ShapeDtypeStruct((ch // HEAD_DIM, t, HEAD_DIM), BF16),
        compiler_params=_cp("parallel", "parallel"),
    )(p4, p4, conv_w)


N_PAIRS = B_V_HEADS // 2


def _tri_inverse_minus_eye(a, row, col, lane_lo):
    n = len(a)
    lane_hi = jnp.logical_not(lane_lo)

    def mm(x, y):
        bd = jnp.concatenate([jnp.where(lane_lo, y, 0.0), jnp.where(lane_hi, y, 0.0)], axis=0)
        return _dot(x.astype(BF16), bd.astype(BF16))

    blk16 = (row // 16) == (col // 16)
    blk32 = (row // 32) == (col // 32)
    d1 = [jnp.where(blk16, a[p], 0.0) for p in range(n)]
    e = [-d1[p] for p in range(n)]
    dk = d1
    for _ in range(3):
        dk = [mm(dk[p], dk[p]) for p in range(n)]
        de = [mm(dk[p], e[p]) for p in range(n)]
        e = [e[p] + dk[p] + de[p] for p in range(n)]
    for sel in (jnp.logical_and(blk32, jnp.logical_not(blk16)), jnp.logical_not(blk32)):
        lo = [jnp.where(sel, a[p], 0.0) for p in range(n)]
        m1 = [lo[p] + mm(e[p], lo[p]) for p in range(n)]
        m2 = [m1[p] + mm(m1[p], e[p]) for p in range(n)]
        e = [e[p] - m2[p] for p in range(n)]
    return e


def _gdn_a_kernel(qkv_ref, ps_ref, pst_ref, alr_ref, alc_ref, dtr_ref, dtc_ref,
                  wq_ref, u_ref, ak_ref, egl_ref):
    c = CHUNK
    pairs = range(N_PAIRS)
    row = lax.broadcasted_iota(I32, (c, LANES), 0)
    lane = lax.broadcasted_iota(I32, (c, LANES), 1)
    col = lane & (c - 1)
    lane_lo = lane < c
    lane_hi = jnp.logical_not(lane_lo)
    incl = row >= col
    strict = row > col
    eye2 = jnp.where(row == col, 1.0, 0.0)
    r64 = lax.broadcasted_iota(I32, (c, c), 0)
    c64 = lax.broadcasted_iota(I32, (c, c), 1)
    lower = jnp.where(r64 >= c64, 1.0, 0.0)
    r128 = lax.broadcasted_iota(I32, (LANES, LANES), 0)
    c128 = lax.broadcasted_iota(I32, (LANES, LANES), 1)
    upper2 = jnp.where(jnp.logical_and(r128 // c == c128 // c, r128 <= c128), 1.0, 0.0)

    beta_col = _sigmoid(ps_ref[:, 2 * LANES:2 * LANES + B_V_HEADS])
    g_col = -jnp.exp(alr_ref[...]) * _softplus(ps_ref[:, 3 * LANES:3 * LANES + B_V_HEADS] + dtr_ref[...])
    g_row2 = -jnp.exp(alc_ref[...]) * _softplus(pst_ref[0] + dtc_ref[...])
    gc_col = _dot_hi(lower, g_col)
    gc_row2 = _dot_hi(g_row2, upper2)
    eg_col = jnp.exp(gc_col)
    egl_ref[0] = eg_col[c - 1:c, :]

    def col_pair(m, p):
        return jnp.where(lane_lo, m[:, 2 * p:2 * p + 1], m[:, 2 * p + 1:2 * p + 2])

    qbf = [qkv_ref[p] for p in pairs]
    kbf = [qkv_ref[B_QK_HEADS + p] for p in pairs]
    q = [qbf[p].astype(F32) for p in pairs]
    k = [kbf[p].astype(F32) for p in pairs]
    g = [_dot_nt(jnp.concatenate([qbf[p], kbf[p]], axis=0),
                 jnp.concatenate([kbf[p], kbf[p]], axis=0)) for p in pairs]
    decay = [jnp.exp(jnp.where(incl, col_pair(gc_col, p) - gc_row2[p:p + 1, :], NEG)) for p in pairs]
    a_mat = [jnp.where(strict, col_pair(beta_col, p) * g[p][c:] * decay[p], 0.0) for p in pairs]
    attn = [g[p][:c] * decay[p] for p in pairs]
    e = _tri_inverse_minus_eye(a_mat, row, col, lane_lo)

    def rhs_of(p, h):
        bc = beta_col[:, h:h + 1]
        v = qkv_ref[2 * B_QK_HEADS + h].astype(F32)
        return jnp.concatenate([v * bc, k[p] * (bc * eg_col[:, h:h + 1])], axis=1)

    rhs = [jnp.concatenate([rhs_of(p, 2 * p), rhs_of(p, 2 * p + 1)], axis=0).astype(BF16) for p in pairs]
    t_mat = [eye2 + e[p] for p in pairs]
    uw = [[_dot(jnp.where(sel, t_mat[p], 0.0).astype(BF16), rhs[p]) for sel in (lane_lo, lane_hi)] for p in pairs]
    for p in pairs:
        kd = []
        for r in range(2):
            h = 2 * p + r
            u_ref[h] = uw[p][r][:, :HEAD_DIM]
            qe = q[p] * eg_col[:, h:h + 1]
            wq_ref[h] = jnp.concatenate([uw[p][r][:, HEAD_DIM:], qe], axis=0).astype(wq_ref.dtype)
            kd.append(k[p] * jnp.exp(gc_col[c - 1:c, h:h + 1] - gc_col[:, h:h + 1]))
        kdt = jnp.concatenate(kd, axis=0).T
        ak_ref[p] = jnp.concatenate([attn[p], kdt], axis=0).astype(ak_ref.dtype)


def _gdn_phase_a(qkv, ps, pst, a_log, dt_bias):
    nh, t, _ = qkv.shape
    n = t // CHUNK
    hv = B_V_HEADS
    fix2 = lambda i: (0, 0)
    return pl.pallas_call(
        _gdn_a_kernel,
        grid=(n,),
        in_specs=[
            pl.BlockSpec((nh, CHUNK, HEAD_DIM), lambda i: (0, i, 0)),
            pl.BlockSpec((CHUNK, ps.shape[1]), lambda i: (i, 0)),
            pl.BlockSpec((1, N_PAIRS, LANES), lambda i: (i, 0, 0)),
            pl.BlockSpec((1, hv), fix2), pl.BlockSpec((N_PAIRS, LANES), fix2),
            pl.BlockSpec((1, hv), fix2), pl.BlockSpec((N_PAIRS, LANES), fix2),
        ],
        out_specs=[
            pl.BlockSpec((hv, 2 * CHUNK, HEAD_DIM), lambda i: (0, i, 0)),
            pl.BlockSpec((hv, CHUNK, HEAD_DIM), lambda i: (0, i, 0)),
            pl.BlockSpec((N_PAIRS, CHUNK + HEAD_DIM, LANES), lambda i: (0, i, 0)),
            pl.BlockSpec((1, 1, hv), lambda i: (i, 0, 0)),
        ],
        out_shape=[
            jax.ShapeDtypeStruct((hv, 2 * t, HEAD_DIM), BF16),
            jax.ShapeDtypeStruct((hv, t, HEAD_DIM), F32),
            jax.ShapeDtypeStruct((N_PAIRS, n * (CHUNK + HEAD_DIM), LANES), BF16),
            jax.ShapeDtypeStruct((n, 1, hv), F32),
        ],
        compiler_params=_cp("parallel"),
    )(qkv, ps, pst, a_log.reshape(1, hv), _pair_lanes(a_log), dt_bias.reshape(1, hv), _pair_lanes(dt_bias))


def _pair_lanes(per_head):
    return jnp.repeat(per_head.reshape(N_PAIRS, 2), CHUNK, axis=1)


def _gdn_b_kernel(wq_ref, u_ref, ak_ref, egl_ref, z_ref, gn_ref, o_ref, s_ref):
    c = CHUNK

    @pl.when(pl.program_id(0) == 0)
    def _():
        s_ref[...] = jnp.zeros_like(s_ref)

    gn = gn_ref[...]
    heads = range(B_V_HEADS)
    lane_lo = lax.broadcasted_iota(I32, (c + HEAD_DIM, LANES), 1) < c
    lane_hi = jnp.logical_not(lane_lo)
    zero = jnp.zeros((), ak_ref.dtype)
    s = [s_ref[h] for h in heads]
    r1 = [_dot(wq_ref[h], s[h].astype(BF16)) for h in heads]
    vn = [u_ref[h] - r1[h][:c] for h in heads]
    r2 = []
    for p in range(N_PAIRS):
        vst = jnp.concatenate([vn[2 * p], vn[2 * p + 1]], axis=0).astype(BF16)
        akp = ak_ref[p]
        r2.append(_dot(jnp.where(lane_lo, akp, zero), vst))
        r2.append(_dot(jnp.where(lane_hi, akp, zero), vst))
    for h in heads:
        sl = slice(h * HEAD_DIM, (h + 1) * HEAD_DIM)
        o = r1[h][c:] + r2[h][:c]
        s_ref[h] = s[h] * egl_ref[0, :, h:h + 1] + r2[h][c:]
        on = o * lax.rsqrt(jnp.mean(o * o, axis=-1, keepdims=True) + EPS) * gn
        z = z_ref[:, sl].astype(F32)
        o_ref[:, sl] = (on * (z * _sigmoid(z))).astype(o_ref.dtype)


def _gdn_phase_b(wq, u, ak, egl, z, out_norm):
    hv, t, _ = u.shape
    n = t // CHUNK
    return pl.pallas_call(
        _gdn_b_kernel,
        grid=(n,),
        in_specs=[
            pl.BlockSpec((hv, 2 * CHUNK, HEAD_DIM), lambda i: (0, i, 0)),
            pl.BlockSpec((hv, CHUNK, HEAD_DIM), lambda i: (0, i, 0)),
            pl.BlockSpec((N_PAIRS, CHUNK + HEAD_DIM, LANES), lambda i: (0, i, 0)),
            pl.BlockSpec((1, 1, hv), lambda i: (i, 0, 0)),
            pl.BlockSpec((CHUNK, hv * HEAD_DIM), lambda i: (i, 0)),
            pl.BlockSpec((1, HEAD_DIM), lambda i: (0, 0)),
        ],
        out_specs=pl.BlockSpec((CHUNK, hv * HEAD_DIM), lambda i: (i, 0)),
        out_shape=jax.ShapeDtypeStruct((t, hv * HEAD_DIM), BF16),
        scratch_shapes=[pltpu.VMEM((hv, HEAD_DIM, HEAD_DIM), F32)],
        compiler_params=_cp("arbitrary"),
    )(wq, u, ak, egl, z, out_norm.reshape(1, HEAD_DIM))


_A_W = A_HEADS * HEAD_DIM
_A_IQ = IDX_HEADS * IDX_DIM
_B_QK = B_QK_HEADS * HEAD_DIM
_B_V = B_V_HEADS * HEAD_DIM
_OFF_AIK = 3 * _A_W + _A_IQ
_OFF_AIW = _OFF_AIK + IDX_DIM
_OFF_BQ = _OFF_AIW + IDX_HEADS
_OFF_BZ = _OFF_BQ + 2 * _B_QK + _B_V
_OFF_BB = _OFF_BZ + _B_V
_OFF_BA = _OFF_BB + B_V_HEADS
_OFF_GATE = _OFF_BA + B_V_HEADS


def _small_weights(w_in):
    d = w_in.shape[0]
    pad = lambda a: jnp.pad(a, ((0, 0), (0, LANES - a.shape[1])))
    return jnp.concatenate([
        pad(w_in[:, _OFF_AIK:_OFF_AIK + IDX_DIM]), pad(w_in[:, _OFF_AIW:_OFF_AIW + IDX_HEADS]),
        pad(w_in[:, _OFF_BB:_OFF_BB + B_V_HEADS]), pad(w_in[:, _OFF_BA:_OFF_BA + B_V_HEADS]),
    ], axis=1).astype(BF16)


def _ba_weights_t(w_in):
    w = w_in[:, _OFF_BA:_OFF_BA + B_V_HEADS]
    return jnp.concatenate([w[:, 0::2], w[:, 1::2]], axis=1).T.astype(BF16)


def _rope_freqs(rot, width):
    half = rot // 2
    inv = jnp.power(ROPE_THETA, -jnp.arange(half, dtype=F32) * 2.0 / rot)
    pat = jnp.concatenate([inv, inv, jnp.zeros((width - rot,), F32)])
    return jnp.tile(pat, LANES // width).reshape(1, LANES)


def _dsa_branch(h, w_in, ps, positions, q_norm, k_norm, idx_k_norm):
    t = h.shape[0]
    tm = 1024
    (p1,) = _matmul(h, w_in[:, 0:2 * _A_W].astype(BF16), [F32], tm, 1024)
    (v,) = _matmul(h, w_in[:, 2 * _A_W:3 * _A_W].astype(BF16), [BF16], tm, 1024)
    (p3,) = _matmul(h, w_in[:, 3 * _A_W:3 * _A_W + _A_IQ].astype(BF16), [F32], tm, 1024)
    ikn = jnp.pad(idx_k_norm.reshape(1, IDX_DIM), ((0, 0), (0, LANES - IDX_DIM)))
    q, k, q_idx, k_idx, w_idx = _dsa_prep(
        p1, p3, ps, positions.reshape(t, 1), _rope_freqs(HEAD_DIM // 4, HEAD_DIM), _rope_freqs(IDX_DIM // 4, IDX_DIM),
        q_norm.reshape(1, HEAD_DIM), k_norm.reshape(1, HEAD_DIM), ikn)
    bias = _indexer(q_idx, w_idx, k_idx, min(TOPK_MAX, t // 4))
    return _attention(q, k, v, bias)


def _gdn_branch(h, w_in, ps, pst, conv_w, a_log, dt_bias, out_norm):
    tm = 1024
    (p4,) = _matmul(h, w_in[:, _OFF_BQ:_OFF_BZ].astype(BF16), [F32], tm, 1024)
    (z,) = _matmul(h, w_in[:, _OFF_BZ:_OFF_BB].astype(BF16), [BF16], tm, 1024)
    qkv = _gdn_prep(p4, conv_w)
    wq, u, ak, egl = _gdn_phase_a(qkv, ps, pst, a_log, dt_bias)
    return _gdn_phase_b(wq, u, ak, egl, z, out_norm)


def _gate_a_epilogue(acc, g):
    return (_sigmoid(g.astype(F32)) * acc,)


def _gate_b_epilogue(acc, g, m):
    return (m + _sigmoid(g.astype(F32)) * acc,)


def _residual_norm_epilogue(acc, x, gain):
    x1 = x + acc
    h2 = x1 * lax.rsqrt(jnp.mean(x1 * x1, axis=-1, keepdims=True) + EPS) * gain
    return x1, h2


def _mix(x, h, w_in, out_a, out_b, w_o_a, w_o_b, w_out, ffn_norm):
    d = x.shape[1]
    (ga,) = _matmul(h, w_in[:, _OFF_GATE:_OFF_GATE + d].astype(BF16), [BF16], 1024, 1024)
    (gb,) = _matmul(h, w_in[:, _OFF_GATE + d:_OFF_GATE + 2 * d].astype(BF16), [BF16], 1024, 1024)
    (mixa,) = _matmul(out_a, w_o_a.astype(BF16), [F32], 1024, 1024, _gate_a_epilogue, [(ga, "tile")])
    (mix,) = _matmul(out_b, w_o_b.astype(BF16), [BF16], 1024, 512, _gate_b_epilogue, [(gb, "tile"), (mixa, "tile")])
    return _matmul(mix, w_out.astype(BF16), [F32, F32], 512, d, _residual_norm_epilogue,
                   [(x, "tile"), (ffn_norm.reshape(1, d), "row")])


MOE_TM = 256
MOE_NF = 1
_R_E1, _R_E2, _R_RANK1, _R_RANK2, _R_W1, _R_W2 = 0, 1, 2, 3, 4, 5


def _router_kernel(h_ref, w_ref, b_ref, r_ref, cnt_ref, carry_ref):
    i = pl.program_id(0)

    @pl.when(i == 0)
    def _():
        carry_ref[...] = jnp.zeros_like(carry_ref)

    tb = h_ref.shape[0]
    logits = _dot(h_ref[...].astype(BF16), w_ref[...]) + b_ref[...]
    le = logits[:, :N_EXPERTS]
    lg = logits[:, N_EXPERTS:]
    lane = lax.broadcasted_iota(I32, (tb, LANES), 1)
    lanef = lane.astype(F32)
    far = float(4 * LANES)

    lgm = jnp.where(lane < N_GROUPS, lg, NEG)
    gmax = jnp.max(lgm, axis=-1, keepdims=True)
    g_idx = jnp.min(jnp.where(lgm == gmax, lanef, far), axis=-1, keepdims=True)
    g_top = 1.0 / jnp.sum(jnp.exp(lgm - gmax), axis=-1, keepdims=True)

    in_grp = (lane // EXPERTS_PER_GROUP).astype(F32) == g_idx
    lem = jnp.where(in_grp, le, NEG)
    e1 = jnp.max(lem, axis=-1, keepdims=True)
    i1 = jnp.min(jnp.where(lem == e1, lanef, far), axis=-1, keepdims=True)
    lem2 = jnp.where(lanef == i1, NEG, lem)
    e2 = jnp.max(lem2, axis=-1, keepdims=True)
    i2 = jnp.min(jnp.where(lem2 == e2, lanef, far), axis=-1, keepdims=True)
    se = jnp.sum(jnp.exp(lem - e1), axis=-1, keepdims=True)
    p1 = 1.0 / se
    p2 = jnp.exp(e2 - e1) / se
    w1 = g_top * p1 / (p1 + p2)
    w2 = g_top * p2 / (p1 + p2)

    o1 = jnp.where(lanef == i1, 1.0, 0.0)
    o2 = jnp.where(lanef == i2, 1.0, 0.0)
    osum = o1 + o2
    rr = lax.broadcasted_iota(I32, (tb, tb), 0)
    cc = lax.broadcasted_iota(I32, (tb, tb), 1)
    before = jnp.where(cc < rr, 1.0, 0.0).astype(BF16)
    prefix = _dot(before, osum.astype(BF16)) + carry_ref[0:1, :]
    rank1 = jnp.sum(prefix * o1, axis=-1, keepdims=True)
    rank2 = jnp.sum(prefix * o2, axis=-1, keepdims=True)
    carry_ref[...] = carry_ref[...] + jnp.sum(osum, axis=0, keepdims=True)
    cnt_ref[...] = carry_ref[...]

    rec = jnp.zeros((tb, LANES), F32)
    for idx, val in ((_R_E1, i1), (_R_E2, i2), (_R_RANK1, rank1), (_R_RANK2, rank2), (_R_W1, w1), (_R_W2, w2)):
        rec = jnp.where(lane == idx, val, rec)
    r_ref[...] = rec


def _router(h2, w_router, b_router):
    t, d = h2.shape
    tb = min(t, 512)
    return pl.pallas_call(
        _router_kernel,
        grid=(t // tb,),
        in_specs=[pl.BlockSpec((tb, d), lambda i: (i, 0)), pl.BlockSpec((d, 2 * LANES), lambda i: (0, 0)),
                  pl.BlockSpec((1, 2 * LANES), lambda i: (0, 0))],
        out_specs=[pl.BlockSpec((tb, LANES), lambda i: (i, 0)), pl.BlockSpec((8, LANES), lambda i: (0, 0))],
        out_shape=[jax.ShapeDtypeStruct((t, LANES), F32), jax.ShapeDtypeStruct((8, LANES), F32)],
        scratch_shapes=[pltpu.VMEM((8, LANES), F32)],
        compiler_params=_cp("arbitrary"),
    )(h2, w_router, b_router)


def _positions_kernel(r_ref, cnt_ref, pos_ref, te_ref, nv_ref):
    t = r_ref.shape[0]
    nt = te_ref.shape[0]
    cnt = cnt_ref[...]
    tiles = jnp.floor((cnt + (MOE_TM - 1)) * (1.0 / MOE_TM))
    rr = lax.broadcasted_iota(I32, (LANES, LANES), 0)
    cc = lax.broadcasted_iota(I32, (LANES, LANES), 1)
    start_tiles = _dot(tiles.astype(BF16), jnp.where(rr < cc, 1.0, 0.0).astype(BF16))
    start = start_tiles[0:1, :] * float(MOE_TM)
    rec = r_ref[...]
    lane = lax.broadcasted_iota(I32, (t, LANES), 1)
    lanef = lane.astype(F32)
    pos1 = jnp.sum(jnp.where(lanef == rec[:, _R_E1:_R_E1 + 1], start, 0.0), axis=-1, keepdims=True) \
        + rec[:, _R_RANK1:_R_RANK1 + 1]
    pos2 = jnp.sum(jnp.where(lanef == rec[:, _R_E2:_R_E2 + 1], start, 0.0), axis=-1, keepdims=True) \
        + rec[:, _R_RANK2:_R_RANK2 + 1]
    pos_ref[...] = jnp.where(lane == 0, pos1, jnp.where(lane == 1, pos2, 0.0)).astype(I32)
    tile_id = lax.broadcasted_iota(I32, (nt, LANES), 0).astype(F32)
    tlane = lax.broadcasted_iota(I32, (nt, LANES), 1)
    owner = jnp.sum(jnp.where(start_tiles[0:1, :] <= tile_id, 1.0, 0.0), axis=-1, keepdims=True) - 1.0
    is_owner = tlane.astype(F32) == owner
    own_cnt = jnp.sum(jnp.where(is_owner, cnt[0:1, :], 0.0), axis=-1, keepdims=True)
    own_start = jnp.sum(jnp.where(is_owner, start_tiles[0:1, :], 0.0), axis=-1, keepdims=True)
    rows = jnp.clip(own_cnt - (tile_id[:, 0:1] - own_start) * float(MOE_TM), 0.0, float(MOE_TM))
    te_ref[...] = jnp.where(tlane == 0, owner, jnp.where(tlane == 1, rows, 0.0)).astype(I32)
    nv_ref[...] = jnp.broadcast_to(jnp.sum(tiles[0:1, :], axis=-1, keepdims=True), (8, LANES)).astype(I32)


def _positions(rec, cnt, n_tiles):
    t = rec.shape[0]
    full = lambda shape: pl.BlockSpec(shape, lambda i: (0, 0))
    return pl.pallas_call(
        _positions_kernel,
        grid=(1,),
        in_specs=[full((t, LANES)), full((8, LANES))],
        out_specs=[full((t, LANES)), full((n_tiles, LANES)), full((8, LANES))],
        out_shape=[jax.ShapeDtypeStruct((t, LANES), I32), jax.ShapeDtypeStruct((n_tiles, LANES), I32),
                   jax.ShapeDtypeStruct((8, LANES), I32)],
        compiler_params=_cp("arbitrary"),
    )(rec, cnt)


def _invert_kernel(pos_ref, asg_ref):
    def scatter(a, _):
        asg_ref[pos_ref[a]] = a
        return 0

    lax.fori_loop(0, pos_ref.shape[0], scatter, 0, unroll=8)


def _invert(pos, n_slots):
    return pl.pallas_call(
        _invert_kernel,
        in_specs=[pl.BlockSpec(memory_space=pltpu.SMEM)],
        out_specs=pl.BlockSpec(memory_space=pltpu.SMEM),
        out_shape=jax.ShapeDtypeStruct((n_slots,), I32),
    )(pos)


ROW_UNROLL = 4


def _for_each_row(n, body):
    full = lax.shift_right_logical(n, ROW_UNROLL.bit_length() - 1)

    def group(g, _):
        for u in range(ROW_UNROLL):
            body(g * ROW_UNROLL + u)
        return 0

    lax.fori_loop(0, full, group, 0)

    def single(r, _):
        body(r)
        return 0

    lax.fori_loop(full * ROW_UNROLL, n, single, 0)


def _ffn_kernel(te_ref, rows_ref, nv_ref, asg_ref, h_ref, wg_ref, wu_ref, wd_ref, y_ref,
                xbuf_ref, x16_ref, obuf_ref, gsem, ssem):
    i = pl.program_id(0)
    f = pl.program_id(1)
    tm, d = x16_ref.shape
    nv = nv_ref[0]
    last_f = pl.num_programs(1) - 1

    def gather_copy(tile, slot, r):
        token = lax.shift_right_logical(asg_ref[tile * tm + r], 1)
        return pltpu.make_async_copy(h_ref.at[pl.ds(token, 1)], xbuf_ref.at[slot, pl.ds(r, 1)], gsem.at[slot])

    def scatter_copy(tile, r):
        a = asg_ref[tile * tm + r]
        col = pl.multiple_of(jnp.bitwise_and(a, 1) * d, d)
        return pltpu.make_async_copy(obuf_ref.at[pl.ds(r, 1)],
                                     y_ref.at[pl.ds(lax.shift_right_logical(a, 1), 1), pl.ds(col, d)], ssem)

    def gather(tile, slot):
        _for_each_row(rows_ref[tile], lambda r: gather_copy(tile, slot, r).start())

    def scatter_wait(tile):
        _for_each_row(rows_ref[tile], lambda r: scatter_copy(tile, r).wait())

    @pl.when(jnp.logical_and(i == 0, f == 0))
    def _():
        xbuf_ref[...] = jnp.zeros_like(xbuf_ref)
        gather(0, 0)

    @pl.when(jnp.logical_and(f == 0, i + 1 < nv))
    def _():
        gather(i + 1, jnp.bitwise_and(i + 1, 1))

    @pl.when(jnp.logical_and(f == 0, i < nv))
    def _():
        slot = jnp.bitwise_and(i, 1)
        _for_each_row(rows_ref[i], lambda r: gather_copy(i, slot, r).wait())
        x16_ref[...] = xbuf_ref[slot].astype(x16_ref.dtype)

    @pl.when(i < nv)
    def _():
        x = x16_ref[...]
        g = _dot(x, wg_ref[0].astype(BF16))
        u = _dot(x, wu_ref[0].astype(BF16))
        act = (g * _sigmoid(g)) * u
        y = _dot(act.astype(BF16), wd_ref[0].astype(BF16))

        @pl.when(f == 0)
        def _():
            @pl.when(i > 0)
            def _():
                scatter_wait(i - 1)

            obuf_ref[...] = y

        @pl.when(f > 0)
        def _():
            obuf_ref[...] = obuf_ref[...] + y

        @pl.when(f == last_f)
        def _():
            _for_each_row(rows_ref[i], lambda r: scatter_copy(i, r).start())

            @pl.when(i == nv - 1)
            def _():
                scatter_wait(i)


def _ffn(te, rows, nv, asg, h2, w_gate, w_up, w_down, n_tiles):
    t, d = h2.shape
    fb = D_FF // MOE_NF

    def tile(i, nv):
        return jnp.minimum(i, nv[0] - 1)

    def fblk(i, f, nv):
        return jnp.where(i < nv[0], f, MOE_NF - 1)

    any_spec = pl.BlockSpec(memory_space=pl.ANY)
    return pl.pallas_call(
        _ffn_kernel,
        grid_spec=pltpu.PrefetchScalarGridSpec(
            num_scalar_prefetch=4,
            grid=(n_tiles, MOE_NF),
            in_specs=[
                any_spec,
                pl.BlockSpec((1, d, fb), lambda i, f, te, rows, nv, asg: (te[tile(i, nv)], 0, fblk(i, f, nv))),
                pl.BlockSpec((1, d, fb), lambda i, f, te, rows, nv, asg: (te[tile(i, nv)], 0, fblk(i, f, nv))),
                pl.BlockSpec((1, fb, d), lambda i, f, te, rows, nv, asg: (te[tile(i, nv)], fblk(i, f, nv), 0)),
            ],
            out_specs=any_spec,
            scratch_shapes=[pltpu.VMEM((2, MOE_TM, d), F32), pltpu.VMEM((MOE_TM, d), BF16),
                            pltpu.VMEM((MOE_TM, d), F32), pltpu.SemaphoreType.DMA((2,)),
                            pltpu.SemaphoreType.DMA(())],
        ),
        out_shape=jax.ShapeDtypeStruct((t, 2 * d), F32),
        compiler_params=_cp("arbitrary", "arbitrary"),
    )(te, rows, nv, asg, h2, w_gate, w_up, w_down)


def _combine_kernel(y_ref, x_ref, r_ref, o_ref):
    d = x_ref.shape[1]
    rec = r_ref[...]
    o_ref[...] = x_ref[...] + rec[:, _R_W1:_R_W1 + 1] * y_ref[:, :d] + rec[:, _R_W2:_R_W2 + 1] * y_ref[:, d:]


def _combine(y, x1, rec):
    t, d = x1.shape
    tb = min(t, 256)
    return pl.pallas_call(
        _combine_kernel,
        grid=(t // tb,),
        in_specs=[pl.BlockSpec((tb, 2 * d), lambda i: (i, 0)), pl.BlockSpec((tb, d), lambda i: (i, 0)),
                  pl.BlockSpec((tb, LANES), lambda i: (i, 0))],
        out_specs=pl.BlockSpec((tb, d), lambda i: (i, 0)),
        out_shape=jax.ShapeDtypeStruct((t, d), F32),
        compiler_params=_cp("parallel"),
    )(y, x1, rec)


def _moe(x1, h2, w_rg, b_rg, w_re, b_re, w_gate, w_up, w_down):
    t, d = x1.shape
    pad_w = jnp.zeros((d, LANES - N_GROUPS), F32)
    w_router = jnp.concatenate([w_re, w_rg, pad_w], axis=1).astype(BF16)
    b_router = jnp.concatenate([b_re, b_rg, jnp.zeros((LANES - N_GROUPS,), F32)]).reshape(1, 2 * LANES)
    n_tiles = 2 * t // MOE_TM + N_EXPERTS
    rec, cnt = _router(h2, w_router, b_router)
    pos2d, te2d, nv2d = _positions(rec, cnt, n_tiles)
    pos = pos2d[:, :2].reshape(2 * t)
    te = te2d[:, 0]
    rows = te2d[:, 1]
    nv = nv2d[0, :1]
    asg = _invert(pos, n_tiles * MOE_TM)
    y = _ffn(te, rows, nv, asg, h2, w_gate, w_up, w_down, n_tiles)
    return _combine(y, x1, rec)


def kernel(x, positions, attn_norm, w_in, q_norm, k_norm, idx_k_norm, conv_w, a_log, dt_bias, gdn_out_norm,
           w_o_a, w_o_b, w_out, ffn_norm, w_router_group, b_router_group, w_router_expert, b_router_expert,
           w_gate, w_up, w_down):
    b, t, d = x.shape
    x2 = x.reshape(t, d)
    layer = 0
    h = _rmsnorm(x2, attn_norm[layer])
    w_small = _small_weights(w_in[layer])
    (ps,) = _matmul(h, w_small, [F32], 1024, w_small.shape[1])
    pst = _matmul_nt(_ba_weights_t(w_in[layer]), h, 1024)
    out_a = _dsa_branch(h, w_in[layer], ps, positions, q_norm[layer], k_norm[layer], idx_k_norm[layer])
    out_b = _gdn_branch(h, w_in[layer], ps, pst, conv_w[layer], a_log[layer], dt_bias[layer], gdn_out_norm[layer])
    x1, h2 = _mix(x2, h, w_in[layer], out_a, out_b, w_o_a[layer], w_o_b[layer], w_out[layer], ffn_norm[layer])
    out = _moe(x1, h2, w_router_group[layer], b_router_group[layer], w_router_expert[layer], b_router_expert[layer],
               w_gate[layer], w_up[layer], w_down[layer])
    return out.reshape(b, t, d)
```

```python
import functools

import jax
import jax.numpy as jnp
from jax import lax
from jax.experimental import pallas as pl
from jax.experimental.pallas import tpu as pltpu

F32 = jnp.float32
BF16 = jnp.bfloat16
I32 = jnp.int32

EPS = 1e-6
NEG = -1e30
BIG = 1e30
LOG2_E = 1.4426950408889634
ROPE_THETA = 500000.0
CHUNK = 64
A_HEADS = 16
HEAD_DIM = 128
IDX_HEADS = 16
IDX_DIM = 64
TOPK_MAX = 256
Q_BLOCK = 128
B_QK_HEADS = 16
B_V_HEADS = 32
CONV_WIDTH = 4
N_GROUPS = 8
EXPERTS_PER_GROUP = 16
N_EXPERTS = 128
D_FF = 768
LANES = 128
VMEM_LIMIT = 56 * 1024 * 1024

NT_DIMS = (((1,), (1,)), ((), ()))


def _cp(*sem):
    return pltpu.CompilerParams(dimension_semantics=sem, vmem_limit_bytes=VMEM_LIMIT)


def _dot(a, b):
    return jnp.dot(a, b, preferred_element_type=F32)


def _dot_nt(a, b):
    return lax.dot_general(a, b, NT_DIMS, preferred_element_type=F32)


def _dot_hi(a, b):
    return jnp.dot(a, b, preferred_element_type=F32, precision=lax.Precision.HIGHEST)


def _sigmoid(x):
    return 1.0 / (1.0 + jnp.exp(-x))


def _softplus(x):
    return jnp.maximum(x, 0.0) + jnp.log(1.0 + jnp.exp(-jnp.abs(x)))


def _rmsnorm_kernel(x_ref, g_ref, o_ref):
    x = x_ref[...]
    ms = jnp.mean(x * x, axis=-1, keepdims=True)
    o_ref[...] = (x * lax.rsqrt(ms + EPS) * g_ref[...]).astype(o_ref.dtype)


def _rmsnorm(x, gain, out_dtype=BF16):
    t, d = x.shape
    tm = min(t, 512)
    return pl.pallas_call(
        _rmsnorm_kernel,
        grid=(t // tm,),
        in_specs=[pl.BlockSpec((tm, d), lambda i: (i, 0)), pl.BlockSpec((1, d), lambda i: (0, 0))],
        out_specs=pl.BlockSpec((tm, d), lambda i: (i, 0)),
        out_shape=jax.ShapeDtypeStruct((t, d), out_dtype),
        compiler_params=_cp("parallel"),
    )(x, gain.reshape(1, d))


def _mm_kernel(*refs, n_extra, epilogue):
    a_ref, b_ref = refs[:2]
    extra = refs[2:2 + n_extra]
    outs = refs[2 + n_extra:]
    acc = _dot(a_ref[...], b_ref[...])
    res = epilogue(acc, *[e[...] for e in extra]) if epilogue is not None else (acc,)
    for o_ref, r in zip(outs, res):
        o_ref[...] = r.astype(o_ref.dtype)


def _matmul(a, b, out_dtypes, tm, tn, epilogue=None, extras=()):
    m, k = a.shape
    n = b.shape[1]
    tm, tn = min(tm, m), min(tn, n)
    in_specs = [pl.BlockSpec((tm, k), lambda i, j: (i, 0)), pl.BlockSpec((k, tn), lambda i, j: (0, j))]
    args = [a, b]
    for arr, kind in extras:
        if kind == "tile":
            in_specs.append(pl.BlockSpec((tm, tn), lambda i, j: (i, j)))
        else:
            in_specs.append(pl.BlockSpec((1, tn), lambda i, j: (0, j)))
        args.append(arr)
    out = pl.pallas_call(
        functools.partial(_mm_kernel, n_extra=len(extras), epilogue=epilogue),
        grid=(m // tm, n // tn),
        in_specs=in_specs,
        out_specs=[pl.BlockSpec((tm, tn), lambda i, j: (i, j)) for _ in out_dtypes],
        out_shape=[jax.ShapeDtypeStruct((m, n), dt) for dt in out_dtypes],
        compiler_params=_cp("parallel", "arbitrary"),
    )(*args)
    return out


def _mm_nt_kernel(a_ref, b_ref, o_ref):
    res = _dot_nt(a_ref[...], b_ref[...])
    half = res.shape[0] // 2
    for c in range(o_ref.shape[0]):
        cols = slice(c * CHUNK, (c + 1) * CHUNK)
        o_ref[c] = jnp.concatenate([res[:half, cols], res[half:, cols]], axis=1)


def _matmul_nt(a, b, tm):
    n, k = a.shape
    m = b.shape[0]
    tm = min(tm, m)
    return pl.pallas_call(
        _mm_nt_kernel,
        grid=(m // tm,),
        in_specs=[pl.BlockSpec((n, k), lambda i: (0, 0)), pl.BlockSpec((tm, k), lambda i: (i, 0))],
        out_specs=pl.BlockSpec((tm // CHUNK, n // 2, 2 * CHUNK), lambda i: (i, 0, 0)),
        out_shape=jax.ShapeDtypeStruct((m // CHUNK, n // 2, 2 * CHUNK), F32),
        compiler_params=_cp("parallel"),
    )(a, b)


def _dsa_prep_kernel(p1_ref, p3_ref, ps_ref, pos_ref, ifa_ref, ifi_ref, qn_ref, kn_ref, ikn_ref,
                     q_ref, k_ref, qi_ref, ki_ref, wi_ref):
    tm = p1_ref.shape[0]
    pos = pos_ref[...].astype(F32)
    lane = lax.broadcasted_iota(I32, (tm, LANES), 1)
    ang_a = pos * ifa_ref[...]
    cos_a = jnp.cos(ang_a)
    sin_a = jnp.sin(ang_a)
    sin_a = jnp.where(lane < 16, -sin_a, sin_a)
    low_a = lane < 16

    def rope_a(x):
        partner = jnp.where(low_a, pltpu.roll(x, LANES - 16, 1), pltpu.roll(x, 16, 1))
        return x * cos_a + partner * sin_a

    d64 = lane & 63
    ang_i = pos * ifi_ref[...]
    cos_i = jnp.cos(ang_i)
    sin_i = jnp.sin(ang_i)
    low_i = d64 < 8
    sin_i = jnp.where(low_i, -sin_i, sin_i)

    def rope_i(x):
        partner = jnp.where(low_i, pltpu.roll(x, LANES - 8, 1), pltpu.roll(x, 8, 1))
        return x * cos_i + partner * sin_i

    qn = qn_ref[...]
    kn = kn_ref[...]
    scale = HEAD_DIM ** -0.5 * LOG2_E
    for h in range(A_HEADS):
        sl = slice(h * HEAD_DIM, (h + 1) * HEAD_DIM)
        xq = p1_ref[:, sl]
        yq = xq * lax.rsqrt(jnp.mean(xq * xq, axis=-1, keepdims=True) + EPS) * qn
        q_ref[:, sl] = (rope_a(yq) * scale).astype(q_ref.dtype)
        xk = p1_ref[:, A_HEADS * HEAD_DIM + h * HEAD_DIM:A_HEADS * HEAD_DIM + (h + 1) * HEAD_DIM]
        yk = xk * lax.rsqrt(jnp.mean(xk * xk, axis=-1, keepdims=True) + EPS) * kn
        k_ref[:, sl] = rope_a(yk).astype(k_ref.dtype)
    for j in range(IDX_HEADS * IDX_DIM // LANES):
        sl = slice(j * LANES, (j + 1) * LANES)
        qi_ref[:, sl] = rope_i(p3_ref[:, sl]).astype(qi_ref.dtype)
    xk = ps_ref[:, 0:LANES]
    ms = jnp.sum(xk * xk, axis=-1, keepdims=True) * (1.0 / IDX_DIM)
    yk = xk * lax.rsqrt(ms + EPS) * ikn_ref[...]
    ki_ref[...] = rope_i(yk).T[:IDX_DIM, :].astype(ki_ref.dtype)
    wi_ref[...] = ps_ref[:, LANES:LANES + IDX_HEADS] * (IDX_HEADS ** -0.5 * IDX_DIM ** -0.5)


def _dsa_prep(p1, p3, ps, positions, ifa, ifi, qn, kn, ikn):
    t = p1.shape[0]
    tm = min(t, 256)
    aw = A_HEADS * HEAD_DIM
    iq = IDX_HEADS * IDX_DIM
    row = lambda i: (i, 0)
    fix = lambda i: (0, 0)
    return pl.pallas_call(
        _dsa_prep_kernel,
        grid=(t // tm,),
        in_specs=[
            pl.BlockSpec((tm, 2 * aw), row), pl.BlockSpec((tm, iq), row), pl.BlockSpec((tm, ps.shape[1]), row),
            pl.BlockSpec((tm, 1), row), pl.BlockSpec((1, LANES), fix), pl.BlockSpec((1, LANES), fix),
            pl.BlockSpec((1, LANES), fix), pl.BlockSpec((1, LANES), fix), pl.BlockSpec((1, LANES), fix),
        ],
        out_specs=[
            pl.BlockSpec((tm, aw), row), pl.BlockSpec((tm, aw), row), pl.BlockSpec((tm, iq), row),
            pl.BlockSpec((IDX_DIM, tm), lambda i: (0, i)), pl.BlockSpec((tm, IDX_HEADS), row),
        ],
        out_shape=[
            jax.ShapeDtypeStruct((t, aw), BF16), jax.ShapeDtypeStruct((t, aw), BF16),
            jax.ShapeDtypeStruct((t, iq), BF16), jax.ShapeDtypeStruct((IDX_DIM, t), BF16),
            jax.ShapeDtypeStruct((t, IDX_HEADS), F32),
        ],
        compiler_params=_cp("parallel"),
    )(p1, p3, ps, positions, ifa, ifi, qn, kn, ikn)


def _indexer_kernel(qi_ref, w_ref, ki_ref, bias_ref, s_ref, wb_ref, qs_ref, *, kc, topk, maxit):
    i = pl.program_id(0)
    qb = Q_BLOCK
    n_ch = ((i + 1) * qb + kc - 1) // kc
    nslab = kc // LANES
    sub_kc = min(kc, 2 * LANES)
    for h in range(IDX_HEADS):
        wb_ref[h] = jnp.broadcast_to(w_ref[:, h:h + 1], (qb, LANES))
        qs_ref[h * qb:(h + 1) * qb, :] = qi_ref[:, h * IDX_DIM:(h + 1) * IDX_DIM]
    row = lax.broadcasted_iota(I32, (qb, LANES), 0)
    lane = lax.broadcasted_iota(I32, (qb, LANES), 1)
    limit = jnp.where(row < CHUNK, i * qb + CHUNK, (i + 1) * qb)

    def score_body(c, carry):
        mn, mx = carry
        off = pl.multiple_of(c * kc, kc)
        accs = []
        for sub in range(kc // sub_kc):
            kblk = ki_ref[:, pl.ds(pl.multiple_of(off + sub * sub_kc, sub_kc), sub_kc)]
            part = [jnp.zeros((qb, LANES), F32) for _ in range(sub_kc // LANES)]
            s_all = _dot(qs_ref[...], kblk)
            for h in range(IDX_HEADS):
                wbh = wb_ref[h]
                for j in range(sub_kc // LANES):
                    s = s_all[h * qb:(h + 1) * qb, j * LANES:(j + 1) * LANES]
                    part[j] = part[j] + wbh * jnp.maximum(s, 0.0)
            accs += part
        for j in range(nslab):
            adm = (lane + (off + j * LANES)) < limit
            val = jnp.where(adm, accs[j], NEG)
            s_ref[:, pl.ds(pl.multiple_of(off + j * LANES, LANES), LANES)] = val
            mn = jnp.minimum(mn, jnp.where(adm, accs[j], BIG))
            mx = jnp.maximum(mx, val)
        return mn, mx

    mn, mx = lax.fori_loop(0, n_ch, score_body,
                           (jnp.full((qb, LANES), BIG, F32), jnp.full((qb, LANES), NEG, F32)))
    lo0 = jnp.min(mn, axis=-1, keepdims=True)
    hi0 = jnp.max(mx, axis=-1, keepdims=True)
    kf = float(topk)

    def count_ge(thr):
        thr_b = jnp.broadcast_to(thr, (qb, LANES))

        def body(c, acc):
            off = pl.multiple_of(c * kc, kc)
            for j in range(nslab):
                sc = s_ref[:, pl.ds(pl.multiple_of(off + j * LANES, LANES), LANES)]
                acc = acc + jnp.where(sc >= thr_b, 1.0, 0.0)
            return acc

        acc = lax.fori_loop(0, n_ch, body, jnp.zeros((qb, LANES), F32))
        return jnp.sum(acc, axis=-1, keepdims=True)

    done0 = jnp.where(limit[:, 0:1] <= topk, 1.0, 0.0)

    def cond(st):
        it, _, _, done = st
        return jnp.logical_and(it < maxit, jnp.min(done) < 0.5)

    def body(st):
        it, lo, hi, done = st
        mid = lo + (hi - lo) * 0.5
        stuck = jnp.logical_or(mid <= lo, mid >= hi)
        c = count_ge(mid)
        ge = c >= kf
        lo = jnp.where(ge, mid, lo)
        hi = jnp.where(ge, hi, mid)
        done = jnp.maximum(done, jnp.where(jnp.logical_or(c == kf, stuck), 1.0, 0.0))
        return it + 1, lo, hi, done

    _, lo, hi, _ = lax.while_loop(cond, body, (jnp.int32(0), lo0, hi0, done0))
    c_hi = count_ge(hi)
    lo = jnp.where(c_hi >= kf, hi, lo)
    lo_b = jnp.broadcast_to(lo, (qb, LANES))

    bias_ref[...] = jnp.full(bias_ref.shape, NEG, bias_ref.dtype)

    def write_body(c, _):
        off = pl.multiple_of(c * kc, kc)
        for j in range(nslab):
            o = pl.multiple_of(off + j * LANES, LANES)
            sc = s_ref[:, pl.ds(o, LANES)]
            bias_ref[:, pl.ds(o, LANES)] = jnp.where(sc >= lo_b, 0.0, NEG).astype(bias_ref.dtype)
        return 0

    lax.fori_loop(0, n_ch, write_body, 0)


def _indexer(q_idx, w_idx, k_idx, topk):
    t = q_idx.shape[0]
    kc = min(t, 512)
    return pl.pallas_call(
        functools.partial(_indexer_kernel, kc=kc, topk=topk, maxit=64),
        grid=(t // Q_BLOCK,),
        in_specs=[
            pl.BlockSpec((Q_BLOCK, IDX_HEADS * IDX_DIM), lambda i: (i, 0)),
            pl.BlockSpec((Q_BLOCK, IDX_HEADS), lambda i: (i, 0)),
            pl.BlockSpec((IDX_DIM, t), lambda i: (0, 0)),
        ],
        out_specs=pl.BlockSpec((Q_BLOCK, t), lambda i: (i, 0)),
        out_shape=jax.ShapeDtypeStruct((t, t), BF16),
        scratch_shapes=[pltpu.VMEM((Q_BLOCK, t), F32), pltpu.VMEM((IDX_HEADS, Q_BLOCK, LANES), F32),
                        pltpu.VMEM((IDX_HEADS * Q_BLOCK, IDX_DIM), BF16)],
        compiler_params=_cp("parallel"),
    )(q_idx, w_idx, k_idx)


def _last_key_block(qi, bq, bk):
    return ((qi + 1) * bq - 1) // bk


def _attn_kernel(q_ref, k_ref, v_ref, b_ref, o_ref, acc_ref, m_ref, l_ref):
    qi = pl.program_id(0)
    ki = pl.program_id(1)
    last = _last_key_block(qi, q_ref.shape[0], k_ref.shape[0])

    @pl.when(ki == 0)
    def _():
        acc_ref[...] = jnp.zeros_like(acc_ref)
        m_ref[...] = jnp.full(m_ref.shape, NEG, F32)
        l_ref[...] = jnp.zeros_like(l_ref)

    @pl.when(ki <= last)
    def _():
        bias = b_ref[...]
        ones = jnp.ones((k_ref.shape[0], HEAD_DIM), BF16)
        for h in range(A_HEADS):
            sl = slice(h * HEAD_DIM, (h + 1) * HEAD_DIM)
            s = _dot_nt(q_ref[:, sl], k_ref[:, sl]).astype(BF16) + bias
            m_prev = m_ref[h]
            m_new = jnp.maximum(m_prev, jnp.max(s, axis=-1, keepdims=True).astype(F32))
            alpha = jnp.exp2(m_prev - m_new)
            p = jnp.exp2(s - m_new[:, 0:1].astype(BF16))
            pv = _dot(p, jnp.concatenate([v_ref[:, sl], ones], axis=1))
            l_ref[h] = alpha * l_ref[h] + pv[:, HEAD_DIM:]
            acc_ref[:, sl] = alpha * acc_ref[:, sl] + pv[:, :HEAD_DIM]
            m_ref[h] = m_new

    @pl.when(ki == last)
    def _():
        for h in range(A_HEADS):
            sl = slice(h * HEAD_DIM, (h + 1) * HEAD_DIM)
            o_ref[:, sl] = (acc_ref[:, sl] / l_ref[h]).astype(o_ref.dtype)


def _attention(q, k, v, bias):
    t, aw = q.shape
    bq = min(t, 512)
    bk = min(t, 1024)
    kv_map = lambda i, j: (jnp.minimum(j, _last_key_block(i, bq, bk)), 0)
    return pl.pallas_call(
        _attn_kernel,
        grid=(t // bq, t // bk),
        in_specs=[
            pl.BlockSpec((bq, aw), lambda i, j: (i, 0)),
            pl.BlockSpec((bk, aw), kv_map),
            pl.BlockSpec((bk, aw), kv_map),
            pl.BlockSpec((bq, bk), lambda i, j: (i, jnp.minimum(j, _last_key_block(i, bq, bk)))),
        ],
        out_specs=pl.BlockSpec((bq, aw), lambda i, j: (i, 0)),
        out_shape=jax.ShapeDtypeStruct((t, aw), BF16),
        scratch_shapes=[pltpu.VMEM((bq, aw), F32), pltpu.VMEM((A_HEADS, bq, LANES), F32),
                        pltpu.VMEM((A_HEADS, bq, LANES), F32)],
        compiler_params=_cp("parallel", "arbitrary"),
    )(q, k, v, bias)


N_PAIRS = B_V_HEADS // 2


def _tri_inverse_minus_eye(a, row, col, lane_lo):
    n = len(a)
    lane_hi = jnp.logical_not(lane_lo)

    def mm(x, y):
        bd = jnp.concatenate([jnp.where(lane_lo, y, 0.0), jnp.where(lane_hi, y, 0.0)], axis=0)
        return _dot(x.astype(BF16), bd.astype(BF16))

    blk16 = (row // 16) == (col // 16)
    blk32 = (row // 32) == (col // 32)
    d1 = [jnp.where(blk16, a[p], 0.0) for p in range(n)]
    e = [-d1[p] for p in range(n)]
    dk = d1
    for _ in range(3):
        dk = [mm(dk[p], dk[p]) for p in range(n)]
        de = [mm(dk[p], e[p]) for p in range(n)]
        e = [e[p] + dk[p] + de[p] for p in range(n)]
    for sel in (jnp.logical_and(blk32, jnp.logical_not(blk16)), jnp.logical_not(blk32)):
        lo = [jnp.where(sel, a[p], 0.0) for p in range(n)]
        m1 = [lo[p] + mm(e[p], lo[p]) for p in range(n)]
        m2 = [m1[p] + mm(m1[p], e[p]) for p in range(n)]
        e = [e[p] - m2[p] for p in range(n)]
    return e


def _gdn_kernel(cur_ref, prev_ref, cw_ref, ps_ref, pst_ref, alr_ref, alc_ref, dtr_ref, dtc_ref, z_ref, gn_ref,
                o_ref, s_ref):
    c = CHUNK
    pairs = range(N_PAIRS)
    heads = range(B_V_HEADS)
    step = pl.program_id(0)

    @pl.when(step == 0)
    def _():
        s_ref[...] = jnp.zeros_like(s_ref)

    keep_prev = jnp.where(step > 0, 1.0, 0.0)
    halo = CONV_WIDTH - 1

    def conv_silu(col0):
        sl = slice(col0, col0 + HEAD_DIM)
        cur = cur_ref[:, sl]
        xcat = jnp.concatenate([prev_ref[:, sl] * keep_prev, cur], axis=0)
        y = cur * cw_ref[halo:halo + 1, sl]
        for tap in range(halo):
            y = y + xcat[8 - halo + tap:8 - halo + tap + c, :] * cw_ref[tap:tap + 1, sl]
        return y * _sigmoid(y)

    def l2n(y, scale):
        return y * (lax.rsqrt(jnp.sum(y * y, axis=-1, keepdims=True) + EPS) * scale)

    q = [l2n(conv_silu(p * HEAD_DIM), HEAD_DIM ** -0.5) for p in pairs]
    k = [l2n(conv_silu((B_QK_HEADS + p) * HEAD_DIM), 1.0) for p in pairs]
    v = [conv_silu((2 * B_QK_HEADS + h) * HEAD_DIM) for h in heads]
    qbf = [q[p].astype(BF16) for p in pairs]
    kbf = [k[p].astype(BF16) for p in pairs]

    row = lax.broadcasted_iota(I32, (c, LANES), 0)
    lane = lax.broadcasted_iota(I32, (c, LANES), 1)
    col = lane & (c - 1)
    lane_lo = lane < c
    lane_hi = jnp.logical_not(lane_lo)
    incl = row >= col
    strict = row > col
    eye2 = jnp.where(row == col, 1.0, 0.0)
    r64 = lax.broadcasted_iota(I32, (c, c), 0)
    c64 = lax.broadcasted_iota(I32, (c, c), 1)
    lower = jnp.where(r64 >= c64, 1.0, 0.0)
    r128 = lax.broadcasted_iota(I32, (LANES, LANES), 0)
    c128 = lax.broadcasted_iota(I32, (LANES, LANES), 1)
    upper2 = jnp.where(jnp.logical_and(r128 // c == c128 // c, r128 <= c128), 1.0, 0.0)

    beta_col = _sigmoid(ps_ref[:, 2 * LANES:2 * LANES + B_V_HEADS])
    g_col = -jnp.exp(alr_ref[...]) * _softplus(ps_ref[:, 3 * LANES:3 * LANES + B_V_HEADS] + dtr_ref[...])
    g_row2 = -jnp.exp(alc_ref[...]) * _softplus(pst_ref[0] + dtc_ref[...])
    gc_col = _dot_hi(lower, g_col)
    gc_row2 = _dot_hi(g_row2, upper2)
    eg_col = jnp.exp(gc_col)
    eg_last = eg_col[c - 1:c, :]

    def col_pair(m, p):
        return jnp.where(lane_lo, m[:, 2 * p:2 * p + 1], m[:, 2 * p + 1:2 * p + 2])

    g = [_dot_nt(jnp.concatenate([qbf[p], kbf[p]], axis=0),
                 jnp.concatenate([kbf[p], kbf[p]], axis=0)) for p in pairs]
    decay = [jnp.exp(jnp.where(incl, col_pair(gc_col, p) - gc_row2[p:p + 1, :], NEG)) for p in pairs]
    a_mat = [jnp.where(strict, col_pair(beta_col, p) * g[p][c:] * decay[p], 0.0) for p in pairs]
    attn = [g[p][:c] * decay[p] for p in pairs]
    e = _tri_inverse_minus_eye(a_mat, row, col, lane_lo)

    def rhs_of(p, h):
        bc = beta_col[:, h:h + 1]
        return jnp.concatenate([v[h] * bc, k[p] * (bc * eg_col[:, h:h + 1])], axis=1)

    rhs = [jnp.concatenate([rhs_of(p, 2 * p), rhs_of(p, 2 * p + 1)], axis=0).astype(BF16) for p in pairs]
    t_mat = [eye2 + e[p] for p in pairs]
    uw = [_dot(jnp.where(sel, t_mat[h // 2], 0.0).astype(BF16), rhs[h // 2])
          for h in heads for sel in ((lane_lo, lane_hi)[h % 2],)]
    wq = [jnp.concatenate([uw[h][:, HEAD_DIM:], q[h // 2] * eg_col[:, h:h + 1]], axis=0).astype(BF16)
          for h in heads]
    kd = [k[h // 2] * jnp.exp(gc_col[c - 1:c, h:h + 1] - gc_col[:, h:h + 1]) for h in heads]
    ak = [jnp.concatenate([attn[p], jnp.concatenate([kd[2 * p], kd[2 * p + 1]], axis=0).T], axis=0).astype(BF16)
          for p in pairs]

    lane_lo_b = lax.broadcasted_iota(I32, (c + HEAD_DIM, LANES), 1) < c
    zero = jnp.zeros((), BF16)
    state = [s_ref[h] for h in heads]
    r1 = [_dot(wq[h], state[h].astype(BF16)) for h in heads]
    vn = [uw[h][:, :HEAD_DIM] - r1[h][:c] for h in heads]
    r2 = []
    for p in pairs:
        vst = jnp.concatenate([vn[2 * p], vn[2 * p + 1]], axis=0).astype(BF16)
        r2.append(_dot(jnp.where(lane_lo_b, ak[p], zero), vst))
        r2.append(_dot(jnp.where(lane_lo_b, zero, ak[p]), vst))
    gn = gn_ref[...]
    for h in heads:
        sl = slice(h * HEAD_DIM, (h + 1) * HEAD_DIM)
        o = r1[h][c:] + r2[h][:c]
        s_ref[h] = state[h] * eg_last[:, h:h + 1] + r2[h][c:]
        on = o * lax.rsqrt(jnp.mean(o * o, axis=-1, keepdims=True) + EPS) * gn
        z = z_ref[:, sl].astype(F32)
        o_ref[:, sl] = (on * (z * _sigmoid(z))).astype(o_ref.dtype)


def _gdn(p4, conv_w, ps, pst, a_log, dt_bias, z, out_norm):
    t, ch = p4.shape
    n = t // CHUNK
    hv = B_V_HEADS
    fix2 = lambda i: (0, 0)
    return pl.pallas_call(
        _gdn_kernel,
        grid=(n,),
        in_specs=[
            pl.BlockSpec((CHUNK, ch), lambda i: (i, 0)),
            pl.BlockSpec((8, ch), lambda i: (jnp.maximum(i * (CHUNK // 8) - 1, 0), 0)),
            pl.BlockSpec((CONV_WIDTH, ch), fix2),
            pl.BlockSpec((CHUNK, ps.shape[1]), lambda i: (i, 0)),
            pl.BlockSpec((1, N_PAIRS, LANES), lambda i: (i, 0, 0)),
            pl.BlockSpec((1, hv), fix2), pl.BlockSpec((N_PAIRS, LANES), fix2),
            pl.BlockSpec((1, hv), fix2), pl.BlockSpec((N_PAIRS, LANES), fix2),
            pl.BlockSpec((CHUNK, hv * HEAD_DIM), lambda i: (i, 0)),
            pl.BlockSpec((1, HEAD_DIM), fix2),
        ],
        out_specs=pl.BlockSpec((CHUNK, hv * HEAD_DIM), lambda i: (i, 0)),
        out_shape=jax.ShapeDtypeStruct((t, hv * HEAD_DIM), BF16),
        scratch_shapes=[pltpu.VMEM((hv, HEAD_DIM, HEAD_DIM), F32)],
        compiler_params=_cp("arbitrary"),
    )(p4, p4, conv_w, ps, pst, a_log.reshape(1, hv), _pair_lanes(a_log), dt_bias.reshape(1, hv),
      _pair_lanes(dt_bias), z, out_norm.reshape(1, HEAD_DIM))


def _pair_lanes(per_head):
    return jnp.repeat(per_head.reshape(N_PAIRS, 2), CHUNK, axis=1)


_A_W = A_HEADS * HEAD_DIM
_A_IQ = IDX_HEADS * IDX_DIM
_B_QK = B_QK_HEADS * HEAD_DIM
_B_V = B_V_HEADS * HEAD_DIM
_OFF_AIK = 3 * _A_W + _A_IQ
_OFF_AIW = _OFF_AIK + IDX_DIM
_OFF_BQ = _OFF_AIW + IDX_HEADS
_OFF_BZ = _OFF_BQ + 2 * _B_QK + _B_V
_OFF_BB = _OFF_BZ + _B_V
_OFF_BA = _OFF_BB + B_V_HEADS
_OFF_GATE = _OFF_BA + B_V_HEADS


def _small_weights(w_in):
    d = w_in.shape[0]
    pad = lambda a: jnp.pad(a, ((0, 0), (0, LANES - a.shape[1])))
    return jnp.concatenate([
        pad(w_in[:, _OFF_AIK:_OFF_AIK + IDX_DIM]), pad(w_in[:, _OFF_AIW:_OFF_AIW + IDX_HEADS]),
        pad(w_in[:, _OFF_BB:_OFF_BB + B_V_HEADS]), pad(w_in[:, _OFF_BA:_OFF_BA + B_V_HEADS]),
    ], axis=1).astype(BF16)


def _ba_weights_t(w_in):
    w = w_in[:, _OFF_BA:_OFF_BA + B_V_HEADS]
    return jnp.concatenate([w[:, 0::2], w[:, 1::2]], axis=1).T.astype(BF16)


def _rope_freqs(rot, width):
    half = rot // 2
    inv = jnp.power(ROPE_THETA, -jnp.arange(half, dtype=F32) * 2.0 / rot)
    pat = jnp.concatenate([inv, inv, jnp.zeros((width - rot,), F32)])
    return jnp.tile(pat, LANES // width).reshape(1, LANES)


def _dsa_branch(h, w_in, ps, positions, q_norm, k_norm, idx_k_norm):
    t = h.shape[0]
    tm = 1024
    (p1,) = _matmul(h, w_in[:, 0:2 * _A_W].astype(BF16), [F32], tm, 1024)
    (v,) = _matmul(h, w_in[:, 2 * _A_W:3 * _A_W].astype(BF16), [BF16], tm, 1024)
    (p3,) = _matmul(h, w_in[:, 3 * _A_W:3 * _A_W + _A_IQ].astype(BF16), [F32], tm, 1024)
    ikn = jnp.pad(idx_k_norm.reshape(1, IDX_DIM), ((0, 0), (0, LANES - IDX_DIM)))
    q, k, q_idx, k_idx, w_idx = _dsa_prep(
        p1, p3, ps, positions.reshape(t, 1), _rope_freqs(HEAD_DIM // 4, HEAD_DIM), _rope_freqs(IDX_DIM // 4, IDX_DIM),
        q_norm.reshape(1, HEAD_DIM), k_norm.reshape(1, HEAD_DIM), ikn)
    bias = _indexer(q_idx, w_idx, k_idx, min(TOPK_MAX, t // 4))
    return _attention(q, k, v, bias)


def _gdn_branch(h, w_in, ps, pst, conv_w, a_log, dt_bias, out_norm):
    tm = 1024
    (p4,) = _matmul(h, w_in[:, _OFF_BQ:_OFF_BZ].astype(BF16), [F32], tm, 1024)
    (z,) = _matmul(h, w_in[:, _OFF_BZ:_OFF_BB].astype(BF16), [BF16], tm, 1024)
    return _gdn(p4, conv_w, ps, pst, a_log, dt_bias, z, out_norm)


def _gate_a_epilogue(acc, g):
    return (_sigmoid(g.astype(F32)) * acc,)


def _gate_b_epilogue(acc, g, m):
    return (m + _sigmoid(g.astype(F32)) * acc,)


def _residual_norm_epilogue(acc, x, gain):
    x1 = x + acc
    h2 = x1 * lax.rsqrt(jnp.mean(x1 * x1, axis=-1, keepdims=True) + EPS) * gain
    return x1, h2


def _mix(x, h, w_in, out_a, out_b, w_o_a, w_o_b, w_out, ffn_norm):
    d = x.shape[1]
    (ga,) = _matmul(h, w_in[:, _OFF_GATE:_OFF_GATE + d].astype(BF16), [BF16], 1024, 1024)
    (gb,) = _matmul(h, w_in[:, _OFF_GATE + d:_OFF_GATE + 2 * d].astype(BF16), [BF16], 1024, 1024)
    (mixa,) = _matmul(out_a, w_o_a.astype(BF16), [F32], 1024, 1024, _gate_a_epilogue, [(ga, "tile")])
    (mix,) = _matmul(out_b, w_o_b.astype(BF16), [BF16], 1024, 512, _gate_b_epilogue, [(gb, "tile"), (mixa, "tile")])
    return _matmul(mix, w_out.astype(BF16), [F32, F32], 512, d, _residual_norm_epilogue,
                   [(x, "tile"), (ffn_norm.reshape(1, d), "row")])


MOE_TM = 256
MOE_NF = 1
_R_E1, _R_E2, _R_RANK1, _R_RANK2, _R_W1, _R_W2 = 0, 1, 2, 3, 4, 5


def _router_kernel(h_ref, w_ref, b_ref, r_ref, cnt_ref, carry_ref):
    i = pl.program_id(0)

    @pl.when(i == 0)
    def _():
        carry_ref[...] = jnp.zeros_like(carry_ref)

    tb = h_ref.shape[0]
    logits = _dot(h_ref[...].astype(BF16), w_ref[...]) + b_ref[...]
    le = logits[:, :N_EXPERTS]
    lg = logits[:, N_EXPERTS:]
    lane = lax.broadcasted_iota(I32, (tb, LANES), 1)
    lanef = lane.astype(F32)
    far = float(4 * LANES)

    lgm = jnp.where(lane < N_GROUPS, lg, NEG)
    gmax = jnp.max(lgm, axis=-1, keepdims=True)
    g_idx = jnp.min(jnp.where(lgm == gmax, lanef, far), axis=-1, keepdims=True)
    g_top = 1.0 / jnp.sum(jnp.exp(lgm - gmax), axis=-1, keepdims=True)

    in_grp = (lane // EXPERTS_PER_GROUP).astype(F32) == g_idx
    lem = jnp.where(in_grp, le, NEG)
    e1 = jnp.max(lem, axis=-1, keepdims=True)
    i1 = jnp.min(jnp.where(lem == e1, lanef, far), axis=-1, keepdims=True)
    lem2 = jnp.where(lanef == i1, NEG, lem)
    e2 = jnp.max(lem2, axis=-1, keepdims=True)
    i2 = jnp.min(jnp.where(lem2 == e2, lanef, far), axis=-1, keepdims=True)
    se = jnp.sum(jnp.exp(lem - e1), axis=-1, keepdims=True)
    p1 = 1.0 / se
    p2 = jnp.exp(e2 - e1) / se
    w1 = g_top * p1 / (p1 + p2)
    w2 = g_top * p2 / (p1 + p2)

    o1 = jnp.where(lanef == i1, 1.0, 0.0)
    o2 = jnp.where(lanef == i2, 1.0, 0.0)
    osum = o1 + o2
    rr = lax.broadcasted_iota(I32, (tb, tb), 0)
    cc = lax.broadcasted_iota(I32, (tb, tb), 1)
    before = jnp.where(cc < rr, 1.0, 0.0).astype(BF16)
    prefix = _dot(before, osum.astype(BF16)) + carry_ref[0:1, :]
    rank1 = jnp.sum(prefix * o1, axis=-1, keepdims=True)
    rank2 = jnp.sum(prefix * o2, axis=-1, keepdims=True)
    carry_ref[...] = carry_ref[...] + jnp.sum(osum, axis=0, keepdims=True)
    cnt_ref[...] = carry_ref[...]

    rec = jnp.zeros((tb, LANES), F32)
    for idx, val in ((_R_E1, i1), (_R_E2, i2), (_R_RANK1, rank1), (_R_RANK2, rank2), (_R_W1, w1), (_R_W2, w2)):
        rec = jnp.where(lane == idx, val, rec)
    r_ref[...] = rec


def _router(h2, w_router, b_router):
    t, d = h2.shape
    tb = min(t, 512)
    return pl.pallas_call(
        _router_kernel,
        grid=(t // tb,),
        in_specs=[pl.BlockSpec((tb, d), lambda i: (i, 0)), pl.BlockSpec((d, 2 * LANES), lambda i: (0, 0)),
                  pl.BlockSpec((1, 2 * LANES), lambda i: (0, 0))],
        out_specs=[pl.BlockSpec((tb, LANES), lambda i: (i, 0)), pl.BlockSpec((8, LANES), lambda i: (0, 0))],
        out_shape=[jax.ShapeDtypeStruct((t, LANES), F32), jax.ShapeDtypeStruct((8, LANES), F32)],
        scratch_shapes=[pltpu.VMEM((8, LANES), F32)],
        compiler_params=_cp("arbitrary"),
    )(h2, w_router, b_router)


def _positions_kernel(r_ref, cnt_ref, pos_ref, te_ref, nv_ref):
    t = r_ref.shape[0]
    nt = te_ref.shape[0]
    cnt = cnt_ref[...]
    tiles = jnp.floor((cnt + (MOE_TM - 1)) * (1.0 / MOE_TM))
    rr = lax.broadcasted_iota(I32, (LANES, LANES), 0)
    cc = lax.broadcasted_iota(I32, (LANES, LANES), 1)
    start_tiles = _dot(tiles.astype(BF16), jnp.where(rr < cc, 1.0, 0.0).astype(BF16))
    start = start_tiles[0:1, :] * float(MOE_TM)
    rec = r_ref[...]
    lane = lax.broadcasted_iota(I32, (t, LANES), 1)
    lanef = lane.astype(F32)
    pos1 = jnp.sum(jnp.where(lanef == rec[:, _R_E1:_R_E1 + 1], start, 0.0), axis=-1, keepdims=True) \
        + rec[:, _R_RANK1:_R_RANK1 + 1]
    pos2 = jnp.sum(jnp.where(lanef == rec[:, _R_E2:_R_E2 + 1], start, 0.0), axis=-1, keepdims=True) \
        + rec[:, _R_RANK2:_R_RANK2 + 1]
    pos_ref[...] = jnp.where(lane == 0, pos1, jnp.where(lane == 1, pos2, 0.0)).astype(I32)
    tile_id = lax.broadcasted_iota(I32, (nt, LANES), 0).astype(F32)
    tlane = lax.broadcasted_iota(I32, (nt, LANES), 1)
    owner = jnp.sum(jnp.where(start_tiles[0:1, :] <= tile_id, 1.0, 0.0), axis=-1, keepdims=True) - 1.0
    is_owner = tlane.astype(F32) == owner
    own_cnt = jnp.sum(jnp.where(is_owner, cnt[0:1, :], 0.0), axis=-1, keepdims=True)
    own_start = jnp.sum(jnp.where(is_owner, start_tiles[0:1, :], 0.0), axis=-1, keepdims=True)
    rows = jnp.clip(own_cnt - (tile_id[:, 0:1] - own_start) * float(MOE_TM), 0.0, float(MOE_TM))
    te_ref[...] = jnp.where(tlane == 0, owner, jnp.where(tlane == 1, rows, 0.0)).astype(I32)
    nv_ref[...] = jnp.broadcast_to(jnp.sum(tiles[0:1, :], axis=-1, keepdims=True), (8, LANES)).astype(I32)


def _positions(rec, cnt, n_tiles):
    t = rec.shape[0]
    full = lambda shape: pl.BlockSpec(shape, lambda i: (0, 0))
    return pl.pallas_call(
        _positions_kernel,
        grid=(1,),
        in_specs=[full((t, LANES)), full((8, LANES))],
        out_specs=[full((t, LANES)), full((n_tiles, LANES)), full((8, LANES))],
        out_shape=[jax.ShapeDtypeStruct((t, LANES), I32), jax.ShapeDtypeStruct((n_tiles, LANES), I32),
                   jax.ShapeDtypeStruct((8, LANES), I32)],
        compiler_params=_cp("arbitrary"),
    )(rec, cnt)


def _invert_kernel(pos_ref, asg_ref):
    def scatter(a, _):
        asg_ref[pos_ref[a]] = a
        return 0

    lax.fori_loop(0, pos_ref.shape[0], scatter, 0, unroll=8)


def _invert(pos, n_slots):
    return pl.pallas_call(
        _invert_kernel,
        in_specs=[pl.BlockSpec(memory_space=pltpu.SMEM)],
        out_specs=pl.BlockSpec(memory_space=pltpu.SMEM),
        out_shape=jax.ShapeDtypeStruct((n_slots,), I32),
    )(pos)


ROW_UNROLL = 4


def _for_each_row(n, body):
    full = lax.shift_right_logical(n, ROW_UNROLL.bit_length() - 1)

    def group(g, _):
        for u in range(ROW_UNROLL):
            body(g * ROW_UNROLL + u)
        return 0

    lax.fori_loop(0, full, group, 0)

    def single(r, _):
        body(r)
        return 0

    lax.fori_loop(full * ROW_UNROLL, n, single, 0)


def _ffn_kernel(te_ref, rows_ref, nv_ref, asg_ref, h_ref, wg_ref, wu_ref, wd_ref, y_ref,
                xbuf_ref, x16_ref, obuf_ref, gsem, ssem):
    i = pl.program_id(0)
    f = pl.program_id(1)
    tm, d = x16_ref.shape
    nv = nv_ref[0]
    last_f = pl.num_programs(1) - 1

    def gather_copy(tile, slot, r):
        token = lax.shift_right_logical(asg_ref[tile * tm + r], 1)
        return pltpu.make_async_copy(h_ref.at[pl.ds(token, 1)], xbuf_ref.at[slot, pl.ds(r, 1)], gsem.at[slot])

    def scatter_copy(tile, r):
        a = asg_ref[tile * tm + r]
        col = pl.multiple_of(jnp.bitwise_and(a, 1) * d, d)
        return pltpu.make_async_copy(obuf_ref.at[pl.ds(r, 1)],
                                     y_ref.at[pl.ds(lax.shift_right_logical(a, 1), 1), pl.ds(col, d)], ssem)

    def gather(tile, slot):
        _for_each_row(rows_ref[tile], lambda r: gather_copy(tile, slot, r).start())

    def scatter_wait(tile):
        _for_each_row(rows_ref[tile], lambda r: scatter_copy(tile, r).wait())

    @pl.when(jnp.logical_and(i == 0, f == 0))
    def _():
        xbuf_ref[...] = jnp.zeros_like(xbuf_ref)
        gather(0, 0)

    @pl.when(jnp.logical_and(f == 0, i + 1 < nv))
    def _():
        gather(i + 1, jnp.bitwise_and(i + 1, 1))

    @pl.when(jnp.logical_and(f == 0, i < nv))
    def _():
        slot = jnp.bitwise_and(i, 1)
        _for_each_row(rows_ref[i], lambda r: gather_copy(i, slot, r).wait())
        x16_ref[...] = xbuf_ref[slot].astype(x16_ref.dtype)

    @pl.when(i < nv)
    def _():
        x = x16_ref[...]
        g = _dot(x, wg_ref[0].astype(BF16))
        u = _dot(x, wu_ref[0].astype(BF16))
        act = (g * _sigmoid(g)) * u
        y = _dot(act.astype(BF16), wd_ref[0].astype(BF16))

        @pl.when(f == 0)
        def _():
            @pl.when(i > 0)
            def _():
                scatter_wait(i - 1)

            obuf_ref[...] = y

        @pl.when(f > 0)
        def _():
            obuf_ref[...] = obuf_ref[...] + y

        @pl.when(f == last_f)
        def _():
            _for_each_row(rows_ref[i], lambda r: scatter_copy(i, r).start())

            @pl.when(i == nv - 1)
            def _():
                scatter_wait(i)


def _ffn(te, rows, nv, asg, h2, w_gate, w_up, w_down, n_tiles):
    t, d = h2.shape
    fb = D_FF // MOE_NF

    def tile(i, nv):
        return jnp.minimum(i, nv[0] - 1)

    def fblk(i, f, nv):
        return jnp.where(i < nv[0], f, MOE_NF - 1)

    any_spec = pl.BlockSpec(memory_space=pl.ANY)
    return pl.pallas_call(
        _ffn_kernel,
        grid_spec=pltpu.PrefetchScalarGridSpec(
            num_scalar_prefetch=4,
            grid=(n_tiles, MOE_NF),
            in_specs=[
                any_spec,
                pl.BlockSpec((1, d, fb), lambda i, f, te, rows, nv, asg: (te[tile(i, nv)], 0, fblk(i, f, nv))),
                pl.BlockSpec((1, d, fb), lambda i, f, te, rows, nv, asg: (te[tile(i, nv)], 0, fblk(i, f, nv))),
                pl.BlockSpec((1, fb, d), lambda i, f, te, rows, nv, asg: (te[tile(i, nv)], fblk(i, f, nv), 0)),
            ],
            out_specs=any_spec,
            scratch_shapes=[pltpu.VMEM((2, MOE_TM, d), F32), pltpu.VMEM((MOE_TM, d), BF16),
                            pltpu.VMEM((MOE_TM, d), F32), pltpu.SemaphoreType.DMA((2,)),
                            pltpu.SemaphoreType.DMA(())],
        ),
        out_shape=jax.ShapeDtypeStruct((t, 2 * d), F32),
        compiler_params=_cp("arbitrary", "arbitrary"),
    )(te, rows, nv, asg, h2, w_gate, w_up, w_down)


def _combine_kernel(y_ref, x_ref, r_ref, o_ref):
    d = x_ref.shape[1]
    rec = r_ref[...]
    o_ref[...] = x_ref[...] + rec[:, _R_W1:_R_W1 + 1] * y_ref[:, :d] + rec[:, _R_W2:_R_W2 + 1] * y_ref[:, d:]


def _combine(y, x1, rec):
    t, d = x1.shape
    tb = min(t, 256)
    return pl.pallas_call(
        _combine_kernel,
        grid=(t // tb,),
        in_specs=[pl.BlockSpec((tb, 2 * d), lambda i: (i, 0)), pl.BlockSpec((tb, d), lambda i: (i, 0)),
                  pl.BlockSpec((tb, LANES), lambda i: (i, 0))],
        out_specs=pl.BlockSpec((tb, d), lambda i: (i, 0)),
        out_shape=jax.ShapeDtypeStruct((t, d), F32),
        compiler_params=_cp("parallel"),
    )(y, x1, rec)


def _moe(x1, h2, w_rg, b_rg, w_re, b_re, w_gate, w_up, w_down):
    t, d = x1.shape
    pad_w = jnp.zeros((d, LANES - N_GROUPS), F32)
    w_router = jnp.concatenate([w_re, w_rg, pad_w], axis=1).astype(BF16)
    b_router = jnp.concatenate([b_re, b_rg, jnp.zeros((LANES - N_GROUPS,), F32)]).reshape(1, 2 * LANES)
    n_tiles = 2 * t // MOE_TM + N_EXPERTS
    rec, cnt = _router(h2, w_router, b_router)
    pos2d, te2d, nv2d = _positions(rec, cnt, n_tiles)
    pos = pos2d[:, :2].reshape(2 * t)
    te = te2d[:, 0]
    rows = te2d[:, 1]
    nv = nv2d[0, :1]
    asg = _invert(pos, n_tiles * MOE_TM)
    y = _ffn(te, rows, nv, asg, h2, w_gate, w_up, w_down, n_tiles)
    return _combine(y, x1, rec)


def kernel(x, positions, attn_norm, w_in, q_norm, k_norm, idx_k_norm, conv_w, a_log, dt_bias, gdn_out_norm,
           w_o_a, w_o_b, w_out, ffn_norm, w_router_group, b_router_group, w_router_expert, b_router_expert,
           w_gate, w_up, w_down):
    b, t, d = x.shape
    x2 = x.reshape(t, d)
    layer = 0
    h = _rmsnorm(x2, attn_norm[layer])
    w_small = _small_weights(w_in[layer])
    (ps,) = _matmul(h, w_small, [F32], 1024, w_small.shape[1])
    pst = _matmul_nt(_ba_weights_t(w_in[layer]), h, 1024)
    out_a = _dsa_branch(h, w_in[layer], ps, positions, q_norm[layer], k_norm[layer], idx_k_norm[layer])
    out_b = _gdn_branch(h, w_in[layer], ps, pst, conv_w[layer], a_log[layer], dt_bias[layer], gdn_out_norm[layer])
    x1, h2 = _mix(x2, h, w_in[layer], out_a, out_b, w_o_a[layer], w_o_b[layer], w_out[layer], ffn_norm[layer])
    out = _moe(x1, h2, w_router_group[layer], b_router_group[layer], w_router_expert[layer], b_router_expert[layer],
               w_gate[layer], w_up[layer], w_down[layer])
    return out.reshape(b, t, d)
```

```python
import functools

import jax
import jax.numpy as jnp
from jax import lax
from jax.experimental import pallas as pl
from jax.experimental.pallas import tpu as pltpu

F32 = jnp.float32
BF16 = jnp.bfloat16
I32 = jnp.int32

EPS = 1e-6
NEG = -1e30
BIG = 1e30
LOG2_E = 1.4426950408889634
ROPE_THETA = 500000.0
CHUNK = 64
A_HEADS = 16
HEAD_DIM = 128
IDX_HEADS = 16
IDX_DIM = 64
TOPK_MAX = 256
Q_BLOCK = 128
IDX_ROWS = 128
B_QK_HEADS = 16
B_V_HEADS = 32
CONV_WIDTH = 4
N_GROUPS = 8
EXPERTS_PER_GROUP = 16
N_EXPERTS = 128
D_FF = 768
LANES = 128
VMEM_LIMIT = 56 * 1024 * 1024

NT_DIMS = (((1,), (1,)), ((), ()))


def _cp(*sem):
    return pltpu.CompilerParams(dimension_semantics=sem, vmem_limit_bytes=VMEM_LIMIT)


def _dot(a, b):
    return jnp.dot(a, b, preferred_element_type=F32)


def _dot_nt(a, b):
    return lax.dot_general(a, b, NT_DIMS, preferred_element_type=F32)


def _dot_hi(a, b):
    return jnp.dot(a, b, preferred_element_type=F32, precision=lax.Precision.HIGHEST)


def _sigmoid(x):
    return 1.0 / (1.0 + jnp.exp(-x))


def _softplus(x):
    return jnp.maximum(x, 0.0) + jnp.log(1.0 + jnp.exp(-jnp.abs(x)))


def _rmsnorm_kernel(x_ref, g_ref, o_ref):
    x = x_ref[...]
    ms = jnp.mean(x * x, axis=-1, keepdims=True)
    o_ref[...] = (x * lax.rsqrt(ms + EPS) * g_ref[...]).astype(o_ref.dtype)


def _rmsnorm(x, gain, out_dtype=BF16):
    t, d = x.shape
    tm = min(t, 512)
    return pl.pallas_call(
        _rmsnorm_kernel,
        grid=(t // tm,),
        in_specs=[pl.BlockSpec((tm, d), lambda i: (i, 0)), pl.BlockSpec((1, d), lambda i: (0, 0))],
        out_specs=pl.BlockSpec((tm, d), lambda i: (i, 0)),
        out_shape=jax.ShapeDtypeStruct((t, d), out_dtype),
        compiler_params=_cp("parallel"),
    )(x, gain.reshape(1, d))


def _mm_kernel(*refs, n_extra, epilogue):
    a_ref, b_ref = refs[:2]
    extra = refs[2:2 + n_extra]
    outs = refs[2 + n_extra:]
    acc = _dot(a_ref[...], b_ref[...])
    res = epilogue(acc, *[e[...] for e in extra]) if epilogue is not None else (acc,)
    for o_ref, r in zip(outs, res):
        o_ref[...] = r.astype(o_ref.dtype)


def _matmul(a, b, out_dtypes, tm, tn, epilogue=None, extras=(), b_cols=None):
    m, k = a.shape
    start, n = b_cols if b_cols is not None else (0, b.shape[1])
    tm, tn = min(tm, m), min(tn, n)
    assert start % tn == 0 and n % tn == 0 and m % tm == 0
    j0 = start // tn
    in_specs = [pl.BlockSpec((tm, k), lambda i, j: (i, 0)), pl.BlockSpec((k, tn), lambda i, j: (0, j + j0))]
    args = [a, b]
    for arr, kind in extras:
        if kind == "tile":
            in_specs.append(pl.BlockSpec((tm, tn), lambda i, j: (i, j)))
        elif kind == "rows":
            in_specs.append(pl.BlockSpec((tm, LANES), lambda i, j: (i, 0)))
        else:
            in_specs.append(pl.BlockSpec((1, tn), lambda i, j: (0, j)))
        args.append(arr)
    out = pl.pallas_call(
        functools.partial(_mm_kernel, n_extra=len(extras), epilogue=epilogue),
        grid=(m // tm, n // tn),
        in_specs=in_specs,
        out_specs=[pl.BlockSpec((tm, tn), lambda i, j: (i, j)) for _ in out_dtypes],
        out_shape=[jax.ShapeDtypeStruct((m, n), dt) for dt in out_dtypes],
        compiler_params=_cp("parallel", "arbitrary"),
    )(*args)
    return out


def _mm_nt_kernel(a_ref, b_ref, o_ref):
    res = _dot_nt(a_ref[...], b_ref[...])
    half = res.shape[0] // 2
    for c in range(o_ref.shape[0]):
        cols = slice(c * CHUNK, (c + 1) * CHUNK)
        o_ref[c] = jnp.concatenate([res[:half, cols], res[half:, cols]], axis=1)


def _matmul_nt(a, b, tm):
    n, k = a.shape
    m = b.shape[0]
    tm = min(tm, m)
    return pl.pallas_call(
        _mm_nt_kernel,
        grid=(m // tm,),
        in_specs=[pl.BlockSpec((n, k), lambda i: (0, 0)), pl.BlockSpec((tm, k), lambda i: (i, 0))],
        out_specs=pl.BlockSpec((tm // CHUNK, n // 2, 2 * CHUNK), lambda i: (i, 0, 0)),
        out_shape=jax.ShapeDtypeStruct((m // CHUNK, n // 2, 2 * CHUNK), F32),
        compiler_params=_cp("parallel"),
    )(a, b)


A_ROT_HALF = HEAD_DIM // 8
I_ROT_HALF = IDX_DIM // 8


def _rope_a(x, cos, sin):
    lane = lax.broadcasted_iota(I32, x.shape, 1)
    partner = jnp.where(lane < A_ROT_HALF, pltpu.roll(x, LANES - A_ROT_HALF, 1), pltpu.roll(x, A_ROT_HALF, 1))
    return x * cos + partner * sin


def _rope_i(x, cos, sin):
    lane = lax.broadcasted_iota(I32, x.shape, 1)
    low = (lane & (IDX_DIM - 1)) < I_ROT_HALF
    partner = jnp.where(low, pltpu.roll(x, LANES - I_ROT_HALF, 1), pltpu.roll(x, I_ROT_HALF, 1))
    return x * cos + partner * sin


def _dsa_prep_kernel(ps_ref, pos_ref, ifa_ref, ifi_ref, ikn_ref,
                     cosa_ref, sina_ref, cosi_ref, sini_ref, ki_ref, wi_ref):
    tm = ps_ref.shape[0]
    pos = pos_ref[...].astype(F32)
    lane = lax.broadcasted_iota(I32, (tm, LANES), 1)
    ang_a = pos * ifa_ref[...]
    sin_a = jnp.sin(ang_a)
    cosa_ref[...] = jnp.cos(ang_a)
    sina_ref[...] = jnp.where(lane < A_ROT_HALF, -sin_a, sin_a)
    ang_i = pos * ifi_ref[...]
    cos_i = jnp.cos(ang_i)
    sin_i = jnp.sin(ang_i)
    sin_i = jnp.where((lane & (IDX_DIM - 1)) < I_ROT_HALF, -sin_i, sin_i)
    cosi_ref[...] = cos_i
    sini_ref[...] = sin_i
    xk = ps_ref[:, 0:LANES]
    ms = jnp.sum(xk * xk, axis=-1, keepdims=True) * (1.0 / IDX_DIM)
    yk = xk * lax.rsqrt(ms + EPS) * ikn_ref[...]
    ki_ref[...] = _rope_i(yk, cos_i, sin_i).T[:IDX_DIM, :].astype(ki_ref.dtype)
    wi_ref[...] = ps_ref[:, LANES:LANES + IDX_HEADS] * (IDX_HEADS ** -0.5 * IDX_DIM ** -0.5)


def _dsa_prep(ps, positions, ifa, ifi, ikn):
    t = ps.shape[0]
    tm = min(t, 512)
    row = lambda i: (i, 0)
    fix = lambda i: (0, 0)
    table = jax.ShapeDtypeStruct((t, LANES), F32)
    return pl.pallas_call(
        _dsa_prep_kernel,
        grid=(t // tm,),
        in_specs=[pl.BlockSpec((tm, ps.shape[1]), row), pl.BlockSpec((tm, 1), row), pl.BlockSpec((1, LANES), fix),
                  pl.BlockSpec((1, LANES), fix), pl.BlockSpec((1, LANES), fix)],
        out_specs=[pl.BlockSpec((tm, LANES), row)] * 4
        + [pl.BlockSpec((IDX_DIM, tm), lambda i: (0, i)), pl.BlockSpec((tm, IDX_HEADS), row)],
        out_shape=[table] * 4 + [jax.ShapeDtypeStruct((IDX_DIM, t), BF16), jax.ShapeDtypeStruct((t, IDX_HEADS), F32)],
        compiler_params=_cp("parallel"),
    )(ps, positions, ifa, ifi, ikn)


def _qk_epilogue(acc, gain, cos, sin):
    heads = range(acc.shape[1] // HEAD_DIM)
    xs = [acc[:, h * HEAD_DIM:(h + 1) * HEAD_DIM] for h in heads]
    ms = [jnp.mean(x * x, axis=-1, keepdims=True) for x in xs]
    rs = [lax.rsqrt(m + EPS) for m in ms]
    ys = [xs[h] * rs[h] * gain[:, h * HEAD_DIM:(h + 1) * HEAD_DIM] for h in heads]
    return (jnp.concatenate([_rope_a(y, cos, sin) for y in ys], axis=1),)


def _qidx_epilogue(acc, cos, sin):
    return (jnp.concatenate([_rope_i(acc[:, j * LANES:(j + 1) * LANES], cos, sin)
                             for j in range(acc.shape[1] // LANES)], axis=1),)


def _indexer_kernel(qi_ref, w_ref, ki_ref, bias_ref, s_ref, wb_ref, qs_ref, *, kc, topk, maxit):
    i = pl.program_id(0)
    qb = qi_ref.shape[0]
    hb = Q_BLOCK
    halves = qb // hb
    n_ch = ((i + 1) * qb + kc - 1) // kc
    nslab = kc // LANES
    sub_kc = min(kc, 2 * LANES)
    for h in range(IDX_HEADS):
        wb_ref[h] = jnp.broadcast_to(w_ref[:, h:h + 1], (qb, LANES))
        for rh in range(halves):
            qs_ref[(rh * IDX_HEADS + h) * hb:(rh * IDX_HEADS + h + 1) * hb, :] = \
                qi_ref[rh * hb:(rh + 1) * hb, h * IDX_DIM:(h + 1) * IDX_DIM]
    row = lax.broadcasted_iota(I32, (qb, LANES), 0)
    lane = lax.broadcasted_iota(I32, (qb, LANES), 1)
    limit = ((i * qb + row) // CHUNK + 1) * CHUNK

    def score_body(c, carry):
        mn, mx = carry
        off = pl.multiple_of(c * kc, kc)
        accs = [[] for _ in range(halves)]
        for sub in range(kc // sub_kc):
            kblk = ki_ref[:, pl.ds(pl.multiple_of(off + sub * sub_kc, sub_kc), sub_kc)]
            for rh in range(halves):
                part = [jnp.zeros((hb, LANES), F32) for _ in range(sub_kc // LANES)]
                s_all = _dot(qs_ref[rh * IDX_HEADS * hb:(rh + 1) * IDX_HEADS * hb, :], kblk)
                for h in range(IDX_HEADS):
                    wbh = wb_ref[h, rh * hb:(rh + 1) * hb, :]
                    for j in range(sub_kc // LANES):
                        s = s_all[h * hb:(h + 1) * hb, j * LANES:(j + 1) * LANES]
                        part[j] = part[j] + wbh * jnp.maximum(s, 0.0)
                accs[rh] += part
        for j in range(nslab):
            acc = jnp.concatenate([accs[rh][j] for rh in range(halves)], axis=0) if halves > 1 else accs[0][j]
            adm = (lane + (off + j * LANES)) < limit
            val = jnp.where(adm, acc, NEG)
            s_ref[:, pl.ds(pl.multiple_of(off + j * LANES, LANES), LANES)] = val
            mn = jnp.minimum(mn, jnp.where(adm, acc, BIG))
            mx = jnp.maximum(mx, val)
        return mn, mx

    mn, mx = lax.fori_loop(0, n_ch, score_body,
                           (jnp.full((qb, LANES), BIG, F32), jnp.full((qb, LANES), NEG, F32)))
    lo0 = jnp.min(mn, axis=-1, keepdims=True)
    hi0 = jnp.max(mx, axis=-1, keepdims=True)
    kf = float(topk)

    def count_ge(thr):
        thr_b = jnp.broadcast_to(thr, (qb, LANES))

        def body(c, acc):
            off = pl.multiple_of(c * kc, kc)
            for j in range(nslab):
                sc = s_ref[:, pl.ds(pl.multiple_of(off + j * LANES, LANES), LANES)]
                acc = acc + jnp.where(sc >= thr_b, 1.0, 0.0)
            return acc

        acc = lax.fori_loop(0, n_ch, body, jnp.zeros((qb, LANES), F32))
        return jnp.sum(acc, axis=-1, keepdims=True)

    done0 = jnp.where(limit[:, 0:1] <= topk, 1.0, 0.0)

    def cond(st):
        it, _, _, done = st
        return jnp.logical_and(it < maxit, jnp.min(done) < 0.5)

    def body(st):
        it, lo, hi, done = st
        mid = lo + (hi - lo) * 0.5
        stuck = jnp.logical_or(mid <= lo, mid >= hi)
        c = count_ge(mid)
        ge = c >= kf
        lo = jnp.where(ge, mid, lo)
        hi = jnp.where(ge, hi, mid)
        done = jnp.maximum(done, jnp.where(jnp.logical_or(c == kf, stuck), 1.0, 0.0))
        return it + 1, lo, hi, done

    _, lo, hi, _ = lax.while_loop(cond, body, (jnp.int32(0), lo0, hi0, done0))
    c_hi = count_ge(hi)
    lo = jnp.where(c_hi >= kf, hi, lo)
    lo_b = jnp.broadcast_to(lo, (qb, LANES))

    bias_ref[...] = jnp.full(bias_ref.shape, NEG, bias_ref.dtype)

    def write_body(c, _):
        off = pl.multiple_of(c * kc, kc)
        for j in range(nslab):
            o = pl.multiple_of(off + j * LANES, LANES)
            sc = s_ref[:, pl.ds(o, LANES)]
            bias_ref[:, pl.ds(o, LANES)] = jnp.where(sc >= lo_b, 0.0, NEG).astype(bias_ref.dtype)
        return 0

    lax.fori_loop(0, n_ch, write_body, 0)


def _indexer(q_idx, w_idx, k_idx, topk):
    t = q_idx.shape[0]
    kc = min(t, 512)
    qb = IDX_ROWS
    return pl.pallas_call(
        functools.partial(_indexer_kernel, kc=kc, topk=topk, maxit=64),
        grid=(t // qb,),
        in_specs=[
            pl.BlockSpec((qb, IDX_HEADS * IDX_DIM), lambda i: (i, 0)),
            pl.BlockSpec((qb, IDX_HEADS), lambda i: (i, 0)),
            pl.BlockSpec((IDX_DIM, t), lambda i: (0, 0)),
        ],
        out_specs=pl.BlockSpec((qb, t), lambda i: (i, 0)),
        out_shape=jax.ShapeDtypeStruct((t, t), BF16),
        scratch_shapes=[pltpu.VMEM((qb, t), F32), pltpu.VMEM((IDX_HEADS, qb, LANES), F32),
                        pltpu.VMEM((IDX_HEADS * qb, IDX_DIM), BF16)],
        compiler_params=_cp("parallel"),
    )(q_idx, w_idx, k_idx)


def _last_key_block(qi, bq, bk):
    return ((qi + 1) * bq - 1) // bk


def _attn_kernel(q_ref, k_ref, v_ref, b_ref, o_ref, acc_ref, m_ref, l_ref):
    qi = pl.program_id(0)
    ki = pl.program_id(1)
    last = _last_key_block(qi, q_ref.shape[0], k_ref.shape[0])

    @pl.when(ki == 0)
    def _():
        acc_ref[...] = jnp.zeros_like(acc_ref)
        m_ref[...] = jnp.full(m_ref.shape, NEG, F32)
        l_ref[...] = jnp.zeros_like(l_ref)

    @pl.when(ki <= last)
    def _():
        bias = b_ref[...]
        ones = jnp.ones((k_ref.shape[0], HEAD_DIM), BF16)
        for h in range(A_HEADS):
            sl = slice(h * HEAD_DIM, (h + 1) * HEAD_DIM)
            s = _dot_nt(q_ref[:, sl], k_ref[:, sl]).astype(BF16) + bias
            m_prev = m_ref[h]
            m_new = jnp.maximum(m_prev, jnp.max(s, axis=-1, keepdims=True).astype(F32))
            alpha = jnp.exp2(m_prev - m_new)
            p = jnp.exp2(s - m_new[:, 0:1].astype(BF16))
            pv = _dot(p, jnp.concatenate([v_ref[:, sl], ones], axis=1))
            l_ref[h] = alpha * l_ref[h] + pv[:, HEAD_DIM:]
            acc_ref[:, sl] = alpha * acc_ref[:, sl] + pv[:, :HEAD_DIM]
            m_ref[h] = m_new

    @pl.when(ki == last)
    def _():
        for h in range(A_HEADS):
            sl = slice(h * HEAD_DIM, (h + 1) * HEAD_DIM)
            o_ref[:, sl] = (acc_ref[:, sl] / l_ref[h]).astype(o_ref.dtype)


def _attention(qk, v, bias):
    t, aw = v.shape
    bq = min(t, 512)
    bk = min(t, 1024)
    kv_map = lambda i, j: (jnp.minimum(j, _last_key_block(i, bq, bk)), 0)
    return pl.pallas_call(
        _attn_kernel,
        grid=(t // bq, t // bk),
        in_specs=[
            pl.BlockSpec((bq, aw), lambda i, j: (i, 0)),
            pl.BlockSpec((bk, aw), lambda i, j: (jnp.minimum(j, _last_key_block(i, bq, bk)), 1)),
            pl.BlockSpec((bk, aw), kv_map),
            pl.BlockSpec((bq, bk), lambda i, j: (i, jnp.minimum(j, _last_key_block(i, bq, bk)))),
        ],
        out_specs=pl.BlockSpec((bq, aw), lambda i, j: (i, 0)),
        out_shape=jax.ShapeDtypeStruct((t, aw), BF16),
        scratch_shapes=[pltpu.VMEM((bq, aw), F32), pltpu.VMEM((A_HEADS, bq, LANES), F32),
                        pltpu.VMEM((A_HEADS, bq, LANES), F32)],
        compiler_params=_cp("parallel", "arbitrary"),
    )(qk, qk, v, bias)


N_PAIRS = B_V_HEADS // 2


def _tri_inverse_minus_eye(a, row, col, lane_lo):
    n = len(a)
    lane_hi = jnp.logical_not(lane_lo)

    def mm(x, y):
        bd = jnp.concatenate([jnp.where(lane_lo, y, 0.0), jnp.where(lane_hi, y, 0.0)], axis=0)
        return _dot(x.astype(BF16), bd.astype(BF16))

    blk16 = (row // 16) == (col // 16)
    blk32 = (row // 32) == (col // 32)
    d1 = [jnp.where(blk16, a[p], 0.0) for p in range(n)]
    e = [-d1[p] for p in range(n)]
    dk = d1
    for _ in range(3):
        dk = [mm(dk[p], dk[p]) for p in range(n)]
        de = [mm(dk[p], e[p]) for p in range(n)]
        e = [e[p] + dk[p] + de[p] for p in range(n)]
    for sel in (jnp.logical_and(blk32, jnp.logical_not(blk16)), jnp.logical_not(blk32)):
        lo = [jnp.where(sel, a[p], 0.0) for p in range(n)]
        m1 = [lo[p] + mm(e[p], lo[p]) for p in range(n)]
        m2 = [m1[p] + mm(m1[p], e[p]) for p in range(n)]
        e = [e[p] - m2[p] for p in range(n)]
    return e


def _gdn_kernel(cur_ref, prev_ref, cw_ref, ps_ref, pst_ref, alr_ref, alc_ref, dtr_ref, dtc_ref, z_ref, gn_ref,
                o_ref, s_ref):
    c = CHUNK
    pairs = range(N_PAIRS)
    heads = range(B_V_HEADS)
    step = pl.program_id(0)

    @pl.when(step == 0)
    def _():
        s_ref[...] = jnp.zeros_like(s_ref)

    keep_prev = jnp.where(step > 0, 1.0, 0.0)
    halo = CONV_WIDTH - 1

    def conv_silu(col0):
        sl = slice(col0, col0 + HEAD_DIM)
        cur = cur_ref[:, sl]
        xcat = jnp.concatenate([prev_ref[:, sl] * keep_prev, cur], axis=0)
        y = cur * cw_ref[halo:halo + 1, sl]
        for tap in range(halo):
            y = y + xcat[8 - halo + tap:8 - halo + tap + c, :] * cw_ref[tap:tap + 1, sl]
        return y * _sigmoid(y)

    def l2n(y, scale):
        return y * (lax.rsqrt(jnp.sum(y * y, axis=-1, keepdims=True) + EPS) * scale)

    q = [l2n(conv_silu(p * HEAD_DIM), HEAD_DIM ** -0.5) for p in pairs]
    k = [l2n(conv_silu((B_QK_HEADS + p) * HEAD_DIM), 1.0) for p in pairs]
    v = [conv_silu((2 * B_QK_HEADS + h) * HEAD_DIM) for h in heads]
    qbf = [q[p].astype(BF16) for p in pairs]
    kbf = [k[p].astype(BF16) for p in pairs]

    row = lax.broadcasted_iota(I32, (c, LANES), 0)
    lane = lax.broadcasted_iota(I32, (c, LANES), 1)
    col = lane & (c - 1)
    lane_lo = lane < c
    lane_hi = jnp.logical_not(lane_lo)
    incl = row >= col
    strict = row > col
    eye2 = jnp.where(row == col, 1.0, 0.0)
    r64 = lax.broadcasted_iota(I32, (c, c), 0)
    c64 = lax.broadcasted_iota(I32, (c, c), 1)
    lower = jnp.where(r64 >= c64, 1.0, 0.0)
    r128 = lax.broadcasted_iota(I32, (LANES, LANES), 0)
    c128 = lax.broadcasted_iota(I32, (LANES, LANES), 1)
    upper2 = jnp.where(jnp.logical_and(r128 // c == c128 // c, r128 <= c128), 1.0, 0.0)

    beta_col = _sigmoid(ps_ref[:, 2 * LANES:2 * LANES + B_V_HEADS])
    g_col = -jnp.exp(alr_ref[...]) * _softplus(ps_ref[:, 3 * LANES:3 * LANES + B_V_HEADS] + dtr_ref[...])
    g_row2 = -jnp.exp(alc_ref[...]) * _softplus(pst_ref[0] + dtc_ref[...])
    gc_col = _dot_hi(lower, g_col)
    gc_row2 = _dot_hi(g_row2, upper2)
    eg_col = jnp.exp(gc_col)
    eg_last = eg_col[c - 1:c, :]

    def col_pair(m, p):
        return jnp.where(lane_lo, m[:, 2 * p:2 * p + 1], m[:, 2 * p + 1:2 * p + 2])

    g = [_dot_nt(jnp.concatenate([qbf[p], kbf[p]], axis=0),
                 jnp.concatenate([kbf[p], kbf[p]], axis=0)) for p in pairs]
    decay = [jnp.exp(jnp.where(incl, col_pair(gc_col, p) - gc_row2[p:p + 1, :], NEG)) for p in pairs]
    a_mat = [jnp.where(strict, col_pair(beta_col, p) * g[p][c:] * decay[p], 0.0) for p in pairs]
    attn = [g[p][:c] * decay[p] for p in pairs]
    e = _tri_inverse_minus_eye(a_mat, row, col, lane_lo)

    def rhs_of(p, h):
        bc = beta_col[:, h:h + 1]
        return jnp.concatenate([v[h] * bc, k[p] * (bc * eg_col[:, h:h + 1])], axis=1)

    rhs = [jnp.concatenate([rhs_of(p, 2 * p), rhs_of(p, 2 * p + 1)], axis=0).astype(BF16) for p in pairs]
    t_mat = [eye2 + e[p] for p in pairs]
    uw = [_dot(jnp.where(sel, t_mat[h // 2], 0.0).astype(BF16), rhs[h // 2])
          for h in heads for sel in ((lane_lo, lane_hi)[h % 2],)]
    wq = [jnp.concatenate([uw[h][:, HEAD_DIM:], q[h // 2] * eg_col[:, h:h + 1]], axis=0).astype(BF16)
          for h in heads]
    kd = [k[h // 2] * jnp.exp(gc_col[c - 1:c, h:h + 1] - gc_col[:, h:h + 1]) for h in heads]
    ak = [jnp.concatenate([attn[p], jnp.concatenate([kd[2 * p], kd[2 * p + 1]], axis=0).T], axis=0).astype(BF16)
          for p in pairs]

    lane_lo_b = lax.broadcasted_iota(I32, (c + HEAD_DIM, LANES), 1) < c
    zero = jnp.zeros((), BF16)
    state = [s_ref[h] for h in heads]
    r1 = [_dot(wq[h], state[h].astype(BF16)) for h in heads]
    vn = [uw[h][:, :HEAD_DIM] - r1[h][:c] for h in heads]
    r2 = []
    for p in pairs:
        vst = jnp.concatenate([vn[2 * p], vn[2 * p + 1]], axis=0).astype(BF16)
        r2.append(_dot(jnp.where(lane_lo_b, ak[p], zero), vst))
        r2.append(_dot(jnp.where(lane_lo_b, zero, ak[p]), vst))
    gn = gn_ref[...]
    for h in heads:
        sl = slice(h * HEAD_DIM, (h + 1) * HEAD_DIM)
        o = r1[h][c:] + r2[h][:c]
        s_ref[h] = state[h] * eg_last[:, h:h + 1] + r2[h][c:]
        on = o * lax.rsqrt(jnp.mean(o * o, axis=-1, keepdims=True) + EPS) * gn
        z = z_ref[:, sl].astype(F32)
        o_ref[:, sl] = (on * (z * _sigmoid(z))).astype(o_ref.dtype)


def _gdn(p4, conv_w, ps, pst, a_log, dt_bias, z, out_norm):
    t, ch = p4.shape
    n = t // CHUNK
    hv = B_V_HEADS
    fix2 = lambda i: (0, 0)
    return pl.pallas_call(
        _gdn_kernel,
        grid=(n,),
        in_specs=[
            pl.BlockSpec((CHUNK, ch), lambda i: (i, 0)),
            pl.BlockSpec((8, ch), lambda i: (jnp.maximum(i * (CHUNK // 8) - 1, 0), 0)),
            pl.BlockSpec((CONV_WIDTH, ch), fix2),
            pl.BlockSpec((CHUNK, ps.shape[1]), lambda i: (i, 0)),
            pl.BlockSpec((1, N_PAIRS, LANES), lambda i: (i, 0, 0)),
            pl.BlockSpec((1, hv), fix2), pl.BlockSpec((N_PAIRS, LANES), fix2),
            pl.BlockSpec((1, hv), fix2), pl.BlockSpec((N_PAIRS, LANES), fix2),
            pl.BlockSpec((CHUNK, hv * HEAD_DIM), lambda i: (i, 0)),
            pl.BlockSpec((1, HEAD_DIM), fix2),
        ],
        out_specs=pl.BlockSpec((CHUNK, hv * HEAD_DIM), lambda i: (i, 0)),
        out_shape=jax.ShapeDtypeStruct((t, hv * HEAD_DIM), BF16),
        scratch_shapes=[pltpu.VMEM((hv, HEAD_DIM, HEAD_DIM), F32)],
        compiler_params=_cp("arbitrary"),
    )(p4, p4, conv_w, ps, pst, a_log.reshape(1, hv), _pair_lanes(a_log), dt_bias.reshape(1, hv),
      _pair_lanes(dt_bias), z, out_norm.reshape(1, HEAD_DIM))


def _pair_lanes(per_head):
    return jnp.repeat(per_head.reshape(N_PAIRS, 2), CHUNK, axis=1)


_A_W = A_HEADS * HEAD_DIM
_A_IQ = IDX_HEADS * IDX_DIM
_B_QK = B_QK_HEADS * HEAD_DIM
_B_V = B_V_HEADS * HEAD_DIM
_OFF_AIK = 3 * _A_W + _A_IQ
_OFF_AIW = _OFF_AIK + IDX_DIM
_OFF_BQ = _OFF_AIW + IDX_HEADS
_OFF_BZ = _OFF_BQ + 2 * _B_QK + _B_V
_OFF_BB = _OFF_BZ + _B_V
_OFF_BA = _OFF_BB + B_V_HEADS
_OFF_GATE = _OFF_BA + B_V_HEADS


_PK_A = 0
_PK_B = 3 * _A_W + _A_IQ
_PK_G = _PK_B + 2 * _B_QK + 2 * _B_V
_PK_S = _PK_G + 2 * _A_W
_PK_S_COLS = 4 * LANES


def _pack_in_weights(w_in):
    pad = lambda a: jnp.pad(a, ((0, 0), (0, LANES - a.shape[1])))
    return jnp.concatenate([
        w_in[:, 0:_OFF_AIK], w_in[:, _OFF_BQ:_OFF_BB], w_in[:, _OFF_GATE:_OFF_GATE + 2 * _A_W],
        pad(w_in[:, _OFF_AIK:_OFF_AIK + IDX_DIM]), pad(w_in[:, _OFF_AIW:_OFF_AIW + IDX_HEADS]),
        pad(w_in[:, _OFF_BB:_OFF_BB + B_V_HEADS]), pad(w_in[:, _OFF_BA:_OFF_BA + B_V_HEADS]),
    ], axis=1).astype(BF16)


def _ba_weights_t(w_in):
    w = w_in[:, _OFF_BA:_OFF_BA + B_V_HEADS]
    return jnp.concatenate([w[:, 0::2], w[:, 1::2]], axis=1).T.astype(BF16)


def _rope_freqs(rot, width):
    half = rot // 2
    inv = jnp.power(ROPE_THETA, -jnp.arange(half, dtype=F32) * 2.0 / rot)
    pat = jnp.concatenate([inv, inv, jnp.zeros((width - rot,), F32)])
    return jnp.tile(pat, LANES // width).reshape(1, LANES)


def _dsa_branch(h, w_pk, ps, positions, q_norm, k_norm, idx_k_norm):
    t = h.shape[0]
    tm = 1024
    ikn = jnp.pad(idx_k_norm.reshape(1, IDX_DIM), ((0, 0), (0, LANES - IDX_DIM)))
    cos_a, sin_a, cos_i, sin_i, k_idx, w_idx = _dsa_prep(
        ps, positions.reshape(t, 1), _rope_freqs(HEAD_DIM // 4, HEAD_DIM), _rope_freqs(IDX_DIM // 4, IDX_DIM), ikn)
    scale = HEAD_DIM ** -0.5 * LOG2_E
    gain = jnp.concatenate([jnp.tile(q_norm * scale, A_HEADS), jnp.tile(k_norm, A_HEADS)]).reshape(1, 2 * _A_W)
    (qk,) = _matmul(h, w_pk, [BF16], tm, 1024, _qk_epilogue, [(gain, "row"), (cos_a, "rows"), (sin_a, "rows")],
                    b_cols=(_PK_A, 2 * _A_W))
    (v,) = _matmul(h, w_pk, [BF16], tm, 1024, b_cols=(_PK_A + 2 * _A_W, _A_W))
    (q_idx,) = _matmul(h, w_pk, [BF16], tm, 1024, _qidx_epilogue, [(cos_i, "rows"), (sin_i, "rows")],
                       b_cols=(_PK_A + 3 * _A_W, _A_IQ))
    bias = _indexer(q_idx, w_idx, k_idx, min(TOPK_MAX, t // 4))
    return _attention(qk, v, bias)


def _gdn_branch(h, w_pk, ps, pst, conv_w, a_log, dt_bias, out_norm):
    tm = 1024
    (p4,) = _matmul(h, w_pk, [F32], tm, 1024, b_cols=(_PK_B, 2 * _B_QK + _B_V))
    (z,) = _matmul(h, w_pk, [BF16], tm, 1024, b_cols=(_PK_B + 2 * _B_QK + _B_V, _B_V))
    return _gdn(p4, conv_w, ps, pst, a_log, dt_bias, z, out_norm)


def _gate_a_epilogue(acc, g):
    return (_sigmoid(g.astype(F32)) * acc,)


def _gate_b_epilogue(acc, g, m):
    return (m + _sigmoid(g.astype(F32)) * acc,)


def _residual_norm_epilogue(acc, x, gain):
    x1 = x + acc
    h2 = x1 * lax.rsqrt(jnp.mean(x1 * x1, axis=-1, keepdims=True) + EPS) * gain
    return x1, h2


def _mix(x, h, w_pk, out_a, out_b, w_o_a, w_o_b, w_out, ffn_norm):
    d = x.shape[1]
    (ga,) = _matmul(h, w_pk, [BF16], 1024, 1024, b_cols=(_PK_G, d))
    (gb,) = _matmul(h, w_pk, [BF16], 1024, 1024, b_cols=(_PK_G + d, d))
    (mixa,) = _matmul(out_a, w_o_a.astype(BF16), [F32], 1024, 1024, _gate_a_epilogue, [(ga, "tile")])
    (mix,) = _matmul(out_b, w_o_b.astype(BF16), [BF16], 1024, 512, _gate_b_epilogue, [(gb, "tile"), (mixa, "tile")])
    return _matmul(mix, w_out.astype(BF16), [F32, F32], 512, d, _residual_norm_epilogue,
                   [(x, "tile"), (ffn_norm.reshape(1, d), "row")])


MOE_TM = 256
MOE_NF = 1
_R_E1, _R_E2, _R_RANK1, _R_RANK2, _R_W1, _R_W2 = 0, 1, 2, 3, 4, 5


def _router_kernel(h_ref, w_ref, b_ref, r_ref, cnt_ref, carry_ref):
    i = pl.program_id(0)

    @pl.when(i == 0)
    def _():
        carry_ref[...] = jnp.zeros_like(carry_ref)

    tb = h_ref.shape[0]
    logits = _dot(h_ref[...].astype(BF16), w_ref[...]) + b_ref[...]
    le = logits[:, :N_EXPERTS]
    lg = logits[:, N_EXPERTS:]
    lane = lax.broadcasted_iota(I32, (tb, LANES), 1)
    lanef = lane.astype(F32)
    far = float(4 * LANES)

    lgm = jnp.where(lane < N_GROUPS, lg, NEG)
    gmax = jnp.max(lgm, axis=-1, keepdims=True)
    g_idx = jnp.min(jnp.where(lgm == gmax, lanef, far), axis=-1, keepdims=True)
    g_top = 1.0 / jnp.sum(jnp.exp(lgm - gmax), axis=-1, keepdims=True)

    in_grp = (lane // EXPERTS_PER_GROUP).astype(F32) == g_idx
    lem = jnp.where(in_grp, le, NEG)
    e1 = jnp.max(lem, axis=-1, keepdims=True)
    i1 = jnp.min(jnp.where(lem == e1, lanef, far), axis=-1, keepdims=True)
    lem2 = jnp.where(lanef == i1, NEG, lem)
    e2 = jnp.max(lem2, axis=-1, keepdims=True)
    i2 = jnp.min(jnp.where(lem2 == e2, lanef, far), axis=-1, keepdims=True)
    se = jnp.sum(jnp.exp(lem - e1), axis=-1, keepdims=True)
    p1 = 1.0 / se
    p2 = jnp.exp(e2 - e1) / se
    w1 = g_top * p1 / (p1 + p2)
    w2 = g_top * p2 / (p1 + p2)

    o1 = jnp.where(lanef == i1, 1.0, 0.0)
    o2 = jnp.where(lanef == i2, 1.0, 0.0)
    osum = o1 + o2
    rr = lax.broadcasted_iota(I32, (tb, tb), 0)
    cc = lax.broadcasted_iota(I32, (tb, tb), 1)
    before = jnp.where(cc < rr, 1.0, 0.0).astype(BF16)
    prefix = _dot(before, osum.astype(BF16)) + carry_ref[0:1, :]
    rank1 = jnp.sum(prefix * o1, axis=-1, keepdims=True)
    rank2 = jnp.sum(prefix * o2, axis=-1, keepdims=True)
    carry_ref[...] = carry_ref[...] + jnp.sum(osum, axis=0, keepdims=True)
    cnt_ref[...] = carry_ref[...]

    rec = jnp.zeros((tb, LANES), F32)
    for idx, val in ((_R_E1, i1), (_R_E2, i2), (_R_RANK1, rank1), (_R_RANK2, rank2), (_R_W1, w1), (_R_W2, w2)):
        rec = jnp.where(lane == idx, val, rec)
    r_ref[...] = rec


def _router(h2, w_router, b_router):
    t, d = h2.shape
    tb = min(t, 512)
    return pl.pallas_call(
        _router_kernel,
        grid=(t // tb,),
        in_specs=[pl.BlockSpec((tb, d), lambda i: (i, 0)), pl.BlockSpec((d, 2 * LANES), lambda i: (0, 0)),
                  pl.BlockSpec((1, 2 * LANES), lambda i: (0, 0))],
        out_specs=[pl.BlockSpec((tb, LANES), lambda i: (i, 0)), pl.BlockSpec((8, LANES), lambda i: (0, 0))],
        out_shape=[jax.ShapeDtypeStruct((t, LANES), F32), jax.ShapeDtypeStruct((8, LANES), F32)],
        scratch_shapes=[pltpu.VMEM((8, LANES), F32)],
        compiler_params=_cp("arbitrary"),
    )(h2, w_router, b_router)


def _positions_kernel(r_ref, cnt_ref, pos_ref, te_ref, nv_ref):
    t = r_ref.shape[0]
    nt = te_ref.shape[0]
    cnt = cnt_ref[...]
    tiles = jnp.floor((cnt + (MOE_TM - 1)) * (1.0 / MOE_TM))
    rr = lax.broadcasted_iota(I32, (LANES, LANES), 0)
    cc = lax.broadcasted_iota(I32, (LANES, LANES), 1)
    start_tiles = _dot(tiles.astype(BF16), jnp.where(rr < cc, 1.0, 0.0).astype(BF16))
    start = start_tiles[0:1, :] * float(MOE_TM)
    rec = r_ref[...]
    lane = lax.broadcasted_iota(I32, (t, LANES), 1)
    lanef = lane.astype(F32)
    pos1 = jnp.sum(jnp.where(lanef == rec[:, _R_E1:_R_E1 + 1], start, 0.0), axis=-1, keepdims=True) \
        + rec[:, _R_RANK1:_R_RANK1 + 1]
    pos2 = jnp.sum(jnp.where(lanef == rec[:, _R_E2:_R_E2 + 1], start, 0.0), axis=-1, keepdims=True) \
        + rec[:, _R_RANK2:_R_RANK2 + 1]
    pos_ref[...] = jnp.where(lane == 0, pos1, jnp.where(lane == 1, pos2, 0.0)).astype(I32)
    tile_id = lax.broadcasted_iota(I32, (nt, LANES), 0).astype(F32)
    tlane = lax.broadcasted_iota(I32, (nt, LANES), 1)
    owner = jnp.sum(jnp.where(start_tiles[0:1, :] <= tile_id, 1.0, 0.0), axis=-1, keepdims=True) - 1.0
    is_owner = tlane.astype(F32) == owner
    own_cnt = jnp.sum(jnp.where(is_owner, cnt[0:1, :], 0.0), axis=-1, keepdims=True)
    own_start = jnp.sum(jnp.where(is_owner, start_tiles[0:1, :], 0.0), axis=-1, keepdims=True)
    rows = jnp.clip(own_cnt - (tile_id[:, 0:1] - own_start) * float(MOE_TM), 0.0, float(MOE_TM))
    te_ref[...] = jnp.where(tlane == 0, owner, jnp.where(tlane == 1, rows, 0.0)).astype(I32)
    nv_ref[...] = jnp.broadcast_to(jnp.sum(tiles[0:1, :], axis=-1, keepdims=True), (8, LANES)).astype(I32)


def _positions(rec, cnt, n_tiles):
    t = rec.shape[0]
    full = lambda shape: pl.BlockSpec(shape, lambda i: (0, 0))
    return pl.pallas_call(
        _positions_kernel,
        grid=(1,),
        in_specs=[full((t, LANES)), full((8, LANES))],
        out_specs=[full((t, LANES)), full((n_tiles, LANES)), full((8, LANES))],
        out_shape=[jax.ShapeDtypeStruct((t, LANES), I32), jax.ShapeDtypeStruct((n_tiles, LANES), I32),
                   jax.ShapeDtypeStruct((8, LANES), I32)],
        compiler_params=_cp("arbitrary"),
    )(rec, cnt)


def _invert_kernel(pos_ref, asg_ref):
    def scatter(a, _):
        asg_ref[pos_ref[a]] = a
        return 0

    lax.fori_loop(0, pos_ref.shape[0], scatter, 0, unroll=8)


def _invert(pos, n_slots):
    return pl.pallas_call(
        _invert_kernel,
        in_specs=[pl.BlockSpec(memory_space=pltpu.SMEM)],
        out_specs=pl.BlockSpec(memory_space=pltpu.SMEM),
        out_shape=jax.ShapeDtypeStruct((n_slots,), I32),
    )(pos)


ROW_UNROLL = 8


def _for_each_row(n, body):
    full = lax.shift_right_logical(n, ROW_UNROLL.bit_length() - 1)

    def group(g, _):
        for u in range(ROW_UNROLL):
            body(g * ROW_UNROLL + u)
        return 0

    lax.fori_loop(0, full, group, 0)

    def single(r, _):
        body(r)
        return 0

    lax.fori_loop(full * ROW_UNROLL, n, single, 0)


def _ffn_kernel(te_ref, rows_ref, nv_ref, asg_ref, h_ref, wg_ref, wu_ref, wd_ref, y_ref,
                xbuf_ref, x16_ref, obuf_ref, gsem, ssem):
    i = pl.program_id(0)
    f = pl.program_id(1)
    tm, d = x16_ref.shape
    nv = nv_ref[0]
    last_f = pl.num_programs(1) - 1

    def gather_copy(tile, slot, r):
        token = lax.shift_right_logical(asg_ref[tile * tm + r], 1)
        return pltpu.make_async_copy(h_ref.at[pl.ds(token, 1)], xbuf_ref.at[slot, pl.ds(r, 1)], gsem.at[slot])

    def scatter_copy(tile, r):
        a = asg_ref[tile * tm + r]
        col = pl.multiple_of(jnp.bitwise_and(a, 1) * d, d)
        return pltpu.make_async_copy(obuf_ref.at[pl.ds(r, 1)],
                                     y_ref.at[pl.ds(lax.shift_right_logical(a, 1), 1), pl.ds(col, d)], ssem)

    def gather(tile, slot):
        _for_each_row(rows_ref[tile], lambda r: gather_copy(tile, slot, r).start())

    def scatter_wait(tile):
        _for_each_row(rows_ref[tile], lambda r: scatter_copy(tile, r).wait())

    @pl.when(jnp.logical_and(i == 0, f == 0))
    def _():
        xbuf_ref[...] = jnp.zeros_like(xbuf_ref)
        gather(0, 0)

    @pl.when(jnp.logical_and(f == 0, i + 1 < nv))
    def _():
        gather(i + 1, jnp.bitwise_and(i + 1, 1))

    @pl.when(jnp.logical_and(f == 0, i < nv))
    def _():
        slot = jnp.bitwise_and(i, 1)
        _for_each_row(rows_ref[i], lambda r: gather_copy(i, slot, r).wait())
        x16_ref[...] = xbuf_ref[slot].astype(x16_ref.dtype)

    @pl.when(i < nv)
    def _():
        x = x16_ref[...]
        g = _dot(x, wg_ref[0].astype(BF16))
        u = _dot(x, wu_ref[0].astype(BF16))
        act = (g * _sigmoid(g)) * u
        y = _dot(act.astype(BF16), wd_ref[0].astype(BF16))

        @pl.when(f == 0)
        def _():
            @pl.when(i > 0)
            def _():
                scatter_wait(i - 1)

            obuf_ref[...] = y

        @pl.when(f > 0)
        def _():
            obuf_ref[...] = obuf_ref[...] + y

        @pl.when(f == last_f)
        def _():
            _for_each_row(rows_ref[i], lambda r: scatter_copy(i, r).start())

            @pl.when(i == nv - 1)
            def _():
                scatter_wait(i)


def _ffn(te, rows, nv, asg, h2, w_gate, w_up, w_down, n_tiles):
    t, d = h2.shape
    fb = D_FF // MOE_NF

    def tile(i, nv):
        return jnp.minimum(i, nv[0] - 1)

    def fblk(i, f, nv):
        return jnp.where(i < nv[0], f, MOE_NF - 1)

    any_spec = pl.BlockSpec(memory_space=pl.ANY)
    return pl.pallas_call(
        _ffn_kernel,
        grid_spec=pltpu.PrefetchScalarGridSpec(
            num_scalar_prefetch=4,
            grid=(n_tiles, MOE_NF),
            in_specs=[
                any_spec,
                pl.BlockSpec((1, d, fb), lambda i, f, te, rows, nv, asg: (te[tile(i, nv)], 0, fblk(i, f, nv))),
                pl.BlockSpec((1, d, fb), lambda i, f, te, rows, nv, asg: (te[tile(i, nv)], 0, fblk(i, f, nv))),
                pl.BlockSpec((1, fb, d), lambda i, f, te, rows, nv, asg: (te[tile(i, nv)], fblk(i, f, nv), 0)),
            ],
            out_specs=any_spec,
            scratch_shapes=[pltpu.VMEM((2, MOE_TM, d), F32), pltpu.VMEM((MOE_TM, d), BF16),
                            pltpu.VMEM((MOE_TM, d), F32), pltpu.SemaphoreType.DMA((2,)),
                            pltpu.SemaphoreType.DMA(())],
        ),
        out_shape=jax.ShapeDtypeStruct((t, 2 * d), F32),
        compiler_params=_cp("arbitrary", "arbitrary"),
    )(te, rows, nv, asg, h2, w_gate, w_up, w_down)


def _combine_kernel(y_ref, x_ref, r_ref, o_ref):
    d = x_ref.shape[1]
    rec = r_ref[...]
    o_ref[...] = x_ref[...] + rec[:, _R_W1:_R_W1 + 1] * y_ref[:, :d] + rec[:, _R_W2:_R_W2 + 1] * y_ref[:, d:]


def _combine(y, x1, rec):
    t, d = x1.shape
    tb = min(t, 256)
    return pl.pallas_call(
        _combine_kernel,
        grid=(t // tb,),
        in_specs=[pl.BlockSpec((tb, 2 * d), lambda i: (i, 0)), pl.BlockSpec((tb, d), lambda i: (i, 0)),
                  pl.BlockSpec((tb, LANES), lambda i: (i, 0))],
        out_specs=pl.BlockSpec((tb, d), lambda i: (i, 0)),
        out_shape=jax.ShapeDtypeStruct((t, d), F32),
        compiler_params=_cp("parallel"),
    )(y, x1, rec)


def _moe(x1, h2, w_rg, b_rg, w_re, b_re, w_gate, w_up, w_down):
    t, d = x1.shape
    pad_w = jnp.zeros((d, LANES - N_GROUPS), F32)
    w_router = jnp.concatenate([w_re, w_rg, pad_w], axis=1).astype(BF16)
    b_router = jnp.concatenate([b_re, b_rg, jnp.zeros((LANES - N_GROUPS,), F32)]).reshape(1, 2 * LANES)
    n_tiles = 2 * t // MOE_TM + N_EXPERTS
    rec, cnt = _router(h2, w_router, b_router)
    pos2d, te2d, nv2d = _positions(rec, cnt, n_tiles)
    pos = pos2d[:, :2].reshape(2 * t)
    te = te2d[:, 0]
    rows = te2d[:, 1]
    nv = nv2d[0, :1]
    asg = _invert(pos, n_tiles * MOE_TM)
    y = _ffn(te, rows, nv, asg, h2, w_gate, w_up, w_down, n_tiles)
    return _combine(y, x1, rec)


def kernel(x, positions, attn_norm, w_in, q_norm, k_norm, idx_k_norm, conv_w, a_log, dt_bias, gdn_out_norm,
           w_o_a, w_o_b, w_out, ffn_norm, w_router_group, b_router_group, w_router_expert, b_router_expert,
           w_gate, w_up, w_down):
    b, t, d = x.shape
    x2 = x.reshape(t, d)
    layer = 0
    h = _rmsnorm(x2, attn_norm[layer])
    w_pk = _pack_in_weights(w_in[layer])
    (ps,) = _matmul(h, w_pk, [F32], 1024, _PK_S_COLS, b_cols=(_PK_S, _PK_S_COLS))
    pst = _matmul_nt(_ba_weights_t(w_in[layer]), h, 1024)
    out_a = _dsa_branch(h, w_pk, ps, positions, q_norm[layer], k_norm[layer], idx_k_norm[layer])
    out_b = _gdn_branch(h, w_pk, ps, pst, conv_w[layer], a_log[layer], dt_bias[layer], gdn_out_norm[layer])
    x1, h2 = _mix(x2, h, w_pk, out_a, out_b, w_o_a[layer], w_o_b[layer], w_out[layer], ffn_norm[layer])
    out = _moe(x1, h2, w_router_group[layer], b_router_group[layer], w_router_expert[layer], b_router_expert[layer],
               w_gate[layer], w_up[layer], w_down[layer])
    return out.reshape(b, t, d)
```

```python
import functools

import jax
import jax.numpy as jnp
from jax import lax
from jax.experimental import pallas as pl
from jax.experimental.pallas import tpu as pltpu

F32 = jnp.float32
BF16 = jnp.bfloat16
I32 = jnp.int32

EPS = 1e-6
NEG = -1e30
BIG = 1e30
LOG2_E = 1.4426950408889634
ROPE_THETA = 500000.0
CHUNK = 64
A_HEADS = 16
HEAD_DIM = 128
IDX_HEADS = 16
IDX_DIM = 64
TOPK_MAX = 256
Q_BLOCK = 128
IDX_ROWS = 128
BISECT_STEPS_PER_TRIP = 2
B_QK_HEADS = 16
B_V_HEADS = 32
CONV_WIDTH = 4
N_GROUPS = 8
EXPERTS_PER_GROUP = 16
N_EXPERTS = 128
D_FF = 768
LANES = 128
VMEM_LIMIT = 56 * 1024 * 1024

MM_TM, MM_TN = 1024, 1024
MM_TN_DEEP = 512
ROW_TILE = 512
ATTN_BQ, ATTN_BK = 512, 1024
IDX_KC = 512

NT_DIMS = (((1,), (1,)), ((), ()))


def _cp(*sem):
    return pltpu.CompilerParams(dimension_semantics=sem, vmem_limit_bytes=VMEM_LIMIT)


def _dot(a, b):
    return jnp.dot(a, b, preferred_element_type=F32)


def _dot_nt(a, b):
    return lax.dot_general(a, b, NT_DIMS, preferred_element_type=F32)


def _dot_hi(a, b):
    return jnp.dot(a, b, preferred_element_type=F32, precision=lax.Precision.HIGHEST)


def _sigmoid(x):
    return 0.5 + 0.5 * jnp.tanh(0.5 * x)


def _silu(x):
    half = 0.5 * x
    return half + half * jnp.tanh(half)


def _softplus(x):
    return jnp.maximum(x, 0.0) + jnp.log(1.0 + jnp.exp(-jnp.abs(x)))


def _rmsnorm_kernel(x_ref, g_ref, o_ref):
    x = x_ref[...]
    ms = jnp.mean(x * x, axis=-1, keepdims=True)
    o_ref[...] = (x * lax.rsqrt(ms + EPS) * g_ref[...]).astype(o_ref.dtype)


def _rmsnorm(x, gain, out_dtype=BF16):
    t, d = x.shape
    tm = min(t, ROW_TILE)
    return pl.pallas_call(
        _rmsnorm_kernel,
        grid=(t // tm,),
        in_specs=[pl.BlockSpec((tm, d), lambda i: (i, 0)), pl.BlockSpec((1, d), lambda i: (0, 0))],
        out_specs=pl.BlockSpec((tm, d), lambda i: (i, 0)),
        out_shape=jax.ShapeDtypeStruct((t, d), out_dtype),
        compiler_params=_cp("parallel"),
    )(x, gain.reshape(1, d))


def _mm_kernel(*refs, n_extra, epilogue):
    a_ref, b_ref = refs[:2]
    extra = refs[2:2 + n_extra]
    outs = refs[2 + n_extra:]
    acc = _dot(a_ref[...], b_ref[...])
    res = epilogue(acc, *[e[...] for e in extra]) if epilogue is not None else (acc,)
    for o_ref, r in zip(outs, res):
        o_ref[...] = r.astype(o_ref.dtype)


def _matmul(a, b, out_dtypes, tm, tn, epilogue=None, extras=(), b_cols=None):
    m, k = a.shape
    start, n = b_cols if b_cols is not None else (0, b.shape[1])
    tm, tn = min(tm, m), min(tn, n)
    assert start % tn == 0 and n % tn == 0 and m % tm == 0
    j0 = start // tn
    in_specs = [pl.BlockSpec((tm, k), lambda i, j: (i, 0)), pl.BlockSpec((k, tn), lambda i, j: (0, j + j0))]
    args = [a, b]
    for arr, kind in extras:
        if kind == "tile":
            in_specs.append(pl.BlockSpec((tm, tn), lambda i, j: (i, j)))
        elif kind == "rows":
            in_specs.append(pl.BlockSpec((tm, LANES), lambda i, j: (i, 0)))
        else:
            in_specs.append(pl.BlockSpec((1, tn), lambda i, j: (0, j)))
        args.append(arr)
    out = pl.pallas_call(
        functools.partial(_mm_kernel, n_extra=len(extras), epilogue=epilogue),
        grid=(m // tm, n // tn),
        in_specs=in_specs,
        out_specs=[pl.BlockSpec((tm, tn), lambda i, j: (i, j)) for _ in out_dtypes],
        out_shape=[jax.ShapeDtypeStruct((m, n), dt) for dt in out_dtypes],
        compiler_params=_cp("parallel", "arbitrary"),
    )(*args)
    return out


def _mm_nt_kernel(a_ref, b_ref, o_ref):
    res = _dot_nt(a_ref[...], b_ref[...])
    half = res.shape[0] // 2
    for c in range(o_ref.shape[0]):
        cols = slice(c * CHUNK, (c + 1) * CHUNK)
        o_ref[c] = jnp.concatenate([res[:half, cols], res[half:, cols]], axis=1)


def _matmul_nt(a, b, tm):
    n, k = a.shape
    m = b.shape[0]
    tm = min(tm, m)
    return pl.pallas_call(
        _mm_nt_kernel,
        grid=(m // tm,),
        in_specs=[pl.BlockSpec((n, k), lambda i: (0, 0)), pl.BlockSpec((tm, k), lambda i: (i, 0))],
        out_specs=pl.BlockSpec((tm // CHUNK, n // 2, 2 * CHUNK), lambda i: (i, 0, 0)),
        out_shape=jax.ShapeDtypeStruct((m // CHUNK, n // 2, 2 * CHUNK), F32),
        compiler_params=_cp("parallel"),
    )(a, b)


A_ROT_HALF = HEAD_DIM // 8
I_ROT_HALF = IDX_DIM // 8


def _rope_a(x, cos, sin):
    lane = lax.broadcasted_iota(I32, x.shape, 1)
    partner = jnp.where(lane < A_ROT_HALF, pltpu.roll(x, LANES - A_ROT_HALF, 1), pltpu.roll(x, A_ROT_HALF, 1))
    return x * cos + partner * sin


def _rope_i(x, cos, sin):
    lane = lax.broadcasted_iota(I32, x.shape, 1)
    low = (lane & (IDX_DIM - 1)) < I_ROT_HALF
    partner = jnp.where(low, pltpu.roll(x, LANES - I_ROT_HALF, 1), pltpu.roll(x, I_ROT_HALF, 1))
    return x * cos + partner * sin


def _dsa_prep_kernel(ps_ref, pos_ref, ifa_ref, ifi_ref, ikn_ref,
                     cosa_ref, sina_ref, cosi_ref, sini_ref, ki_ref, wi_ref):
    tm = ps_ref.shape[0]
    pos = pos_ref[...].astype(F32)
    lane = lax.broadcasted_iota(I32, (tm, LANES), 1)
    ang_a = pos * ifa_ref[...]
    sin_a = jnp.sin(ang_a)
    cosa_ref[...] = jnp.cos(ang_a)
    sina_ref[...] = jnp.where(lane < A_ROT_HALF, -sin_a, sin_a)
    ang_i = pos * ifi_ref[...]
    cos_i = jnp.cos(ang_i)
    sin_i = jnp.sin(ang_i)
    sin_i = jnp.where((lane & (IDX_DIM - 1)) < I_ROT_HALF, -sin_i, sin_i)
    cosi_ref[...] = cos_i
    sini_ref[...] = sin_i
    xk = ps_ref[:, 0:LANES]
    ms = jnp.sum(xk * xk, axis=-1, keepdims=True) * (1.0 / IDX_DIM)
    yk = xk * lax.rsqrt(ms + EPS) * ikn_ref[...]
    ki_ref[...] = _rope_i(yk, cos_i, sin_i).T[:IDX_DIM, :].astype(ki_ref.dtype)
    wi_ref[...] = ps_ref[:, LANES:LANES + IDX_HEADS] * (IDX_HEADS ** -0.5 * IDX_DIM ** -0.5)


def _dsa_prep(ps, positions, ifa, ifi, ikn):
    t = ps.shape[0]
    tm = min(t, ROW_TILE)
    row = lambda i: (i, 0)
    fix = lambda i: (0, 0)
    table = jax.ShapeDtypeStruct((t, LANES), F32)
    return pl.pallas_call(
        _dsa_prep_kernel,
        grid=(t // tm,),
        in_specs=[pl.BlockSpec((tm, ps.shape[1]), row), pl.BlockSpec((tm, 1), row), pl.BlockSpec((1, LANES), fix),
                  pl.BlockSpec((1, LANES), fix), pl.BlockSpec((1, LANES), fix)],
        out_specs=[pl.BlockSpec((tm, LANES), row)] * 4
        + [pl.BlockSpec((IDX_DIM, tm), lambda i: (0, i)), pl.BlockSpec((tm, IDX_HEADS), row)],
        out_shape=[table] * 4 + [jax.ShapeDtypeStruct((IDX_DIM, t), BF16), jax.ShapeDtypeStruct((t, IDX_HEADS), F32)],
        compiler_params=_cp("parallel"),
    )(ps, positions, ifa, ifi, ikn)


def _qk_epilogue(acc, gain, cos, sin):
    heads = range(acc.shape[1] // HEAD_DIM)
    xs = [acc[:, h * HEAD_DIM:(h + 1) * HEAD_DIM] for h in heads]
    ms = [jnp.mean(x * x, axis=-1, keepdims=True) for x in xs]
    rs = [lax.rsqrt(m + EPS) for m in ms]
    ys = [xs[h] * rs[h] * gain[:, h * HEAD_DIM:(h + 1) * HEAD_DIM] for h in heads]
    return (jnp.concatenate([_rope_a(y, cos, sin) for y in ys], axis=1),)


def _qidx_epilogue(acc, cos, sin):
    return (jnp.concatenate([_rope_i(acc[:, j * LANES:(j + 1) * LANES], cos, sin)
                             for j in range(acc.shape[1] // LANES)], axis=1),)


def _indexer_kernel(qi_ref, w_ref, ki_ref, bias_ref, s_ref, wb_ref, qs_ref, *, kc, topk, maxit):
    i = pl.program_id(0)
    qb = qi_ref.shape[0]
    hb = Q_BLOCK
    halves = qb // hb
    n_ch = ((i + 1) * qb + kc - 1) // kc
    nslab = kc // LANES
    sub_kc = min(kc, 2 * LANES)
    for h in range(IDX_HEADS):
        wb_ref[h] = jnp.broadcast_to(w_ref[:, h:h + 1], (qb, LANES))
        for rh in range(halves):
            qs_ref[(rh * IDX_HEADS + h) * hb:(rh * IDX_HEADS + h + 1) * hb, :] = \
                qi_ref[rh * hb:(rh + 1) * hb, h * IDX_DIM:(h + 1) * IDX_DIM]
    row = lax.broadcasted_iota(I32, (qb, LANES), 0)
    lane = lax.broadcasted_iota(I32, (qb, LANES), 1)
    limit = ((i * qb + row) // CHUNK + 1) * CHUNK

    def score_body(c, carry):
        mn, mx = carry
        off = pl.multiple_of(c * kc, kc)
        accs = [[] for _ in range(halves)]
        for sub in range(kc // sub_kc):
            kblk = ki_ref[:, pl.ds(pl.multiple_of(off + sub * sub_kc, sub_kc), sub_kc)]
            for rh in range(halves):
                part = [jnp.zeros((hb, LANES), F32) for _ in range(sub_kc // LANES)]
                s_all = _dot(qs_ref[rh * IDX_HEADS * hb:(rh + 1) * IDX_HEADS * hb, :], kblk)
                for h in range(IDX_HEADS):
                    wbh = wb_ref[h, rh * hb:(rh + 1) * hb, :]
                    for j in range(sub_kc // LANES):
                        s = s_all[h * hb:(h + 1) * hb, j * LANES:(j + 1) * LANES]
                        part[j] = part[j] + wbh * jnp.maximum(s, 0.0)
                accs[rh] += part
        for j in range(nslab):
            acc = jnp.concatenate([accs[rh][j] for rh in range(halves)], axis=0) if halves > 1 else accs[0][j]
            adm = (lane + (off + j * LANES)) < limit
            val = jnp.where(adm, acc, NEG)
            s_ref[:, pl.ds(pl.multiple_of(off + j * LANES, LANES), LANES)] = val
            mn = jnp.minimum(mn, jnp.where(adm, acc, BIG))
            mx = jnp.maximum(mx, val)
        return mn, mx

    mn, mx = lax.fori_loop(0, n_ch, score_body,
                           (jnp.full((qb, LANES), BIG, F32), jnp.full((qb, LANES), NEG, F32)))
    lo0 = jnp.min(mn, axis=-1, keepdims=True)
    hi0 = jnp.max(mx, axis=-1, keepdims=True)
    kf = float(topk)

    def count_ge(thr):
        thr_b = jnp.broadcast_to(thr, (qb, LANES))

        def body(c, acc):
            off = pl.multiple_of(c * kc, kc)
            for j in range(nslab):
                sc = s_ref[:, pl.ds(pl.multiple_of(off + j * LANES, LANES), LANES)]
                acc = acc + jnp.where(sc >= thr_b, 1.0, 0.0)
            return acc

        acc = lax.fori_loop(0, n_ch, body, jnp.zeros((qb, LANES), F32))
        return jnp.sum(acc, axis=-1, keepdims=True)

    done0 = jnp.where(limit[:, 0:1] <= topk, 1.0, 0.0)

    def cond(st):
        it, _, _, done = st
        return jnp.logical_and(it < maxit, jnp.min(done) < 0.5)

    def body(st):
        it, lo, hi, done = st
        for _ in range(BISECT_STEPS_PER_TRIP):
            mid = lo + (hi - lo) * 0.5
            stuck = jnp.logical_or(mid <= lo, mid >= hi)
            c = count_ge(mid)
            ge = c >= kf
            lo = jnp.where(ge, mid, lo)
            hi = jnp.where(ge, hi, mid)
            done = jnp.maximum(done, jnp.where(jnp.logical_or(c == kf, stuck), 1.0, 0.0))
        return it + BISECT_STEPS_PER_TRIP, lo, hi, done

    _, lo, hi, _ = lax.while_loop(cond, body, (jnp.int32(0), lo0, hi0, done0))
    c_hi = count_ge(hi)
    lo = jnp.where(c_hi >= kf, hi, lo)
    lo_b = jnp.broadcast_to(lo, (qb, LANES))

    bias_ref[...] = jnp.full(bias_ref.shape, NEG, bias_ref.dtype)

    def write_body(c, _):
        off = pl.multiple_of(c * kc, kc)
        for j in range(nslab):
            o = pl.multiple_of(off + j * LANES, LANES)
            sc = s_ref[:, pl.ds(o, LANES)]
            bias_ref[:, pl.ds(o, LANES)] = jnp.where(sc >= lo_b, 0.0, NEG).astype(bias_ref.dtype)
        return 0

    lax.fori_loop(0, n_ch, write_body, 0)


def _indexer(q_idx, w_idx, k_idx, topk):
    t = q_idx.shape[0]
    kc = min(t, IDX_KC)
    qb = IDX_ROWS
    return pl.pallas_call(
        functools.partial(_indexer_kernel, kc=kc, topk=topk, maxit=64),
        grid=(t // qb,),
        in_specs=[
            pl.BlockSpec((qb, IDX_HEADS * IDX_DIM), lambda i: (i, 0)),
            pl.BlockSpec((qb, IDX_HEADS), lambda i: (i, 0)),
            pl.BlockSpec((IDX_DIM, t), lambda i: (0, 0)),
        ],
        out_specs=pl.BlockSpec((qb, t), lambda i: (i, 0)),
        out_shape=jax.ShapeDtypeStruct((t, t), BF16),
        scratch_shapes=[pltpu.VMEM((qb, t), F32), pltpu.VMEM((IDX_HEADS, qb, LANES), F32),
                        pltpu.VMEM((IDX_HEADS * qb, IDX_DIM), BF16)],
        compiler_params=_cp("parallel"),
    )(q_idx, w_idx, k_idx)


def _last_key_block(qi, bq, bk):
    return ((qi + 1) * bq - 1) // bk


def _attn_kernel(q_ref, k_ref, v_ref, b_ref, o_ref, acc_ref, m_ref, l_ref):
    qi = pl.program_id(0)
    ki = pl.program_id(1)
    last = _last_key_block(qi, q_ref.shape[0], k_ref.shape[0])

    @pl.when(ki == 0)
    def _():
        acc_ref[...] = jnp.zeros_like(acc_ref)
        m_ref[...] = jnp.full(m_ref.shape, NEG, F32)
        l_ref[...] = jnp.zeros_like(l_ref)

    @pl.when(ki <= last)
    def _():
        bias = b_ref[...]
        ones = jnp.ones((k_ref.shape[0], HEAD_DIM), BF16)
        for h in range(A_HEADS):
            sl = slice(h * HEAD_DIM, (h + 1) * HEAD_DIM)
            s = _dot_nt(q_ref[:, sl], k_ref[:, sl]).astype(BF16) + bias
            m_prev = m_ref[h]
            m_new = jnp.maximum(m_prev, jnp.max(s, axis=-1, keepdims=True).astype(F32))
            alpha = jnp.exp2(m_prev - m_new)
            p = jnp.exp2(s - m_new[:, 0:1].astype(BF16))
            pv = _dot(p, jnp.concatenate([v_ref[:, sl], ones], axis=1))
            l_ref[h] = alpha * l_ref[h] + pv[:, HEAD_DIM:]
            acc_ref[:, sl] = alpha * acc_ref[:, sl] + pv[:, :HEAD_DIM]
            m_ref[h] = m_new

    @pl.when(ki == last)
    def _():
        for h in range(A_HEADS):
            sl = slice(h * HEAD_DIM, (h + 1) * HEAD_DIM)
            o_ref[:, sl] = (acc_ref[:, sl] / l_ref[h]).astype(o_ref.dtype)


def _attention(qk, v, bias):
    t, aw = v.shape
    bq = min(t, ATTN_BQ)
    bk = min(t, ATTN_BK)
    kv_map = lambda i, j: (jnp.minimum(j, _last_key_block(i, bq, bk)), 0)
    return pl.pallas_call(
        _attn_kernel,
        grid=(t // bq, t // bk),
        in_specs=[
            pl.BlockSpec((bq, aw), lambda i, j: (i, 0)),
            pl.BlockSpec((bk, aw), lambda i, j: (jnp.minimum(j, _last_key_block(i, bq, bk)), 1)),
            pl.BlockSpec((bk, aw), kv_map),
            pl.BlockSpec((bq, bk), lambda i, j: (i, jnp.minimum(j, _last_key_block(i, bq, bk)))),
        ],
        out_specs=pl.BlockSpec((bq, aw), lambda i, j: (i, 0)),
        out_shape=jax.ShapeDtypeStruct((t, aw), BF16),
        scratch_shapes=[pltpu.VMEM((bq, aw), F32), pltpu.VMEM((A_HEADS, bq, LANES), F32),
                        pltpu.VMEM((A_HEADS, bq, LANES), F32)],
        compiler_params=_cp("parallel", "arbitrary"),
    )(qk, qk, v, bias)


N_PAIRS = B_V_HEADS // 2


def _tri_inverse_minus_eye(a, row, col, lane_lo):
    n = len(a)
    lane_hi = jnp.logical_not(lane_lo)

    def mm(x, y):
        bd = jnp.concatenate([jnp.where(lane_lo, y, 0.0), jnp.where(lane_hi, y, 0.0)], axis=0)
        return _dot(x.astype(BF16), bd.astype(BF16))

    blk16 = (row // 16) == (col // 16)
    blk32 = (row // 32) == (col // 32)
    d1 = [jnp.where(blk16, a[p], 0.0) for p in range(n)]
    e = [-d1[p] for p in range(n)]
    dk = d1
    for _ in range(3):
        dk = [mm(dk[p], dk[p]) for p in range(n)]
        de = [mm(dk[p], e[p]) for p in range(n)]
        e = [e[p] + dk[p] + de[p] for p in range(n)]
    for sel in (jnp.logical_and(blk32, jnp.logical_not(blk16)), jnp.logical_not(blk32)):
        lo = [jnp.where(sel, a[p], 0.0) for p in range(n)]
        m1 = [lo[p] + mm(e[p], lo[p]) for p in range(n)]
        m2 = [m1[p] + mm(m1[p], e[p]) for p in range(n)]
        e = [e[p] - m2[p] for p in range(n)]
    return e


def _gdn_kernel(cur_ref, prev_ref, cw_ref, ps_ref, pst_ref, alr_ref, alc_ref, dtr_ref, dtc_ref, z_ref, gn_ref,
                o_ref, s_ref):
    c = CHUNK
    pairs = range(N_PAIRS)
    heads = range(B_V_HEADS)
    step = pl.program_id(0)

    @pl.when(step == 0)
    def _():
        s_ref[...] = jnp.zeros_like(s_ref)

    keep_prev = jnp.where(step > 0, 1.0, 0.0)
    halo = CONV_WIDTH - 1

    def conv_silu(col0):
        sl = slice(col0, col0 + HEAD_DIM)
        cur = cur_ref[:, sl]
        xcat = jnp.concatenate([prev_ref[:, sl] * keep_prev, cur], axis=0)
        y = cur * cw_ref[halo:halo + 1, sl]
        for tap in range(halo):
            y = y + xcat[8 - halo + tap:8 - halo + tap + c, :] * cw_ref[tap:tap + 1, sl]
        return _silu(y)

    def l2n(y, scale):
        return y * (lax.rsqrt(jnp.sum(y * y, axis=-1, keepdims=True) + EPS) * scale)

    q = [l2n(conv_silu(p * HEAD_DIM), HEAD_DIM ** -0.5) for p in pairs]
    k = [l2n(conv_silu((B_QK_HEADS + p) * HEAD_DIM), 1.0) for p in pairs]
    v = [conv_silu((2 * B_QK_HEADS + h) * HEAD_DIM) for h in heads]
    qbf = [q[p].astype(BF16) for p in pairs]
    kbf = [k[p].astype(BF16) for p in pairs]

    row = lax.broadcasted_iota(I32, (c, LANES), 0)
    lane = lax.broadcasted_iota(I32, (c, LANES), 1)
    col = lane & (c - 1)
    lane_lo = lane < c
    lane_hi = jnp.logical_not(lane_lo)
    incl = row >= col
    strict = row > col
    eye2 = jnp.where(row == col, 1.0, 0.0)
    r64 = lax.broadcasted_iota(I32, (c, c), 0)
    c64 = lax.broadcasted_iota(I32, (c, c), 1)
    lower = jnp.where(r64 >= c64, 1.0, 0.0)
    r128 = lax.broadcasted_iota(I32, (LANES, LANES), 0)
    c128 = lax.broadcasted_iota(I32, (LANES, LANES), 1)
    upper2 = jnp.where(jnp.logical_and(r128 // c == c128 // c, r128 <= c128), 1.0, 0.0)

    beta_col = _sigmoid(ps_ref[:, 2 * LANES:2 * LANES + B_V_HEADS])
    g_col = -jnp.exp(alr_ref[...]) * _softplus(ps_ref[:, 3 * LANES:3 * LANES + B_V_HEADS] + dtr_ref[...])
    g_row2 = -jnp.exp(alc_ref[...]) * _softplus(pst_ref[0] + dtc_ref[...])
    gc_col = _dot_hi(lower, g_col)
    gc_row2 = _dot_hi(g_row2, upper2)
    eg_col = jnp.exp(gc_col)
    eg_last = eg_col[c - 1:c, :]

    def col_pair(m, p):
        return jnp.where(lane_lo, m[:, 2 * p:2 * p + 1], m[:, 2 * p + 1:2 * p + 2])

    g = [_dot_nt(jnp.concatenate([qbf[p], kbf[p]], axis=0),
                 jnp.concatenate([kbf[p], kbf[p]], axis=0)) for p in pairs]
    decay = [jnp.exp(jnp.where(incl, col_pair(gc_col, p) - gc_row2[p:p + 1, :], NEG)) for p in pairs]
    a_mat = [jnp.where(strict, col_pair(beta_col, p) * g[p][c:] * decay[p], 0.0) for p in pairs]
    attn = [g[p][:c] * decay[p] for p in pairs]
    e = _tri_inverse_minus_eye(a_mat, row, col, lane_lo)

    def rhs_of(p, h):
        bc = beta_col[:, h:h + 1]
        return jnp.concatenate([v[h] * bc, k[p] * (bc * eg_col[:, h:h + 1])], axis=1)

    rhs = [jnp.concatenate([rhs_of(p, 2 * p), rhs_of(p, 2 * p + 1)], axis=0).astype(BF16) for p in pairs]
    t_mat = [eye2 + e[p] for p in pairs]
    uw = [_dot(jnp.where(sel, t_mat[h // 2], 0.0).astype(BF16), rhs[h // 2])
          for h in heads for sel in ((lane_lo, lane_hi)[h % 2],)]
    wq = [jnp.concatenate([uw[h][:, HEAD_DIM:], q[h // 2] * eg_col[:, h:h + 1]], axis=0).astype(BF16)
          for h in heads]
    kd = [k[h // 2] * jnp.exp(gc_col[c - 1:c, h:h + 1] - gc_col[:, h:h + 1]) for h in heads]
    ak = [jnp.concatenate([attn[p], jnp.concatenate([kd[2 * p], kd[2 * p + 1]], axis=0).T], axis=0).astype(BF16)
          for p in pairs]

    lane_lo_b = lax.broadcasted_iota(I32, (c + HEAD_DIM, LANES), 1) < c
    zero = jnp.zeros((), BF16)
    state = [s_ref[h] for h in heads]
    r1 = [_dot(wq[h], state[h].astype(BF16)) for h in heads]
    vn = [uw[h][:, :HEAD_DIM] - r1[h][:c] for h in heads]
    r2 = []
    for p in pairs:
        vst = jnp.concatenate([vn[2 * p], vn[2 * p + 1]], axis=0).astype(BF16)
        r2.append(_dot(jnp.where(lane_lo_b, ak[p], zero), vst))
        r2.append(_dot(jnp.where(lane_lo_b, zero, ak[p]), vst))
    gn = gn_ref[...]
    for h in heads:
        sl = slice(h * HEAD_DIM, (h + 1) * HEAD_DIM)
        o = r1[h][c:] + r2[h][:c]
        s_ref[h] = state[h] * eg_last[:, h:h + 1] + r2[h][c:]
        on = o * lax.rsqrt(jnp.mean(o * o, axis=-1, keepdims=True) + EPS) * gn
        z = z_ref[:, sl].astype(F32)
        o_ref[:, sl] = (on * _silu(z)).astype(o_ref.dtype)


def _gdn(p4, conv_w, ps, pst, a_log, dt_bias, z, out_norm):
    t, ch = p4.shape
    n = t // CHUNK
    hv = B_V_HEADS
    fix2 = lambda i: (0, 0)
    return pl.pallas_call(
        _gdn_kernel,
        grid=(n,),
        in_specs=[
            pl.BlockSpec((CHUNK, ch), lambda i: (i, 0)),
            pl.BlockSpec((8, ch), lambda i: (jnp.maximum(i * (CHUNK // 8) - 1, 0), 0)),
            pl.BlockSpec((CONV_WIDTH, ch), fix2),
            pl.BlockSpec((CHUNK, ps.shape[1]), lambda i: (i, 0)),
            pl.BlockSpec((1, N_PAIRS, LANES), lambda i: (i, 0, 0)),
            pl.BlockSpec((1, hv), fix2), pl.BlockSpec((N_PAIRS, LANES), fix2),
            pl.BlockSpec((1, hv), fix2), pl.BlockSpec((N_PAIRS, LANES), fix2),
            pl.BlockSpec((CHUNK, hv * HEAD_DIM), lambda i: (i, 0)),
            pl.BlockSpec((1, HEAD_DIM), fix2),
        ],
        out_specs=pl.BlockSpec((CHUNK, hv * HEAD_DIM), lambda i: (i, 0)),
        out_shape=jax.ShapeDtypeStruct((t, hv * HEAD_DIM), BF16),
        scratch_shapes=[pltpu.VMEM((hv, HEAD_DIM, HEAD_DIM), F32)],
        compiler_params=_cp("arbitrary"),
    )(p4, p4, conv_w, ps, pst, a_log.reshape(1, hv), _pair_lanes(a_log), dt_bias.reshape(1, hv),
      _pair_lanes(dt_bias), z, out_norm.reshape(1, HEAD_DIM))


def _pair_lanes(per_head):
    return jnp.repeat(per_head.reshape(N_PAIRS, 2), CHUNK, axis=1)


_A_W = A_HEADS * HEAD_DIM
_A_IQ = IDX_HEADS * IDX_DIM
_B_QK = B_QK_HEADS * HEAD_DIM
_B_V = B_V_HEADS * HEAD_DIM
_OFF_AIK = 3 * _A_W + _A_IQ
_OFF_AIW = _OFF_AIK + IDX_DIM
_OFF_BQ = _OFF_AIW + IDX_HEADS
_OFF_BZ = _OFF_BQ + 2 * _B_QK + _B_V
_OFF_BB = _OFF_BZ + _B_V
_OFF_BA = _OFF_BB + B_V_HEADS
_OFF_GATE = _OFF_BA + B_V_HEADS


_PK_A = 0
_PK_B = 3 * _A_W + _A_IQ
_PK_G = _PK_B + 2 * _B_QK + 2 * _B_V
_PK_S = _PK_G + 2 * _A_W
_PK_S_COLS = 4 * LANES


def _pack_in_weights(w_in):
    pad = lambda a: jnp.pad(a, ((0, 0), (0, LANES - a.shape[1])))
    return jnp.concatenate([
        w_in[:, 0:_OFF_AIK], w_in[:, _OFF_BQ:_OFF_BB], w_in[:, _OFF_GATE:_OFF_GATE + 2 * _A_W],
        pad(w_in[:, _OFF_AIK:_OFF_AIK + IDX_DIM]), pad(w_in[:, _OFF_AIW:_OFF_AIW + IDX_HEADS]),
        pad(w_in[:, _OFF_BB:_OFF_BB + B_V_HEADS]), pad(w_in[:, _OFF_BA:_OFF_BA + B_V_HEADS]),
    ], axis=1).astype(BF16)


def _ba_weights_t(w_in):
    w = w_in[:, _OFF_BA:_OFF_BA + B_V_HEADS]
    return jnp.concatenate([w[:, 0::2], w[:, 1::2]], axis=1).T.astype(BF16)


def _rope_freqs(rot, width):
    half = rot // 2
    inv = jnp.power(ROPE_THETA, -jnp.arange(half, dtype=F32) * 2.0 / rot)
    pat = jnp.concatenate([inv, inv, jnp.zeros((width - rot,), F32)])
    return jnp.tile(pat, LANES // width).reshape(1, LANES)


def _dsa_branch(h, w_pk, ps, positions, q_norm, k_norm, idx_k_norm):
    t = h.shape[0]
    tm = MM_TM
    ikn = jnp.pad(idx_k_norm.reshape(1, IDX_DIM), ((0, 0), (0, LANES - IDX_DIM)))
    cos_a, sin_a, cos_i, sin_i, k_idx, w_idx = _dsa_prep(
        ps, positions.reshape(t, 1), _rope_freqs(HEAD_DIM // 4, HEAD_DIM), _rope_freqs(IDX_DIM // 4, IDX_DIM), ikn)
    scale = HEAD_DIM ** -0.5 * LOG2_E
    gain = jnp.concatenate([jnp.tile(q_norm * scale, A_HEADS), jnp.tile(k_norm, A_HEADS)]).reshape(1, 2 * _A_W)
    (qk,) = _matmul(h, w_pk, [BF16], tm, MM_TN, _qk_epilogue, [(gain, "row"), (cos_a, "rows"), (sin_a, "rows")],
                    b_cols=(_PK_A, 2 * _A_W))
    (v,) = _matmul(h, w_pk, [BF16], tm, MM_TN, b_cols=(_PK_A + 2 * _A_W, _A_W))
    (q_idx,) = _matmul(h, w_pk, [BF16], tm, MM_TN, _qidx_epilogue, [(cos_i, "rows"), (sin_i, "rows")],
                       b_cols=(_PK_A + 3 * _A_W, _A_IQ))
    bias = _indexer(q_idx, w_idx, k_idx, min(TOPK_MAX, t // 4))
    return _attention(qk, v, bias)


def _gdn_branch(h, w_pk, ps, pst, conv_w, a_log, dt_bias, out_norm):
    tm = MM_TM
    (p4,) = _matmul(h, w_pk, [F32], tm, MM_TN, b_cols=(_PK_B, 2 * _B_QK + _B_V))
    (z,) = _matmul(h, w_pk, [BF16], tm, MM_TN, b_cols=(_PK_B + 2 * _B_QK + _B_V, _B_V))
    return _gdn(p4, conv_w, ps, pst, a_log, dt_bias, z, out_norm)


def _gate_a_epilogue(acc, g):
    return (_sigmoid(g.astype(F32)) * acc,)


def _gate_b_epilogue(acc, g, m):
    return (m + _sigmoid(g.astype(F32)) * acc,)


def _residual_norm_epilogue(acc, x, gain):
    x1 = x + acc
    h2 = x1 * lax.rsqrt(jnp.mean(x1 * x1, axis=-1, keepdims=True) + EPS) * gain
    return x1, h2


def _mix(x, h, w_pk, out_a, out_b, w_o_a, w_o_b, w_out, ffn_norm):
    d = x.shape[1]
    (ga,) = _matmul(h, w_pk, [BF16], MM_TM, MM_TN, b_cols=(_PK_G, d))
    (gb,) = _matmul(h, w_pk, [BF16], MM_TM, MM_TN, b_cols=(_PK_G + d, d))
    (mixa,) = _matmul(out_a, w_o_a.astype(BF16), [F32], MM_TM, MM_TN, _gate_a_epilogue, [(ga, "tile")])
    (mix,) = _matmul(out_b, w_o_b.astype(BF16), [BF16], MM_TM, MM_TN_DEEP, _gate_b_epilogue,
                     [(gb, "tile"), (mixa, "tile")])
    return _matmul(mix, w_out.astype(BF16), [F32, F32], ROW_TILE, d, _residual_norm_epilogue,
                   [(x, "tile"), (ffn_norm.reshape(1, d), "row")])


MOE_TM = 256
MOE_NF = 1
_R_E1, _R_E2, _R_RANK1, _R_RANK2, _R_W1, _R_W2 = 0, 1, 2, 3, 4, 5


def _router_kernel(h_ref, w_ref, b_ref, r_ref, cnt_ref, carry_ref):
    i = pl.program_id(0)

    @pl.when(i == 0)
    def _():
        carry_ref[...] = jnp.zeros_like(carry_ref)

    tb = h_ref.shape[0]
    logits = _dot(h_ref[...].astype(BF16), w_ref[...]) + b_ref[...]
    le = logits[:, :N_EXPERTS]
    lg = logits[:, N_EXPERTS:]
    lane = lax.broadcasted_iota(I32, (tb, LANES), 1)
    lanef = lane.astype(F32)
    far = float(4 * LANES)

    lgm = jnp.where(lane < N_GROUPS, lg, NEG)
    gmax = jnp.max(lgm, axis=-1, keepdims=True)
    g_idx = jnp.min(jnp.where(lgm == gmax, lanef, far), axis=-1, keepdims=True)
    g_top = 1.0 / jnp.sum(jnp.exp(lgm - gmax), axis=-1, keepdims=True)

    in_grp = (lane // EXPERTS_PER_GROUP).astype(F32) == g_idx
    lem = jnp.where(in_grp, le, NEG)
    e1 = jnp.max(lem, axis=-1, keepdims=True)
    i1 = jnp.min(jnp.where(lem == e1, lanef, far), axis=-1, keepdims=True)
    lem2 = jnp.where(lanef == i1, NEG, lem)
    e2 = jnp.max(lem2, axis=-1, keepdims=True)
    i2 = jnp.min(jnp.where(lem2 == e2, lanef, far), axis=-1, keepdims=True)
    se = jnp.sum(jnp.exp(lem - e1), axis=-1, keepdims=True)
    p1 = 1.0 / se
    p2 = jnp.exp(e2 - e1) / se
    w1 = g_top * p1 / (p1 + p2)
    w2 = g_top * p2 / (p1 + p2)

    o1 = jnp.where(lanef == i1, 1.0, 0.0)
    o2 = jnp.where(lanef == i2, 1.0, 0.0)
    osum = o1 + o2
    rr = lax.broadcasted_iota(I32, (tb, tb), 0)
    cc = lax.broadcasted_iota(I32, (tb, tb), 1)
    before = jnp.where(cc < rr, 1.0, 0.0).astype(BF16)
    prefix = _dot(before, osum.astype(BF16)) + carry_ref[0:1, :]
    rank1 = jnp.sum(prefix * o1, axis=-1, keepdims=True)
    rank2 = jnp.sum(prefix * o2, axis=-1, keepdims=True)
    carry_ref[...] = carry_ref[...] + jnp.sum(osum, axis=0, keepdims=True)
    cnt_ref[...] = carry_ref[...]

    rec = jnp.zeros((tb, LANES), F32)
    for idx, val in ((_R_E1, i1), (_R_E2, i2), (_R_RANK1, rank1), (_R_RANK2, rank2), (_R_W1, w1), (_R_W2, w2)):
        rec = jnp.where(lane == idx, val, rec)
    r_ref[...] = rec


def _router(h2, w_router, b_router):
    t, d = h2.shape
    tb = min(t, ROW_TILE)
    return pl.pallas_call(
        _router_kernel,
        grid=(t // tb,),
        in_specs=[pl.BlockSpec((tb, d), lambda i: (i, 0)), pl.BlockSpec((d, 2 * LANES), lambda i: (0, 0)),
                  pl.BlockSpec((1, 2 * LANES), lambda i: (0, 0))],
        out_specs=[pl.BlockSpec((tb, LANES), lambda i: (i, 0)), pl.BlockSpec((8, LANES), lambda i: (0, 0))],
        out_shape=[jax.ShapeDtypeStruct((t, LANES), F32), jax.ShapeDtypeStruct((8, LANES), F32)],
        scratch_shapes=[pltpu.VMEM((8, LANES), F32)],
        compiler_params=_cp("arbitrary"),
    )(h2, w_router, b_router)


def _positions_kernel(r_ref, cnt_ref, pos_ref, te_ref, nv_ref):
    t = r_ref.shape[0]
    nt = te_ref.shape[0]
    cnt = cnt_ref[...]
    tiles = jnp.floor((cnt + (MOE_TM - 1)) * (1.0 / MOE_TM))
    rr = lax.broadcasted_iota(I32, (LANES, LANES), 0)
    cc = lax.broadcasted_iota(I32, (LANES, LANES), 1)
    start_tiles = _dot(tiles.astype(BF16), jnp.where(rr < cc, 1.0, 0.0).astype(BF16))
    start = start_tiles[0:1, :] * float(MOE_TM)
    rec = r_ref[...]
    lane = lax.broadcasted_iota(I32, (t, LANES), 1)
    lanef = lane.astype(F32)
    pos1 = jnp.sum(jnp.where(lanef == rec[:, _R_E1:_R_E1 + 1], start, 0.0), axis=-1, keepdims=True) \
        + rec[:, _R_RANK1:_R_RANK1 + 1]
    pos2 = jnp.sum(jnp.where(lanef == rec[:, _R_E2:_R_E2 + 1], start, 0.0), axis=-1, keepdims=True) \
        + rec[:, _R_RANK2:_R_RANK2 + 1]
    pos_ref[...] = jnp.where(lane == 0, pos1, jnp.where(lane == 1, pos2, 0.0)).astype(I32)
    tile_id = lax.broadcasted_iota(I32, (nt, LANES), 0).astype(F32)
    tlane = lax.broadcasted_iota(I32, (nt, LANES), 1)
    owner = jnp.sum(jnp.where(start_tiles[0:1, :] <= tile_id, 1.0, 0.0), axis=-1, keepdims=True) - 1.0
    is_owner = tlane.astype(F32) == owner
    own_cnt = jnp.sum(jnp.where(is_owner, cnt[0:1, :], 0.0), axis=-1, keepdims=True)
    own_start = jnp.sum(jnp.where(is_owner, start_tiles[0:1, :], 0.0), axis=-1, keepdims=True)
    rows = jnp.clip(own_cnt - (tile_id[:, 0:1] - own_start) * float(MOE_TM), 0.0, float(MOE_TM))
    te_ref[...] = jnp.where(tlane == 0, owner, jnp.where(tlane == 1, rows, 0.0)).astype(I32)
    nv_ref[...] = jnp.broadcast_to(jnp.sum(tiles[0:1, :], axis=-1, keepdims=True), (8, LANES)).astype(I32)


def _positions(rec, cnt, n_tiles):
    t = rec.shape[0]
    full = lambda shape: pl.BlockSpec(shape, lambda i: (0, 0))
    return pl.pallas_call(
        _positions_kernel,
        grid=(1,),
        in_specs=[full((t, LANES)), full((8, LANES))],
        out_specs=[full((t, LANES)), full((n_tiles, LANES)), full((8, LANES))],
        out_shape=[jax.ShapeDtypeStruct((t, LANES), I32), jax.ShapeDtypeStruct((n_tiles, LANES), I32),
                   jax.ShapeDtypeStruct((8, LANES), I32)],
        compiler_params=_cp("arbitrary"),
    )(rec, cnt)


def _invert_kernel(pos_ref, asg_ref):
    def scatter(a, _):
        asg_ref[pos_ref[a]] = a
        return 0

    lax.fori_loop(0, pos_ref.shape[0], scatter, 0, unroll=8)


def _invert(pos, n_slots):
    return pl.pallas_call(
        _invert_kernel,
        in_specs=[pl.BlockSpec(memory_space=pltpu.SMEM)],
        out_specs=pl.BlockSpec(memory_space=pltpu.SMEM),
        out_shape=jax.ShapeDtypeStruct((n_slots,), I32),
    )(pos)


ROW_UNROLL = 8


def _for_each_row(n, body):
    full = lax.shift_right_logical(n, ROW_UNROLL.bit_length() - 1)

    def group(g, _):
        for u in range(ROW_UNROLL):
            body(g * ROW_UNROLL + u)
        return 0

    lax.fori_loop(0, full, group, 0)

    def single(r, _):
        body(r)
        return 0

    lax.fori_loop(full * ROW_UNROLL, n, single, 0)


def _ffn_kernel(te_ref, rows_ref, nv_ref, asg_ref, h_ref, wg_ref, wu_ref, wd_ref, y_ref,
                xbuf_ref, x16_ref, obuf_ref, gsem, ssem):
    i = pl.program_id(0)
    f = pl.program_id(1)
    tm, d = x16_ref.shape
    nv = nv_ref[0]
    last_f = pl.num_programs(1) - 1

    def gather_copy(tile, slot, r):
        token = lax.shift_right_logical(asg_ref[tile * tm + r], 1)
        return pltpu.make_async_copy(h_ref.at[pl.ds(token, 1)], xbuf_ref.at[slot, pl.ds(r, 1)], gsem.at[slot])

    def scatter_copy(tile, r):
        a = asg_ref[tile * tm + r]
        col = pl.multiple_of(jnp.bitwise_and(a, 1) * d, d)
        return pltpu.make_async_copy(obuf_ref.at[pl.ds(r, 1)],
                                     y_ref.at[pl.ds(lax.shift_right_logical(a, 1), 1), pl.ds(col, d)], ssem)

    def gather(tile, slot):
        _for_each_row(rows_ref[tile], lambda r: gather_copy(tile, slot, r).start())

    def scatter_wait(tile):
        _for_each_row(rows_ref[tile], lambda r: scatter_copy(tile, r).wait())

    @pl.when(jnp.logical_and(i == 0, f == 0))
    def _():
        xbuf_ref[...] = jnp.zeros_like(xbuf_ref)
        gather(0, 0)

    @pl.when(jnp.logical_and(f == 0, i + 1 < nv))
    def _():
        gather(i + 1, jnp.bitwise_and(i + 1, 1))

    @pl.when(jnp.logical_and(f == 0, i < nv))
    def _():
        slot = jnp.bitwise_and(i, 1)
        _for_each_row(rows_ref[i], lambda r: gather_copy(i, slot, r).wait())
        x16_ref[...] = xbuf_ref[slot].astype(x16_ref.dtype)

    @pl.when(i < nv)
    def _():
        x = x16_ref[...]
        g = _dot(x, wg_ref[0].astype(BF16))
        u = _dot(x, wu_ref[0].astype(BF16))
        act = _silu(g) * u
        y = _dot(act.astype(BF16), wd_ref[0].astype(BF16))

        @pl.when(f == 0)
        def _():
            @pl.when(i > 0)
            def _():
                scatter_wait(i - 1)

            obuf_ref[...] = y

        @pl.when(f > 0)
        def _():
            obuf_ref[...] = obuf_ref[...] + y

        @pl.when(f == last_f)
        def _():
            _for_each_row(rows_ref[i], lambda r: scatter_copy(i, r).start())

            @pl.when(i == nv - 1)
            def _():
                scatter_wait(i)


def _ffn(te, rows, nv, asg, h2, w_gate, w_up, w_down, n_tiles):
    t, d = h2.shape
    fb = D_FF // MOE_NF

    def tile(i, nv):
        return jnp.minimum(i, nv[0] - 1)

    def fblk(i, f, nv):
        return jnp.where(i < nv[0], f, MOE_NF - 1)

    any_spec = pl.BlockSpec(memory_space=pl.ANY)
    return pl.pallas_call(
        _ffn_kernel,
        grid_spec=pltpu.PrefetchScalarGridSpec(
            num_scalar_prefetch=4,
            grid=(n_tiles, MOE_NF),
            in_specs=[
                any_spec,
                pl.BlockSpec((1, d, fb), lambda i, f, te, rows, nv, asg: (te[tile(i, nv)], 0, fblk(i, f, nv))),
                pl.BlockSpec((1, d, fb), lambda i, f, te, rows, nv, asg: (te[tile(i, nv)], 0, fblk(i, f, nv))),
                pl.BlockSpec((1, fb, d), lambda i, f, te, rows, nv, asg: (te[tile(i, nv)], fblk(i, f, nv), 0)),
            ],
            out_specs=any_spec,
            scratch_shapes=[pltpu.VMEM((2, MOE_TM, d), F32), pltpu.VMEM((MOE_TM, d), BF16),
                            pltpu.VMEM((MOE_TM, d), F32), pltpu.SemaphoreType.DMA((2,)),
                            pltpu.SemaphoreType.DMA(())],
        ),
        out_shape=jax.ShapeDtypeStruct((t, 2 * d), F32),
        compiler_params=_cp("arbitrary", "arbitrary"),
    )(te, rows, nv, asg, h2, w_gate, w_up, w_down)


def _combine_kernel(y_ref, x_ref, r_ref, o_ref):
    d = x_ref.shape[1]
    rec = r_ref[...]
    o_ref[...] = x_ref[...] + rec[:, _R_W1:_R_W1 + 1] * y_ref[:, :d] + rec[:, _R_W2:_R_W2 + 1] * y_ref[:, d:]


def _combine(y, x1, rec):
    t, d = x1.shape
    tb = min(t, 256)
    return pl.pallas_call(
        _combine_kernel,
        grid=(t // tb,),
        in_specs=[pl.BlockSpec((tb, 2 * d), lambda i: (i, 0)), pl.BlockSpec((tb, d), lambda i: (i, 0)),
                  pl.BlockSpec((tb, LANES), lambda i: (i, 0))],
        out_specs=pl.BlockSpec((tb, d), lambda i: (i, 0)),
        out_shape=jax.ShapeDtypeStruct((t, d), F32),
        compiler_params=_cp("parallel"),
    )(y, x1, rec)


def _moe(x1, h2, w_rg, b_rg, w_re, b_re, w_gate, w_up, w_down):
    t, d = x1.shape
    pad_w = jnp.zeros((d, LANES - N_GROUPS), F32)
    w_router = jnp.concatenate([w_re, w_rg, pad_w], axis=1).astype(BF16)
    b_router = jnp.concatenate([b_re, b_rg, jnp.zeros((LANES - N_GROUPS,), F32)]).reshape(1, 2 * LANES)
    n_tiles = 2 * t // MOE_TM + N_EXPERTS
    rec, cnt = _router(h2, w_router, b_router)
    pos2d, te2d, nv2d = _positions(rec, cnt, n_tiles)
    pos = pos2d[:, :2].reshape(2 * t)
    te = te2d[:, 0]
    rows = te2d[:, 1]
    nv = nv2d[0, :1]
    asg = _invert(pos, n_tiles * MOE_TM)
    y = _ffn(te, rows, nv, asg, h2, w_gate, w_up, w_down, n_tiles)
    return _combine(y, x1, rec)


def kernel(x, positions, attn_norm, w_in, q_norm, k_norm, idx_k_norm, conv_w, a_log, dt_bias, gdn_out_norm,
           w_o_a, w_o_b, w_out, ffn_norm, w_router_group, b_router_group, w_router_expert, b_router_expert,
           w_gate, w_up, w_down):
    b, t, d = x.shape
    x2 = x.reshape(t, d)
    layer = 0
    h = _rmsnorm(x2, attn_norm[layer])
    w_pk = _pack_in_weights(w_in[layer])
    (ps,) = _matmul(h, w_pk, [F32], MM_TM, _PK_S_COLS, b_cols=(_PK_S, _PK_S_COLS))
    pst = _matmul_nt(_ba_weights_t(w_in[layer]), h, MM_TM)
    out_a = _dsa_branch(h, w_pk, ps, positions, q_norm[layer], k_norm[layer], idx_k_norm[layer])
    out_b = _gdn_branch(h, w_pk, ps, pst, conv_w[layer], a_log[layer], dt_bias[layer], gdn_out_norm[layer])
    x1, h2 = _mix(x2, h, w_pk, out_a, out_b, w_o_a[layer], w_o_b[layer], w_out[layer], ffn_norm[layer])
    out = _moe(x1, h2, w_router_group[layer], b_router_group[layer], w_router_expert[layer], b_router_expert[layer],
               w_gate[layer], w_up[layer], w_down[layer])
    return out.reshape(b, t, d)
```

```python
import functools

import jax
import jax.numpy as jnp
from jax import lax
from jax.experimental import pallas as pl
from jax.experimental.pallas import tpu as pltpu

F32 = jnp.float32
BF16 = jnp.bfloat16
I32 = jnp.int32

EPS = 1e-6
NEG = -1e30
BIG = 1e30
LOG2_E = 1.4426950408889634
ROPE_THETA = 500000.0
CHUNK = 64
A_HEADS = 16
HEAD_DIM = 128
IDX_HEADS = 16
IDX_DIM = 64
TOPK_MAX = 256
Q_BLOCK = 128
IDX_ROWS = 128
BISECT_STEPS_PER_TRIP = 2
B_QK_HEADS = 16
B_V_HEADS = 32
CONV_WIDTH = 4
N_GROUPS = 8
EXPERTS_PER_GROUP = 16
N_EXPERTS = 128
D_FF = 768
LANES = 128
VMEM_LIMIT = 56 * 1024 * 1024

MM_TM, MM_TN = 1024, 1024
MM_TN_DEEP = 512
ROW_TILE = 512
ATTN_BQ, ATTN_BK = 512, 1024
IDX_KC = 512

NT_DIMS = (((1,), (1,)), ((), ()))


def _cp(*sem):
    return pltpu.CompilerParams(dimension_semantics=sem, vmem_limit_bytes=VMEM_LIMIT)


def _dot(a, b):
    return jnp.dot(a, b, preferred_element_type=F32)


def _dot_nt(a, b):
    return lax.dot_general(a, b, NT_DIMS, preferred_element_type=F32)


def _dot_hi(a, b):
    return jnp.dot(a, b, preferred_element_type=F32, precision=lax.Precision.HIGHEST)


def _sigmoid(x):
    return 0.5 + 0.5 * jnp.tanh(0.5 * x)


def _silu(x):
    half = 0.5 * x
    return half + half * jnp.tanh(half)


def _softplus(x):
    return jnp.maximum(x, 0.0) + jnp.log(1.0 + jnp.exp(-jnp.abs(x)))


def _rmsnorm_kernel(x_ref, g_ref, o_ref):
    x = x_ref[...]
    ms = jnp.mean(x * x, axis=-1, keepdims=True)
    o_ref[...] = (x * lax.rsqrt(ms + EPS) * g_ref[...]).astype(o_ref.dtype)


def _rmsnorm(x, gain, out_dtype=BF16):
    t, d = x.shape
    tm = min(t, ROW_TILE)
    return pl.pallas_call(
        _rmsnorm_kernel,
        grid=(t // tm,),
        in_specs=[pl.BlockSpec((tm, d), lambda i: (i, 0)), pl.BlockSpec((1, d), lambda i: (0, 0))],
        out_specs=pl.BlockSpec((tm, d), lambda i: (i, 0)),
        out_shape=jax.ShapeDtypeStruct((t, d), out_dtype),
        compiler_params=_cp("parallel"),
    )(x, gain.reshape(1, d))


def _mm_kernel(*refs, n_extra, epilogue):
    a_ref, b_ref = refs[:2]
    extra = refs[2:2 + n_extra]
    outs = refs[2 + n_extra:]
    acc = _dot(a_ref[...], b_ref[...])
    res = epilogue(acc, *[e[...] for e in extra]) if epilogue is not None else (acc,)
    for o_ref, r in zip(outs, res):
        o_ref[...] = r.astype(o_ref.dtype)


def _matmul(a, b, out_dtypes, tm, tn, epilogue=None, extras=(), b_cols=None):
    m, k = a.shape
    start, n = b_cols if b_cols is not None else (0, b.shape[1])
    tm, tn = min(tm, m), min(tn, n)
    assert start % tn == 0 and n % tn == 0 and m % tm == 0
    j0 = start // tn
    in_specs = [pl.BlockSpec((tm, k), lambda i, j: (i, 0)), pl.BlockSpec((k, tn), lambda i, j: (0, j + j0))]
    args = [a, b]
    for arr, kind in extras:
        if kind == "tile":
            in_specs.append(pl.BlockSpec((tm, tn), lambda i, j: (i, j)))
        elif kind == "rows":
            in_specs.append(pl.BlockSpec((tm, LANES), lambda i, j: (i, 0)))
        else:
            in_specs.append(pl.BlockSpec((1, tn), lambda i, j: (0, j)))
        args.append(arr)
    out = pl.pallas_call(
        functools.partial(_mm_kernel, n_extra=len(extras), epilogue=epilogue),
        grid=(m // tm, n // tn),
        in_specs=in_specs,
        out_specs=[pl.BlockSpec((tm, tn), lambda i, j: (i, j)) for _ in out_dtypes],
        out_shape=[jax.ShapeDtypeStruct((m, n), dt) for dt in out_dtypes],
        compiler_params=_cp("parallel", "arbitrary"),
    )(*args)
    return out


def _mm_nt_kernel(a_ref, b_ref, o_ref):
    res = _dot_nt(a_ref[...], b_ref[...])
    half = res.shape[0] // 2
    for c in range(o_ref.shape[0]):
        cols = slice(c * CHUNK, (c + 1) * CHUNK)
        o_ref[c] = jnp.concatenate([res[:half, cols], res[half:, cols]], axis=1)


def _matmul_nt(a, b, tm):
    n, k = a.shape
    m = b.shape[0]
    tm = min(tm, m)
    return pl.pallas_call(
        _mm_nt_kernel,
        grid=(m // tm,),
        in_specs=[pl.BlockSpec((n, k), lambda i: (0, 0)), pl.BlockSpec((tm, k), lambda i: (i, 0))],
        out_specs=pl.BlockSpec((tm // CHUNK, n // 2, 2 * CHUNK), lambda i: (i, 0, 0)),
        out_shape=jax.ShapeDtypeStruct((m // CHUNK, n // 2, 2 * CHUNK), F32),
        compiler_params=_cp("parallel"),
    )(a, b)


A_ROT_HALF = HEAD_DIM // 8
I_ROT_HALF = IDX_DIM // 8


def _rope_a(x, cos, sin):
    lane = lax.broadcasted_iota(I32, x.shape, 1)
    partner = jnp.where(lane < A_ROT_HALF, pltpu.roll(x, LANES - A_ROT_HALF, 1), pltpu.roll(x, A_ROT_HALF, 1))
    return x * cos + partner * sin


def _rope_i(x, cos, sin):
    lane = lax.broadcasted_iota(I32, x.shape, 1)
    low = (lane & (IDX_DIM - 1)) < I_ROT_HALF
    partner = jnp.where(low, pltpu.roll(x, LANES - I_ROT_HALF, 1), pltpu.roll(x, I_ROT_HALF, 1))
    return x * cos + partner * sin


def _dsa_prep_kernel(ps_ref, pos_ref, ifa_ref, ifi_ref, ikn_ref,
                     cosa_ref, sina_ref, cosi_ref, sini_ref, ki_ref, wi_ref):
    tm = ps_ref.shape[0]
    pos = pos_ref[...].astype(F32)
    lane = lax.broadcasted_iota(I32, (tm, LANES), 1)
    ang_a = pos * ifa_ref[...]
    sin_a = jnp.sin(ang_a)
    cosa_ref[...] = jnp.cos(ang_a)
    sina_ref[...] = jnp.where(lane < A_ROT_HALF, -sin_a, sin_a)
    ang_i = pos * ifi_ref[...]
    cos_i = jnp.cos(ang_i)
    sin_i = jnp.sin(ang_i)
    sin_i = jnp.where((lane & (IDX_DIM - 1)) < I_ROT_HALF, -sin_i, sin_i)
    cosi_ref[...] = cos_i
    sini_ref[...] = sin_i
    xk = ps_ref[:, 0:LANES]
    ms = jnp.sum(xk * xk, axis=-1, keepdims=True) * (1.0 / IDX_DIM)
    yk = xk * lax.rsqrt(ms + EPS) * ikn_ref[...]
    ki_ref[...] = _rope_i(yk, cos_i, sin_i).T[:IDX_DIM, :].astype(ki_ref.dtype)
    wi_ref[...] = ps_ref[:, LANES:LANES + IDX_HEADS] * (IDX_HEADS ** -0.5 * IDX_DIM ** -0.5)


def _dsa_prep(ps, positions, ifa, ifi, ikn):
    t = ps.shape[0]
    tm = min(t, ROW_TILE)
    row = lambda i: (i, 0)
    fix = lambda i: (0, 0)
    table = jax.ShapeDtypeStruct((t, LANES), F32)
    return pl.pallas_call(
        _dsa_prep_kernel,
        grid=(t // tm,),
        in_specs=[pl.BlockSpec((tm, ps.shape[1]), row), pl.BlockSpec((tm, 1), row), pl.BlockSpec((1, LANES), fix),
                  pl.BlockSpec((1, LANES), fix), pl.BlockSpec((1, LANES), fix)],
        out_specs=[pl.BlockSpec((tm, LANES), row)] * 4
        + [pl.BlockSpec((IDX_DIM, tm), lambda i: (0, i)), pl.BlockSpec((tm, IDX_HEADS), row)],
        out_shape=[table] * 4 + [jax.ShapeDtypeStruct((IDX_DIM, t), BF16), jax.ShapeDtypeStruct((t, IDX_HEADS), F32)],
        compiler_params=_cp("parallel"),
    )(ps, positions, ifa, ifi, ikn)


def _qk_epilogue(acc, gain, cos, sin):
    heads = range(acc.shape[1] // HEAD_DIM)
    xs = [acc[:, h * HEAD_DIM:(h + 1) * HEAD_DIM] for h in heads]
    ms = [jnp.mean(x * x, axis=-1, keepdims=True) for x in xs]
    rs = [lax.rsqrt(m + EPS) for m in ms]
    ys = [xs[h] * rs[h] * gain[:, h * HEAD_DIM:(h + 1) * HEAD_DIM] for h in heads]
    return (jnp.concatenate([_rope_a(y, cos, sin) for y in ys], axis=1),)


def _qidx_epilogue(acc, cos, sin):
    return (jnp.concatenate([_rope_i(acc[:, j * LANES:(j + 1) * LANES], cos, sin)
                             for j in range(acc.shape[1] // LANES)], axis=1),)


def _indexer_kernel(qi_ref, w_ref, ki_ref, bias_ref, s_ref, wb_ref, qs_ref, *, kc, topk, maxit):
    i = pl.program_id(0)
    qb = qi_ref.shape[0]
    hb = Q_BLOCK
    halves = qb // hb
    n_ch = ((i + 1) * qb + kc - 1) // kc
    nslab = kc // LANES
    sub_kc = min(kc, 2 * LANES)
    for h in range(IDX_HEADS):
        wb_ref[h] = jnp.broadcast_to(w_ref[:, h:h + 1], (qb, LANES))
        for rh in range(halves):
            qs_ref[(rh * IDX_HEADS + h) * hb:(rh * IDX_HEADS + h + 1) * hb, :] = \
                qi_ref[rh * hb:(rh + 1) * hb, h * IDX_DIM:(h + 1) * IDX_DIM]
    row = lax.broadcasted_iota(I32, (qb, LANES), 0)
    lane = lax.broadcasted_iota(I32, (qb, LANES), 1)
    limit = ((i * qb + row) // CHUNK + 1) * CHUNK

    def score_body(c, carry):
        mn, mx = carry
        off = pl.multiple_of(c * kc, kc)
        accs = [[] for _ in range(halves)]
        for sub in range(kc // sub_kc):
            kblk = ki_ref[:, pl.ds(pl.multiple_of(off + sub * sub_kc, sub_kc), sub_kc)]
            for rh in range(halves):
                part = [jnp.zeros((hb, LANES), F32) for _ in range(sub_kc // LANES)]
                s_all = _dot(qs_ref[rh * IDX_HEADS * hb:(rh + 1) * IDX_HEADS * hb, :], kblk)
                for h in range(IDX_HEADS):
                    wbh = wb_ref[h, rh * hb:(rh + 1) * hb, :]
                    for j in range(sub_kc // LANES):
                        s = s_all[h * hb:(h + 1) * hb, j * LANES:(j + 1) * LANES]
                        part[j] = part[j] + wbh * jnp.maximum(s, 0.0)
                accs[rh] += part
        for j in range(nslab):
            acc = jnp.concatenate([accs[rh][j] for rh in range(halves)], axis=0) if halves > 1 else accs[0][j]
            adm = (lane + (off + j * LANES)) < limit
            val = jnp.where(adm, acc, NEG)
            s_ref[:, pl.ds(pl.multiple_of(off + j * LANES, LANES), LANES)] = val
            mn = jnp.minimum(mn, jnp.where(adm, acc, BIG))
            mx = jnp.maximum(mx, val)
        return mn, mx

    mn, mx = lax.fori_loop(0, n_ch, score_body,
                           (jnp.full((qb, LANES), BIG, F32), jnp.full((qb, LANES), NEG, F32)))
    lo0 = jnp.min(mn, axis=-1, keepdims=True)
    hi0 = jnp.max(mx, axis=-1, keepdims=True)
    kf = float(topk)

    def count_ge(thr):
        thr_b = jnp.broadcast_to(thr, (qb, LANES))

        def body(c, acc):
            off = pl.multiple_of(c * kc, kc)
            for j in range(nslab):
                sc = s_ref[:, pl.ds(pl.multiple_of(off + j * LANES, LANES), LANES)]
                acc = acc + jnp.where(sc >= thr_b, 1.0, 0.0)
            return acc

        acc = lax.fori_loop(0, n_ch, body, jnp.zeros((qb, LANES), F32))
        return jnp.sum(acc, axis=-1, keepdims=True)

    done0 = jnp.where(limit[:, 0:1] <= topk, 1.0, 0.0)

    def cond(st):
        it, _, _, done = st
        return jnp.logical_and(it < maxit, jnp.min(done) < 0.5)

    def body(st):
        it, lo, hi, done = st
        for _ in range(BISECT_STEPS_PER_TRIP):
            mid = lo + (hi - lo) * 0.5
            stuck = jnp.logical_or(mid <= lo, mid >= hi)
            c = count_ge(mid)
            ge = c >= kf
            lo = jnp.where(ge, mid, lo)
            hi = jnp.where(ge, hi, mid)
            done = jnp.maximum(done, jnp.where(jnp.logical_or(c == kf, stuck), 1.0, 0.0))
        return it + BISECT_STEPS_PER_TRIP, lo, hi, done

    _, lo, hi, _ = lax.while_loop(cond, body, (jnp.int32(0), lo0, hi0, done0))
    c_hi = count_ge(hi)
    lo = jnp.where(c_hi >= kf, hi, lo)
    lo_b = jnp.broadcast_to(lo, (qb, LANES))

    bias_ref[...] = jnp.full(bias_ref.shape, NEG, bias_ref.dtype)

    def write_body(c, _):
        off = pl.multiple_of(c * kc, kc)
        for j in range(nslab):
            o = pl.multiple_of(off + j * LANES, LANES)
            sc = s_ref[:, pl.ds(o, LANES)]
            bias_ref[:, pl.ds(o, LANES)] = jnp.where(sc >= lo_b, 0.0, NEG).astype(bias_ref.dtype)
        return 0

    lax.fori_loop(0, n_ch, write_body, 0)


def _indexer(q_idx, w_idx, k_idx, topk):
    t = q_idx.shape[0]
    kc = min(t, IDX_KC)
    qb = IDX_ROWS
    return pl.pallas_call(
        functools.partial(_indexer_kernel, kc=kc, topk=topk, maxit=64),
        grid=(t // qb,),
        in_specs=[
            pl.BlockSpec((qb, IDX_HEADS * IDX_DIM), lambda i: (i, 0)),
            pl.BlockSpec((qb, IDX_HEADS), lambda i: (i, 0)),
            pl.BlockSpec((IDX_DIM, t), lambda i: (0, 0)),
        ],
        out_specs=pl.BlockSpec((qb, t), lambda i: (i, 0)),
        out_shape=jax.ShapeDtypeStruct((t, t), BF16),
        scratch_shapes=[pltpu.VMEM((qb, t), F32), pltpu.VMEM((IDX_HEADS, qb, LANES), F32),
                        pltpu.VMEM((IDX_HEADS * qb, IDX_DIM), BF16)],
        compiler_params=_cp("parallel"),
    )(q_idx, w_idx, k_idx)


def _last_key_block(qi, bq, bk):
    return ((qi + 1) * bq - 1) // bk


def _attn_kernel(q_ref, k_ref, v_ref, b_ref, o_ref, acc_ref, m_ref, l_ref):
    qi = pl.program_id(0)
    ki = pl.program_id(1)
    last = _last_key_block(qi, q_ref.shape[0], k_ref.shape[0])

    @pl.when(ki == 0)
    def _():
        acc_ref[...] = jnp.zeros_like(acc_ref)
        m_ref[...] = jnp.full(m_ref.shape, NEG, F32)
        l_ref[...] = jnp.zeros_like(l_ref)

    @pl.when(ki <= last)
    def _():
        bias = b_ref[...]
        ones = jnp.ones((k_ref.shape[0], HEAD_DIM), BF16)
        for h in range(A_HEADS):
            sl = slice(h * HEAD_DIM, (h + 1) * HEAD_DIM)
            s = _dot_nt(q_ref[:, sl], k_ref[:, sl]).astype(BF16) + bias
            m_prev = m_ref[h]
            m_new = jnp.maximum(m_prev, jnp.max(s, axis=-1, keepdims=True).astype(F32))
            alpha = jnp.exp2(m_prev - m_new)
            p = jnp.exp2(s - m_new[:, 0:1].astype(BF16))
            pv = _dot(p, jnp.concatenate([v_ref[:, sl], ones], axis=1))
            l_ref[h] = alpha * l_ref[h] + pv[:, HEAD_DIM:]
            acc_ref[:, sl] = alpha * acc_ref[:, sl] + pv[:, :HEAD_DIM]
            m_ref[h] = m_new

    @pl.when(ki == last)
    def _():
        for h in range(A_HEADS):
            sl = slice(h * HEAD_DIM, (h + 1) * HEAD_DIM)
            o_ref[:, sl] = (acc_ref[:, sl] / l_ref[h]).astype(o_ref.dtype)


def _attention(qk, v, bias):
    t, aw = v.shape
    bq = min(t, ATTN_BQ)
    bk = min(t, ATTN_BK)
    kv_map = lambda i, j: (jnp.minimum(j, _last_key_block(i, bq, bk)), 0)
    return pl.pallas_call(
        _attn_kernel,
        grid=(t // bq, t // bk),
        in_specs=[
            pl.BlockSpec((bq, aw), lambda i, j: (i, 0)),
            pl.BlockSpec((bk, aw), lambda i, j: (jnp.minimum(j, _last_key_block(i, bq, bk)), 1)),
            pl.BlockSpec((bk, aw), kv_map),
            pl.BlockSpec((bq, bk), lambda i, j: (i, jnp.minimum(j, _last_key_block(i, bq, bk)))),
        ],
        out_specs=pl.BlockSpec((bq, aw), lambda i, j: (i, 0)),
        out_shape=jax.ShapeDtypeStruct((t, aw), BF16),
        scratch_shapes=[pltpu.VMEM((bq, aw), F32), pltpu.VMEM((A_HEADS, bq, LANES), F32),
                        pltpu.VMEM((A_HEADS, bq, LANES), F32)],
        compiler_params=_cp("parallel", "arbitrary"),
    )(qk, qk, v, bias)


N_PAIRS = B_V_HEADS // 2


def _tri_inverse_minus_eye(a, row, col, lane_lo):
    n = len(a)
    lane_hi = jnp.logical_not(lane_lo)

    def mm(x, y):
        bd = jnp.concatenate([jnp.where(lane_lo, y, 0.0), jnp.where(lane_hi, y, 0.0)], axis=0)
        return _dot(x.astype(BF16), bd.astype(BF16))

    blk16 = (row // 16) == (col // 16)
    blk32 = (row // 32) == (col // 32)
    d1 = [jnp.where(blk16, a[p], 0.0) for p in range(n)]
    e = [-d1[p] for p in range(n)]
    dk = d1
    for _ in range(3):
        dk = [mm(dk[p], dk[p]) for p in range(n)]
        de = [mm(dk[p], e[p]) for p in range(n)]
        e = [e[p] + dk[p] + de[p] for p in range(n)]
    for sel in (jnp.logical_and(blk32, jnp.logical_not(blk16)), jnp.logical_not(blk32)):
        lo = [jnp.where(sel, a[p], 0.0) for p in range(n)]
        m1 = [lo[p] + mm(e[p], lo[p]) for p in range(n)]
        m2 = [m1[p] + mm(m1[p], e[p]) for p in range(n)]
        e = [e[p] - m2[p] for p in range(n)]
    return e


def _gdn_kernel(cur_ref, prev_ref, cw_ref, ps_ref, pst_ref, alr_ref, alc_ref, dtr_ref, dtc_ref, z_ref, gn_ref,
                o_ref, s_ref):
    c = CHUNK
    pairs = range(N_PAIRS)
    heads = range(B_V_HEADS)
    step = pl.program_id(0)

    @pl.when(step == 0)
    def _():
        s_ref[...] = jnp.zeros_like(s_ref)

    keep_prev = jnp.where(step > 0, 1.0, 0.0)
    halo = CONV_WIDTH - 1

    def conv_silu(col0):
        sl = slice(col0, col0 + HEAD_DIM)
        cur = cur_ref[:, sl]
        xcat = jnp.concatenate([prev_ref[:, sl] * keep_prev, cur], axis=0)
        y = cur * cw_ref[halo:halo + 1, sl]
        for tap in range(halo):
            y = y + xcat[8 - halo + tap:8 - halo + tap + c, :] * cw_ref[tap:tap + 1, sl]
        return _silu(y)

    def l2n(y, scale):
        return y * (lax.rsqrt(jnp.sum(y * y, axis=-1, keepdims=True) + EPS) * scale)

    q = [l2n(conv_silu(p * HEAD_DIM), HEAD_DIM ** -0.5) for p in pairs]
    k = [l2n(conv_silu((B_QK_HEADS + p) * HEAD_DIM), 1.0) for p in pairs]
    v = [conv_silu((2 * B_QK_HEADS + h) * HEAD_DIM) for h in heads]
    qbf = [q[p].astype(BF16) for p in pairs]
    kbf = [k[p].astype(BF16) for p in pairs]

    row = lax.broadcasted_iota(I32, (c, LANES), 0)
    lane = lax.broadcasted_iota(I32, (c, LANES), 1)
    col = lane & (c - 1)
    lane_lo = lane < c
    lane_hi = jnp.logical_not(lane_lo)
    incl = row >= col
    strict = row > col
    eye2 = jnp.where(row == col, 1.0, 0.0)
    r64 = lax.broadcasted_iota(I32, (c, c), 0)
    c64 = lax.broadcasted_iota(I32, (c, c), 1)
    lower = jnp.where(r64 >= c64, 1.0, 0.0)
    r128 = lax.broadcasted_iota(I32, (LANES, LANES), 0)
    c128 = lax.broadcasted_iota(I32, (LANES, LANES), 1)
    upper2 = jnp.where(jnp.logical_and(r128 // c == c128 // c, r128 <= c128), 1.0, 0.0)

    beta_col = _sigmoid(ps_ref[:, 2 * LANES:2 * LANES + B_V_HEADS])
    g_col = -jnp.exp(alr_ref[...]) * _softplus(ps_ref[:, 3 * LANES:3 * LANES + B_V_HEADS] + dtr_ref[...])
    g_row2 = -jnp.exp(alc_ref[...]) * _softplus(pst_ref[0] + dtc_ref[...])
    gc_col = _dot_hi(lower, g_col)
    gc_row2 = _dot_hi(g_row2, upper2)
    eg_col = jnp.exp(gc_col)
    eg_last = eg_col[c - 1:c, :]

    def col_pair(m, p):
        return jnp.where(lane_lo, m[:, 2 * p:2 * p + 1], m[:, 2 * p + 1:2 * p + 2])

    g = [_dot_nt(jnp.concatenate([qbf[p], kbf[p]], axis=0),
                 jnp.concatenate([kbf[p], kbf[p]], axis=0)) for p in pairs]
    decay = [jnp.exp(jnp.where(incl, col_pair(gc_col, p) - gc_row2[p:p + 1, :], NEG)) for p in pairs]
    a_mat = [jnp.where(strict, col_pair(beta_col, p) * g[p][c:] * decay[p], 0.0) for p in pairs]
    attn = [g[p][:c] * decay[p] for p in pairs]
    e = _tri_inverse_minus_eye(a_mat, row, col, lane_lo)

    def rhs_of(p, h):
        bc = beta_col[:, h:h + 1]
        return jnp.concatenate([v[h] * bc, k[p] * (bc * eg_col[:, h:h + 1])], axis=1)

    rhs = [jnp.concatenate([rhs_of(p, 2 * p), rhs_of(p, 2 * p + 1)], axis=0).astype(BF16) for p in pairs]
    t_mat = [eye2 + e[p] for p in pairs]
    uw = [_dot(jnp.where(sel, t_mat[h // 2], 0.0).astype(BF16), rhs[h // 2])
          for h in heads for sel in ((lane_lo, lane_hi)[h % 2],)]
    wq = [jnp.concatenate([uw[h][:, HEAD_DIM:], q[h // 2] * eg_col[:, h:h + 1]], axis=0).astype(BF16)
          for h in heads]
    kd = [k[h // 2] * jnp.exp(gc_col[c - 1:c, h:h + 1] - gc_col[:, h:h + 1]) for h in heads]
    ak = [jnp.concatenate([attn[p], jnp.concatenate([kd[2 * p], kd[2 * p + 1]], axis=0).T], axis=0).astype(BF16)
          for p in pairs]

    lane_lo_b = lax.broadcasted_iota(I32, (c + HEAD_DIM, LANES), 1) < c
    zero = jnp.zeros((), BF16)
    state = [s_ref[h] for h in heads]
    r1 = [_dot(wq[h], state[h].astype(BF16)) for h in heads]
    vn = [uw[h][:, :HEAD_DIM] - r1[h][:c] for h in heads]
    r2 = []
    for p in pairs:
        vst = jnp.concatenate([vn[2 * p], vn[2 * p + 1]], axis=0).astype(BF16)
        r2.append(_dot(jnp.where(lane_lo_b, ak[p], zero), vst))
        r2.append(_dot(jnp.where(lane_lo_b, zero, ak[p]), vst))
    gn = gn_ref[...]
    for h in heads:
        sl = slice(h * HEAD_DIM, (h + 1) * HEAD_DIM)
        o = r1[h][c:] + r2[h][:c]
        s_ref[h] = state[h] * eg_last[:, h:h + 1] + r2[h][c:]
        on = o * lax.rsqrt(jnp.mean(o * o, axis=-1, keepdims=True) + EPS) * gn
        z = z_ref[:, sl].astype(F32)
        o_ref[:, sl] = (on * _silu(z)).astype(o_ref.dtype)


def _gdn(p4, conv_w, ps, pst, a_log, dt_bias, z, out_norm):
    t, ch = p4.shape
    n = t // CHUNK
    hv = B_V_HEADS
    fix2 = lambda i: (0, 0)
    return pl.pallas_call(
        _gdn_kernel,
        grid=(n,),
        in_specs=[
            pl.BlockSpec((CHUNK, ch), lambda i: (i, 0)),
            pl.BlockSpec((8, ch), lambda i: (jnp.maximum(i * (CHUNK // 8) - 1, 0), 0)),
            pl.BlockSpec((CONV_WIDTH, ch), fix2),
            pl.BlockSpec((CHUNK, ps.shape[1]), lambda i: (i, 0)),
            pl.BlockSpec((1, N_PAIRS, LANES), lambda i: (i, 0, 0)),
            pl.BlockSpec((1, hv), fix2), pl.BlockSpec((N_PAIRS, LANES), fix2),
            pl.BlockSpec((1, hv), fix2), pl.BlockSpec((N_PAIRS, LANES), fix2),
            pl.BlockSpec((CHUNK, hv * HEAD_DIM), lambda i: (i, 0)),
            pl.BlockSpec((1, HEAD_DIM), fix2),
        ],
        out_specs=pl.BlockSpec((CHUNK, hv * HEAD_DIM), lambda i: (i, 0)),
        out_shape=jax.ShapeDtypeStruct((t, hv * HEAD_DIM), BF16),
        scratch_shapes=[pltpu.VMEM((hv, HEAD_DIM, HEAD_DIM), F32)],
        compiler_params=_cp("arbitrary"),
    )(p4, p4, conv_w, ps, pst, a_log.reshape(1, hv), _pair_lanes(a_log), dt_bias.reshape(1, hv),
      _pair_lanes(dt_bias), z, out_norm.reshape(1, HEAD_DIM))


def _pair_lanes(per_head):
    return jnp.repeat(per_head.reshape(N_PAIRS, 2), CHUNK, axis=1)


_A_W = A_HEADS * HEAD_DIM
_A_IQ = IDX_HEADS * IDX_DIM
_B_QK = B_QK_HEADS * HEAD_DIM
_B_V = B_V_HEADS * HEAD_DIM
_OFF_AIK = 3 * _A_W + _A_IQ
_OFF_AIW = _OFF_AIK + IDX_DIM
_OFF_BQ = _OFF_AIW + IDX_HEADS
_OFF_BZ = _OFF_BQ + 2 * _B_QK + _B_V
_OFF_BB = _OFF_BZ + _B_V
_OFF_BA = _OFF_BB + B_V_HEADS
_OFF_GATE = _OFF_BA + B_V_HEADS


_PK_S = 2 * _A_W
_PK_S_COLS = 4 * LANES


def _cast_cols_kernel(src_ref, o_ref, *, shift):
    n = o_ref.shape[1]
    if shift == 0:
        o_ref[...] = src_ref[:, :n].astype(o_ref.dtype)
        return
    lane = lax.broadcasted_iota(I32, (o_ref.shape[0], LANES), 1)
    for j in range(n // LANES):
        a = src_ref[:, j * LANES:(j + 1) * LANES]
        b = src_ref[:, (j + 1) * LANES:(j + 2) * LANES]
        o_ref[:, j * LANES:(j + 1) * LANES] = pltpu.roll(jnp.where(lane >= shift, a, b), LANES - shift, 1) \
            .astype(o_ref.dtype)


def _cast_cols(w, start, n):
    d = w.shape[0]
    shift = start % LANES
    window = n + (LANES if shift else 0)
    tr = 128
    return pl.pallas_call(
        functools.partial(_cast_cols_kernel, shift=shift),
        grid=(d // tr,),
        in_specs=[pl.BlockSpec((pl.Element(tr), pl.Element(window)), lambda i: (i * tr, start - shift))],
        out_specs=pl.BlockSpec((tr, n), lambda i: (i, 0)),
        out_shape=jax.ShapeDtypeStruct((d, n), BF16),
        compiler_params=_cp("parallel"),
    )(w)


def _pack_in_weights(w_in):
    pad = lambda a: jnp.pad(a, ((0, 0), (0, LANES - a.shape[1])))
    n_gate = 2 * _A_W
    side = jnp.concatenate([
        w_in[:, _OFF_GATE + n_gate - LANES:_OFF_GATE + n_gate],
        pad(w_in[:, _OFF_AIK:_OFF_AIK + IDX_DIM]), pad(w_in[:, _OFF_AIW:_OFF_AIW + IDX_HEADS]),
        pad(w_in[:, _OFF_BB:_OFF_BB + B_V_HEADS]), pad(w_in[:, _OFF_BA:_OFF_BA + B_V_HEADS]),
    ], axis=1)
    tail = jnp.concatenate([_cast_cols(w_in, _OFF_GATE, n_gate - LANES), _cast_cols(side, 0, side.shape[1])], axis=1)
    return _cast_cols(w_in, 0, _OFF_AIK), _cast_cols(w_in, _OFF_BQ, _OFF_BB - _OFF_BQ), tail


def _ba_weights_t(w_tail):
    w = w_tail[:, _PK_S + 3 * LANES:_PK_S + 3 * LANES + B_V_HEADS]
    return jnp.concatenate([w[:, 0::2], w[:, 1::2]], axis=1).T


def _rope_freqs(rot, width):
    half = rot // 2
    inv = jnp.power(ROPE_THETA, -jnp.arange(half, dtype=F32) * 2.0 / rot)
    pat = jnp.concatenate([inv, inv, jnp.zeros((width - rot,), F32)])
    return jnp.tile(pat, LANES // width).reshape(1, LANES)


def _dsa_branch(h, w_pk, ps, positions, q_norm, k_norm, idx_k_norm):
    t = h.shape[0]
    tm = MM_TM
    ikn = jnp.pad(idx_k_norm.reshape(1, IDX_DIM), ((0, 0), (0, LANES - IDX_DIM)))
    cos_a, sin_a, cos_i, sin_i, k_idx, w_idx = _dsa_prep(
        ps, positions.reshape(t, 1), _rope_freqs(HEAD_DIM // 4, HEAD_DIM), _rope_freqs(IDX_DIM // 4, IDX_DIM), ikn)
    scale = HEAD_DIM ** -0.5 * LOG2_E
    gain = jnp.concatenate([jnp.tile(q_norm * scale, A_HEADS), jnp.tile(k_norm, A_HEADS)]).reshape(1, 2 * _A_W)
    (qk,) = _matmul(h, w_pk[0], [BF16], tm, MM_TN, _qk_epilogue,
                    [(gain, "row"), (cos_a, "rows"), (sin_a, "rows")], b_cols=(0, 2 * _A_W))
    (v,) = _matmul(h, w_pk[0], [BF16], tm, MM_TN, b_cols=(2 * _A_W, _A_W))
    (q_idx,) = _matmul(h, w_pk[0], [BF16], tm, MM_TN, _qidx_epilogue, [(cos_i, "rows"), (sin_i, "rows")],
                       b_cols=(3 * _A_W, _A_IQ))
    bias = _indexer(q_idx, w_idx, k_idx, min(TOPK_MAX, t // 4))
    return _attention(qk, v, bias)


def _gdn_branch(h, w_pk, ps, pst, conv_w, a_log, dt_bias, out_norm):
    tm = MM_TM
    (p4,) = _matmul(h, w_pk[1], [F32], tm, MM_TN, b_cols=(0, 2 * _B_QK + _B_V))
    (z,) = _matmul(h, w_pk[1], [BF16], tm, MM_TN, b_cols=(2 * _B_QK + _B_V, _B_V))
    return _gdn(p4, conv_w, ps, pst, a_log, dt_bias, z, out_norm)


def _gate_a_epilogue(acc, g):
    return (_sigmoid(g.astype(F32)) * acc,)


def _gate_b_epilogue(acc, g, m):
    return (m + _sigmoid(g.astype(F32)) * acc,)


def _residual_norm_epilogue(acc, x, gain):
    x1 = x + acc
    h2 = x1 * lax.rsqrt(jnp.mean(x1 * x1, axis=-1, keepdims=True) + EPS) * gain
    return x1, h2


def _mix(x, h, w_pk, out_a, out_b, w_o_a, w_o_b, w_out, ffn_norm):
    d = x.shape[1]
    (ga,) = _matmul(h, w_pk[2], [BF16], MM_TM, MM_TN, b_cols=(0, d))
    (gb,) = _matmul(h, w_pk[2], [BF16], MM_TM, MM_TN, b_cols=(d, d))
    (mixa,) = _matmul(out_a, w_o_a.astype(BF16), [F32], MM_TM, MM_TN, _gate_a_epilogue, [(ga, "tile")])
    (mix,) = _matmul(out_b, w_o_b.astype(BF16), [BF16], MM_TM, MM_TN_DEEP, _gate_b_epilogue,
                     [(gb, "tile"), (mixa, "tile")])
    return _matmul(mix, w_out.astype(BF16), [F32, F32], ROW_TILE, d, _residual_norm_epilogue,
                   [(x, "tile"), (ffn_norm.reshape(1, d), "row")])


MOE_TM = 256
MOE_NF = 1
_R_E1, _R_E2, _R_RANK1, _R_RANK2, _R_W1, _R_W2 = 0, 1, 2, 3, 4, 5


def _router_kernel(h_ref, w_ref, b_ref, r_ref, cnt_ref, carry_ref):
    i = pl.program_id(0)

    @pl.when(i == 0)
    def _():
        carry_ref[...] = jnp.zeros_like(carry_ref)

    tb = h_ref.shape[0]
    logits = _dot(h_ref[...].astype(BF16), w_ref[...]) + b_ref[...]
    le = logits[:, :N_EXPERTS]
    lg = logits[:, N_EXPERTS:]
    lane = lax.broadcasted_iota(I32, (tb, LANES), 1)
    lanef = lane.astype(F32)
    far = float(4 * LANES)

    lgm = jnp.where(lane < N_GROUPS, lg, NEG)
    gmax = jnp.max(lgm, axis=-1, keepdims=True)
    g_idx = jnp.min(jnp.where(lgm == gmax, lanef, far), axis=-1, keepdims=True)
    g_top = 1.0 / jnp.sum(jnp.exp(lgm - gmax), axis=-1, keepdims=True)

    in_grp = (lane // EXPERTS_PER_GROUP).astype(F32) == g_idx
    lem = jnp.where(in_grp, le, NEG)
    e1 = jnp.max(lem, axis=-1, keepdims=True)
    i1 = jnp.min(jnp.where(lem == e1, lanef, far), axis=-1, keepdims=True)
    lem2 = jnp.where(lanef == i1, NEG, lem)
    e2 = jnp.max(lem2, axis=-1, keepdims=True)
    i2 = jnp.min(jnp.where(lem2 == e2, lanef, far), axis=-1, keepdims=True)
    se = jnp.sum(jnp.exp(lem - e1), axis=-1, keepdims=True)
    p1 = 1.0 / se
    p2 = jnp.exp(e2 - e1) / se
    w1 = g_top * p1 / (p1 + p2)
    w2 = g_top * p2 / (p1 + p2)

    o1 = jnp.where(lanef == i1, 1.0, 0.0)
    o2 = jnp.where(lanef == i2, 1.0, 0.0)
    osum = o1 + o2
    rr = lax.broadcasted_iota(I32, (tb, tb), 0)
    cc = lax.broadcasted_iota(I32, (tb, tb), 1)
    before = jnp.where(cc < rr, 1.0, 0.0).astype(BF16)
    prefix = _dot(before, osum.astype(BF16)) + carry_ref[0:1, :]
    rank1 = jnp.sum(prefix * o1, axis=-1, keepdims=True)
    rank2 = jnp.sum(prefix * o2, axis=-1, keepdims=True)
    carry_ref[...] = carry_ref[...] + jnp.sum(osum, axis=0, keepdims=True)
    cnt_ref[...] = carry_ref[...]

    rec = jnp.zeros((tb, LANES), F32)
    for idx, val in ((_R_E1, i1), (_R_E2, i2), (_R_RANK1, rank1), (_R_RANK2, rank2), (_R_W1, w1), (_R_W2, w2)):
        rec = jnp.where(lane == idx, val, rec)
    r_ref[...] = rec


def _router(h2, w_router, b_router):
    t, d = h2.shape
    tb = min(t, ROW_TILE)
    return pl.pallas_call(
        _router_kernel,
        grid=(t // tb,),
        in_specs=[pl.BlockSpec((tb, d), lambda i: (i, 0)), pl.BlockSpec((d, 2 * LANES), lambda i: (0, 0)),
                  pl.BlockSpec((1, 2 * LANES), lambda i: (0, 0))],
        out_specs=[pl.BlockSpec((tb, LANES), lambda i: (i, 0)), pl.BlockSpec((8, LANES), lambda i: (0, 0))],
        out_shape=[jax.ShapeDtypeStruct((t, LANES), F32), jax.ShapeDtypeStruct((8, LANES), F32)],
        scratch_shapes=[pltpu.VMEM((8, LANES), F32)],
        compiler_params=_cp("arbitrary"),
    )(h2, w_router, b_router)


def _positions_kernel(r_ref, cnt_ref, pos_ref, te_ref, nv_ref):
    t = r_ref.shape[0]
    nt = te_ref.shape[0]
    cnt = cnt_ref[...]
    tiles = jnp.floor((cnt + (MOE_TM - 1)) * (1.0 / MOE_TM))
    rr = lax.broadcasted_iota(I32, (LANES, LANES), 0)
    cc = lax.broadcasted_iota(I32, (LANES, LANES), 1)
    start_tiles = _dot(tiles.astype(BF16), jnp.where(rr < cc, 1.0, 0.0).astype(BF16))
    start = start_tiles[0:1, :] * float(MOE_TM)
    rec = r_ref[...]
    lane = lax.broadcasted_iota(I32, (t, LANES), 1)
    lanef = lane.astype(F32)
    pos1 = jnp.sum(jnp.where(lanef == rec[:, _R_E1:_R_E1 + 1], start, 0.0), axis=-1, keepdims=True) \
        + rec[:, _R_RANK1:_R_RANK1 + 1]
    pos2 = jnp.sum(jnp.where(lanef == rec[:, _R_E2:_R_E2 + 1], start, 0.0), axis=-1, keepdims=True) \
        + rec[:, _R_RANK2:_R_RANK2 + 1]
    pos_ref[...] = jnp.where(lane == 0, pos1, jnp.where(lane == 1, pos2, 0.0)).astype(I32)
    tile_id = lax.broadcasted_iota(I32, (nt, LANES), 0).astype(F32)
    tlane = lax.broadcasted_iota(I32, (nt, LANES), 1)
    owner = jnp.sum(jnp.where(start_tiles[0:1, :] <= tile_id, 1.0, 0.0), axis=-1, keepdims=True) - 1.0
    is_owner = tlane.astype(F32) == owner
    own_cnt = jnp.sum(jnp.where(is_owner, cnt[0:1, :], 0.0), axis=-1, keepdims=True)
    own_start = jnp.sum(jnp.where(is_owner, start_tiles[0:1, :], 0.0), axis=-1, keepdims=True)
    rows = jnp.clip(own_cnt - (tile_id[:, 0:1] - own_start) * float(MOE_TM), 0.0, float(MOE_TM))
    te_ref[...] = jnp.where(tlane == 0, owner, jnp.where(tlane == 1, rows, 0.0)).astype(I32)
    nv_ref[...] = jnp.broadcast_to(jnp.sum(tiles[0:1, :], axis=-1, keepdims=True), (8, LANES)).astype(I32)


def _positions(rec, cnt, n_tiles):
    t = rec.shape[0]
    full = lambda shape: pl.BlockSpec(shape, lambda i: (0, 0))
    return pl.pallas_call(
        _positions_kernel,
        grid=(1,),
        in_specs=[full((t, LANES)), full((8, LANES))],
        out_specs=[full((t, LANES)), full((n_tiles, LANES)), full((8, LANES))],
        out_shape=[jax.ShapeDtypeStruct((t, LANES), I32), jax.ShapeDtypeStruct((n_tiles, LANES), I32),
                   jax.ShapeDtypeStruct((8, LANES), I32)],
        compiler_params=_cp("arbitrary"),
    )(rec, cnt)


def _invert_kernel(pos_ref, asg_ref):
    def scatter(a, _):
        asg_ref[pos_ref[a]] = a
        return 0

    lax.fori_loop(0, pos_ref.shape[0], scatter, 0, unroll=8)


def _invert(pos, n_slots):
    return pl.pallas_call(
        _invert_kernel,
        in_specs=[pl.BlockSpec(memory_space=pltpu.SMEM)],
        out_specs=pl.BlockSpec(memory_space=pltpu.SMEM),
        out_shape=jax.ShapeDtypeStruct((n_slots,), I32),
    )(pos)


ROW_UNROLL = 8


def _for_each_row(n, body):
    full = lax.shift_right_logical(n, ROW_UNROLL.bit_length() - 1)

    def group(g, _):
        for u in range(ROW_UNROLL):
            body(g * ROW_UNROLL + u)
        return 0

    lax.fori_loop(0, full, group, 0)

    def single(r, _):
        body(r)
        return 0

    lax.fori_loop(full * ROW_UNROLL, n, single, 0)


def _ffn_kernel(te_ref, rows_ref, nv_ref, asg_ref, h_ref, wg_ref, wu_ref, wd_ref, y_ref,
                xbuf_ref, x16_ref, obuf_ref, gsem, ssem):
    i = pl.program_id(0)
    f = pl.program_id(1)
    tm, d = x16_ref.shape
    nv = nv_ref[0]
    last_f = pl.num_programs(1) - 1

    def gather_copy(tile, slot, r):
        token = lax.shift_right_logical(asg_ref[tile * tm + r], 1)
        return pltpu.make_async_copy(h_ref.at[pl.ds(token, 1)], xbuf_ref.at[slot, pl.ds(r, 1)], gsem.at[slot])

    def scatter_copy(tile, r):
        a = asg_ref[tile * tm + r]
        col = pl.multiple_of(jnp.bitwise_and(a, 1) * d, d)
        return pltpu.make_async_copy(obuf_ref.at[pl.ds(r, 1)],
                                     y_ref.at[pl.ds(lax.shift_right_logical(a, 1), 1), pl.ds(col, d)], ssem)

    def gather(tile, slot):
        _for_each_row(rows_ref[tile], lambda r: gather_copy(tile, slot, r).start())

    def scatter_wait(tile):
        _for_each_row(rows_ref[tile], lambda r: scatter_copy(tile, r).wait())

    @pl.when(jnp.logical_and(i == 0, f == 0))
    def _():
        xbuf_ref[...] = jnp.zeros_like(xbuf_ref)
        gather(0, 0)

    @pl.when(jnp.logical_and(f == 0, i + 1 < nv))
    def _():
        gather(i + 1, jnp.bitwise_and(i + 1, 1))

    @pl.when(jnp.logical_and(f == 0, i < nv))
    def _():
        slot = jnp.bitwise_and(i, 1)
        _for_each_row(rows_ref[i], lambda r: gather_copy(i, slot, r).wait())
        x16_ref[...] = xbuf_ref[slot].astype(x16_ref.dtype)

    @pl.when(i < nv)
    def _():
        x = x16_ref[...]
        g = _dot(x, wg_ref[0].astype(BF16))
        u = _dot(x, wu_ref[0].astype(BF16))
        act = _silu(g) * u
        y = _dot(act.astype(BF16), wd_ref[0].astype(BF16))

        @pl.when(f == 0)
        def _():
            @pl.when(i > 0)
            def _():
                scatter_wait(i - 1)

            obuf_ref[...] = y

        @pl.when(f > 0)
        def _():
            obuf_ref[...] = obuf_ref[...] + y

        @pl.when(f == last_f)
        def _():
            _for_each_row(rows_ref[i], lambda r: scatter_copy(i, r).start())

            @pl.when(i == nv - 1)
            def _():
                scatter_wait(i)


def _ffn(te, rows, nv, asg, h2, w_gate, w_up, w_down, n_tiles):
    t, d = h2.shape
    fb = D_FF // MOE_NF

    def tile(i, nv):
        return jnp.minimum(i, nv[0] - 1)

    def fblk(i, f, nv):
        return jnp.where(i < nv[0], f, MOE_NF - 1)

    any_spec = pl.BlockSpec(memory_space=pl.ANY)
    return pl.pallas_call(
        _ffn_kernel,
        grid_spec=pltpu.PrefetchScalarGridSpec(
            num_scalar_prefetch=4,
            grid=(n_tiles, MOE_NF),
            in_specs=[
                any_spec,
                pl.BlockSpec((1, d, fb), lambda i, f, te, rows, nv, asg: (te[tile(i, nv)], 0, fblk(i, f, nv))),
                pl.BlockSpec((1, d, fb), lambda i, f, te, rows, nv, asg: (te[tile(i, nv)], 0, fblk(i, f, nv))),
                pl.BlockSpec((1, fb, d), lambda i, f, te, rows, nv, asg: (te[tile(i, nv)], fblk(i, f, nv), 0)),
            ],
            out_specs=any_spec,
            scratch_shapes=[pltpu.VMEM((2, MOE_TM, d), F32), pltpu.VMEM((MOE_TM, d), BF16),
                            pltpu.VMEM((MOE_TM, d), F32), pltpu.SemaphoreType.DMA((2,)),
                            pltpu.SemaphoreType.DMA(())],
        ),
        out_shape=jax.ShapeDtypeStruct((t, 2 * d), F32),
        compiler_params=_cp("arbitrary", "arbitrary"),
    )(te, rows, nv, asg, h2, w_gate, w_up, w_down)


def _combine_kernel(y_ref, x_ref, r_ref, o_ref):
    d = x_ref.shape[1]
    rec = r_ref[...]
    o_ref[...] = x_ref[...] + rec[:, _R_W1:_R_W1 + 1] * y_ref[:, :d] + rec[:, _R_W2:_R_W2 + 1] * y_ref[:, d:]


def _combine(y, x1, rec):
    t, d = x1.shape
    tb = min(t, 256)
    return pl.pallas_call(
        _combine_kernel,
        grid=(t // tb,),
        in_specs=[pl.BlockSpec((tb, 2 * d), lambda i: (i, 0)), pl.BlockSpec((tb, d), lambda i: (i, 0)),
                  pl.BlockSpec((tb, LANES), lambda i: (i, 0))],
        out_specs=pl.BlockSpec((tb, d), lambda i: (i, 0)),
        out_shape=jax.ShapeDtypeStruct((t, d), F32),
        compiler_params=_cp("parallel"),
    )(y, x1, rec)


def _moe(x1, h2, w_rg, b_rg, w_re, b_re, w_gate, w_up, w_down):
    t, d = x1.shape
    pad_w = jnp.zeros((d, LANES - N_GROUPS), F32)
    w_router = jnp.concatenate([w_re, w_rg, pad_w], axis=1).astype(BF16)
    b_router = jnp.concatenate([b_re, b_rg, jnp.zeros((LANES - N_GROUPS,), F32)]).reshape(1, 2 * LANES)
    n_tiles = 2 * t // MOE_TM + N_EXPERTS
    rec, cnt = _router(h2, w_router, b_router)
    pos2d, te2d, nv2d = _positions(rec, cnt, n_tiles)
    pos = pos2d[:, :2].reshape(2 * t)
    te = te2d[:, 0]
    rows = te2d[:, 1]
    nv = nv2d[0, :1]
    asg = _invert(pos, n_tiles * MOE_TM)
    y = _ffn(te, rows, nv, asg, h2, w_gate, w_up, w_down, n_tiles)
    return _combine(y, x1, rec)


def kernel(x, positions, attn_norm, w_in, q_norm, k_norm, idx_k_norm, conv_w, a_log, dt_bias, gdn_out_norm,
           w_o_a, w_o_b, w_out, ffn_norm, w_router_group, b_router_group, w_router_expert, b_router_expert,
           w_gate, w_up, w_down):
    b, t, d = x.shape
    x2 = x.reshape(t, d)
    layer = 0
    h = _rmsnorm(x2, attn_norm[layer])
    w_pk = _pack_in_weights(w_in[layer])
    (ps,) = _matmul(h, w_pk[2], [F32], MM_TM, _PK_S_COLS, b_cols=(_PK_S, _PK_S_COLS))
    pst = _matmul_nt(_ba_weights_t(w_pk[2]), h, MM_TM)
    out_a = _dsa_branch(h, w_pk, ps, positions, q_norm[layer], k_norm[layer], idx_k_norm[layer])
    out_b = _gdn_branch(h, w_pk, ps, pst, conv_w[layer], a_log[layer], dt_bias[layer], gdn_out_norm[layer])
    x1, h2 = _mix(x2, h, w_pk, out_a, out_b, w_o_a[layer], w_o_b[layer], w_out[layer], ffn_norm[layer])
    out = _moe(x1, h2, w_router_group[layer], b_router_group[layer], w_router_expert[layer], b_router_expert[layer],
               w_gate[layer], w_up[layer], w_down[layer])
    return out.reshape(b, t, d)
```

```python
import functools

import jax
import jax.numpy as jnp
from jax import lax
from jax.experimental import pallas as pl
from jax.experimental.pallas import tpu as pltpu

F32 = jnp.float32
BF16 = jnp.bfloat16
I32 = jnp.int32

EPS = 1e-6
NEG = -1e30
BIG = 1e30
LOG2_E = 1.4426950408889634
ROPE_THETA = 500000.0
CHUNK = 64
A_HEADS = 16
HEAD_DIM = 128
IDX_HEADS = 16
IDX_DIM = 64
TOPK_MAX = 256
Q_BLOCK = 128
IDX_ROWS = 128
BISECT_STEPS_PER_TRIP = 2
B_QK_HEADS = 16
B_V_HEADS = 32
CONV_WIDTH = 4
N_GROUPS = 8
EXPERTS_PER_GROUP = 16
N_EXPERTS = 128
D_FF = 768
LANES = 128
VMEM_LIMIT = 56 * 1024 * 1024

MM_TM, MM_TN = 1024, 1024
MM_TN_DEEP = 512
ROW_TILE = 512
ATTN_BQ, ATTN_BK = 512, 1024
IDX_KC = 1024

NT_DIMS = (((1,), (1,)), ((), ()))


def _cp(*sem):
    return pltpu.CompilerParams(dimension_semantics=sem, vmem_limit_bytes=VMEM_LIMIT)


def _dot(a, b):
    return jnp.dot(a, b, preferred_element_type=F32)


def _dot_nt(a, b):
    return lax.dot_general(a, b, NT_DIMS, preferred_element_type=F32)


def _dot_hi(a, b):
    return jnp.dot(a, b, preferred_element_type=F32, precision=lax.Precision.HIGHEST)


def _sigmoid(x):
    return 0.5 + 0.5 * jnp.tanh(0.5 * x)


def _silu(x):
    half = 0.5 * x
    return half + half * jnp.tanh(half)


def _softplus(x):
    return jnp.maximum(x, 0.0) + jnp.log(1.0 + jnp.exp(-jnp.abs(x)))


def _rmsnorm_kernel(x_ref, g_ref, o_ref):
    x = x_ref[...]
    ms = jnp.mean(x * x, axis=-1, keepdims=True)
    o_ref[...] = (x * lax.rsqrt(ms + EPS) * g_ref[...]).astype(o_ref.dtype)


def _rmsnorm(x, gain, out_dtype=BF16):
    t, d = x.shape
    tm = min(t, ROW_TILE)
    return pl.pallas_call(
        _rmsnorm_kernel,
        grid=(t // tm,),
        in_specs=[pl.BlockSpec((tm, d), lambda i: (i, 0)), pl.BlockSpec((1, d), lambda i: (0, 0))],
        out_specs=pl.BlockSpec((tm, d), lambda i: (i, 0)),
        out_shape=jax.ShapeDtypeStruct((t, d), out_dtype),
        compiler_params=_cp("parallel"),
    )(x, gain.reshape(1, d))


def _mm_kernel(*refs, n_extra, epilogue):
    a_ref, b_ref = refs[:2]
    extra = refs[2:2 + n_extra]
    outs = refs[2 + n_extra:]
    acc = _dot(a_ref[...], b_ref[...])
    res = epilogue(acc, *[e[...] for e in extra]) if epilogue is not None else (acc,)
    for o_ref, r in zip(outs, res):
        o_ref[...] = r.astype(o_ref.dtype)


def _matmul(a, b, out_dtypes, tm, tn, epilogue=None, extras=(), b_cols=None):
    m, k = a.shape
    start, n = b_cols if b_cols is not None else (0, b.shape[1])
    tm, tn = min(tm, m), min(tn, n)
    assert start % tn == 0 and n % tn == 0 and m % tm == 0
    j0 = start // tn
    in_specs = [pl.BlockSpec((tm, k), lambda i, j: (i, 0)), pl.BlockSpec((k, tn), lambda i, j: (0, j + j0))]
    args = [a, b]
    for arr, kind in extras:
        if kind == "tile":
            in_specs.append(pl.BlockSpec((tm, tn), lambda i, j: (i, j)))
        elif kind == "rows":
            in_specs.append(pl.BlockSpec((tm, LANES), lambda i, j: (i, 0)))
        else:
            in_specs.append(pl.BlockSpec((1, tn), lambda i, j: (0, j)))
        args.append(arr)
    out = pl.pallas_call(
        functools.partial(_mm_kernel, n_extra=len(extras), epilogue=epilogue),
        grid=(m // tm, n // tn),
        in_specs=in_specs,
        out_specs=[pl.BlockSpec((tm, tn), lambda i, j: (i, j)) for _ in out_dtypes],
        out_shape=[jax.ShapeDtypeStruct((m, n), dt) for dt in out_dtypes],
        compiler_params=_cp("parallel", "arbitrary"),
    )(*args)
    return out


def _mm_nt_kernel(a_ref, b_ref, o_ref):
    res = _dot_nt(a_ref[...], b_ref[...])
    half = res.shape[0] // 2
    for c in range(o_ref.shape[0]):
        cols = slice(c * CHUNK, (c + 1) * CHUNK)
        o_ref[c] = jnp.concatenate([res[:half, cols], res[half:, cols]], axis=1)


def _matmul_nt(a, b, tm):
    n, k = a.shape
    m = b.shape[0]
    tm = min(tm, m)
    return pl.pallas_call(
        _mm_nt_kernel,
        grid=(m // tm,),
        in_specs=[pl.BlockSpec((n, k), lambda i: (0, 0)), pl.BlockSpec((tm, k), lambda i: (i, 0))],
        out_specs=pl.BlockSpec((tm // CHUNK, n // 2, 2 * CHUNK), lambda i: (i, 0, 0)),
        out_shape=jax.ShapeDtypeStruct((m // CHUNK, n // 2, 2 * CHUNK), F32),
        compiler_params=_cp("parallel"),
    )(a, b)


A_ROT_HALF = HEAD_DIM // 8
I_ROT_HALF = IDX_DIM // 8


def _rope_a(x, cos, sin):
    lane = lax.broadcasted_iota(I32, x.shape, 1)
    partner = jnp.where(lane < A_ROT_HALF, pltpu.roll(x, LANES - A_ROT_HALF, 1), pltpu.roll(x, A_ROT_HALF, 1))
    return x * cos + partner * sin


def _rope_i(x, cos, sin):
    lane = lax.broadcasted_iota(I32, x.shape, 1)
    low = (lane & (IDX_DIM - 1)) < I_ROT_HALF
    partner = jnp.where(low, pltpu.roll(x, LANES - I_ROT_HALF, 1), pltpu.roll(x, I_ROT_HALF, 1))
    return x * cos + partner * sin


def _dsa_prep_kernel(ps_ref, pos_ref, ifa_ref, ifi_ref, ikn_ref,
                     cosa_ref, sina_ref, cosi_ref, sini_ref, ki_ref, wi_ref):
    tm = ps_ref.shape[0]
    pos = pos_ref[...].astype(F32)
    lane = lax.broadcasted_iota(I32, (tm, LANES), 1)
    ang_a = pos * ifa_ref[...]
    sin_a = jnp.sin(ang_a)
    cosa_ref[...] = jnp.cos(ang_a)
    sina_ref[...] = jnp.where(lane < A_ROT_HALF, -sin_a, sin_a)
    ang_i = pos * ifi_ref[...]
    cos_i = jnp.cos(ang_i)
    sin_i = jnp.sin(ang_i)
    sin_i = jnp.where((lane & (IDX_DIM - 1)) < I_ROT_HALF, -sin_i, sin_i)
    cosi_ref[...] = cos_i
    sini_ref[...] = sin_i
    xk = ps_ref[:, 0:LANES]
    ms = jnp.sum(xk * xk, axis=-1, keepdims=True) * (1.0 / IDX_DIM)
    yk = xk * lax.rsqrt(ms + EPS) * ikn_ref[...]
    ki_ref[...] = _rope_i(yk, cos_i, sin_i).T[:IDX_DIM, :].astype(ki_ref.dtype)
    wi_ref[...] = ps_ref[:, LANES:LANES + IDX_HEADS] * (IDX_HEADS ** -0.5 * IDX_DIM ** -0.5)


def _dsa_prep(ps, positions, ifa, ifi, ikn):
    t = ps.shape[0]
    tm = min(t, ROW_TILE)
    row = lambda i: (i, 0)
    fix = lambda i: (0, 0)
    table = jax.ShapeDtypeStruct((t, LANES), F32)
    return pl.pallas_call(
        _dsa_prep_kernel,
        grid=(t // tm,),
        in_specs=[pl.BlockSpec((tm, ps.shape[1]), row), pl.BlockSpec((tm, 1), row), pl.BlockSpec((1, LANES), fix),
                  pl.BlockSpec((1, LANES), fix), pl.BlockSpec((1, LANES), fix)],
        out_specs=[pl.BlockSpec((tm, LANES), row)] * 4
        + [pl.BlockSpec((IDX_DIM, tm), lambda i: (0, i)), pl.BlockSpec((tm, IDX_HEADS), row)],
        out_shape=[table] * 4 + [jax.ShapeDtypeStruct((IDX_DIM, t), BF16), jax.ShapeDtypeStruct((t, IDX_HEADS), F32)],
        compiler_params=_cp("parallel"),
    )(ps, positions, ifa, ifi, ikn)


def _qk_epilogue(acc, gain, cos, sin):
    heads = range(acc.shape[1] // HEAD_DIM)
    xs = [acc[:, h * HEAD_DIM:(h + 1) * HEAD_DIM] for h in heads]
    ms = [jnp.mean(x * x, axis=-1, keepdims=True) for x in xs]
    rs = [lax.rsqrt(m + EPS) for m in ms]
    ys = [xs[h] * rs[h] * gain[:, h * HEAD_DIM:(h + 1) * HEAD_DIM] for h in heads]
    return (jnp.concatenate([_rope_a(y, cos, sin) for y in ys], axis=1),)


def _qidx_epilogue(acc, cos, sin):
    return (jnp.concatenate([_rope_i(acc[:, j * LANES:(j + 1) * LANES], cos, sin)
                             for j in range(acc.shape[1] // LANES)], axis=1),)


def _indexer_kernel(qi_ref, w_ref, ki_ref, bias_ref, s_ref, wb_ref, qs_ref, *, kc, topk, maxit):
    i = pl.program_id(0)
    qb = qi_ref.shape[0]
    hb = Q_BLOCK
    halves = qb // hb
    n_ch = ((i + 1) * qb + kc - 1) // kc
    nslab = kc // LANES
    sub_kc = min(kc, 2 * LANES)
    for h in range(IDX_HEADS):
        wb_ref[h] = jnp.broadcast_to(w_ref[:, h:h + 1], (qb, LANES))
        for rh in range(halves):
            qs_ref[(rh * IDX_HEADS + h) * hb:(rh * IDX_HEADS + h + 1) * hb, :] = \
                qi_ref[rh * hb:(rh + 1) * hb, h * IDX_DIM:(h + 1) * IDX_DIM]
    row = lax.broadcasted_iota(I32, (qb, LANES), 0)
    lane = lax.broadcasted_iota(I32, (qb, LANES), 1)
    limit = ((i * qb + row) // CHUNK + 1) * CHUNK

    def score_body(c, carry):
        mn, mx = carry
        off = pl.multiple_of(c * kc, kc)
        accs = [[] for _ in range(halves)]
        for sub in range(kc // sub_kc):
            kblk = ki_ref[:, pl.ds(pl.multiple_of(off + sub * sub_kc, sub_kc), sub_kc)]
            for rh in range(halves):
                part = [jnp.zeros((hb, LANES), F32) for _ in range(sub_kc // LANES)]
                s_all = _dot(qs_ref[rh * IDX_HEADS * hb:(rh + 1) * IDX_HEADS * hb, :], kblk)
                for h in range(IDX_HEADS):
                    wbh = wb_ref[h, rh * hb:(rh + 1) * hb, :]
                    for j in range(sub_kc // LANES):
                        s = s_all[h * hb:(h + 1) * hb, j * LANES:(j + 1) * LANES]
                        part[j] = part[j] + wbh * jnp.maximum(s, 0.0)
                accs[rh] += part
        for j in range(nslab):
            acc = jnp.concatenate([accs[rh][j] for rh in range(halves)], axis=0) if halves > 1 else accs[0][j]
            adm = (lane + (off + j * LANES)) < limit
            val = jnp.where(adm, acc, NEG)
            s_ref[:, pl.ds(pl.multiple_of(off + j * LANES, LANES), LANES)] = val
            mn = jnp.minimum(mn, jnp.where(adm, acc, BIG))
            mx = jnp.maximum(mx, val)
        return mn, mx

    mn, mx = lax.fori_loop(0, n_ch, score_body,
                           (jnp.full((qb, LANES), BIG, F32), jnp.full((qb, LANES), NEG, F32)))
    lo0 = jnp.min(mn, axis=-1, keepdims=True)
    hi0 = jnp.max(mx, axis=-1, keepdims=True)
    kf = float(topk)

    def count_ge(thr):
        thr_b = jnp.broadcast_to(thr, (qb, LANES))

        def body(c, acc):
            off = pl.multiple_of(c * kc, kc)
            for j in range(nslab):
                sc = s_ref[:, pl.ds(pl.multiple_of(off + j * LANES, LANES), LANES)]
                acc = acc + jnp.where(sc >= thr_b, 1.0, 0.0)
            return acc

        acc = lax.fori_loop(0, n_ch, body, jnp.zeros((qb, LANES), F32))
        return jnp.sum(acc, axis=-1, keepdims=True)

    done0 = jnp.where(limit[:, 0:1] <= topk, 1.0, 0.0)

    def cond(st):
        it, _, _, done = st
        return jnp.logical_and(it < maxit, jnp.min(done) < 0.5)

    def body(st):
        it, lo, hi, done = st
        for _ in range(BISECT_STEPS_PER_TRIP):
            mid = lo + (hi - lo) * 0.5
            stuck = jnp.logical_or(mid <= lo, mid >= hi)
            c = count_ge(mid)
            ge = c >= kf
            lo = jnp.where(ge, mid, lo)
            hi = jnp.where(ge, hi, mid)
            done = jnp.maximum(done, jnp.where(jnp.logical_or(c == kf, stuck), 1.0, 0.0))
        return it + BISECT_STEPS_PER_TRIP, lo, hi, done

    _, lo, hi, _ = lax.while_loop(cond, body, (jnp.int32(0), lo0, hi0, done0))
    c_hi = count_ge(hi)
    lo = jnp.where(c_hi >= kf, hi, lo)
    lo_b = jnp.broadcast_to(lo, (qb, LANES))

    bias_ref[...] = jnp.full(bias_ref.shape, NEG, bias_ref.dtype)

    def write_body(c, _):
        off = pl.multiple_of(c * kc, kc)
        for j in range(nslab):
            o = pl.multiple_of(off + j * LANES, LANES)
            sc = s_ref[:, pl.ds(o, LANES)]
            bias_ref[:, pl.ds(o, LANES)] = jnp.where(sc >= lo_b, 0.0, NEG).astype(bias_ref.dtype)
        return 0

    lax.fori_loop(0, n_ch, write_body, 0)


def _indexer(q_idx, w_idx, k_idx, topk):
    t = q_idx.shape[0]
    kc = min(t, IDX_KC)
    qb = IDX_ROWS
    return pl.pallas_call(
        functools.partial(_indexer_kernel, kc=kc, topk=topk, maxit=64),
        grid=(t // qb,),
        in_specs=[
            pl.BlockSpec((qb, IDX_HEADS * IDX_DIM), lambda i: (i, 0)),
            pl.BlockSpec((qb, IDX_HEADS), lambda i: (i, 0)),
            pl.BlockSpec((IDX_DIM, t), lambda i: (0, 0)),
        ],
        out_specs=pl.BlockSpec((qb, t), lambda i: (i, 0)),
        out_shape=jax.ShapeDtypeStruct((t, t), BF16),
        scratch_shapes=[pltpu.VMEM((qb, t), F32), pltpu.VMEM((IDX_HEADS, qb, LANES), F32),
                        pltpu.VMEM((IDX_HEADS * qb, IDX_DIM), BF16)],
        compiler_params=_cp("parallel"),
    )(q_idx, w_idx, k_idx)


def _last_key_block(qi, bq, bk):
    return ((qi + 1) * bq - 1) // bk


def _attn_kernel(q_ref, k_ref, v_ref, b_ref, o_ref, acc_ref, m_ref, l_ref):
    qi = pl.program_id(0)
    ki = pl.program_id(1)
    last = _last_key_block(qi, q_ref.shape[0], k_ref.shape[0])

    @pl.when(ki == 0)
    def _():
        acc_ref[...] = jnp.zeros_like(acc_ref)
        m_ref[...] = jnp.full(m_ref.shape, NEG, F32)
        l_ref[...] = jnp.zeros_like(l_ref)

    @pl.when(ki <= last)
    def _():
        bias = b_ref[...]
        ones = jnp.ones((k_ref.shape[0], HEAD_DIM), BF16)
        for h in range(A_HEADS):
            sl = slice(h * HEAD_DIM, (h + 1) * HEAD_DIM)
            s = _dot_nt(q_ref[:, sl], k_ref[:, sl]).astype(BF16) + bias
            m_prev = m_ref[h]
            m_new = jnp.maximum(m_prev, jnp.max(s, axis=-1, keepdims=True).astype(F32))
            alpha = jnp.exp2(m_prev - m_new)
            p = jnp.exp2(s - m_new[:, 0:1].astype(BF16))
            pv = _dot(p, jnp.concatenate([v_ref[:, sl], ones], axis=1))
            l_ref[h] = alpha * l_ref[h] + pv[:, HEAD_DIM:]
            acc_ref[:, sl] = alpha * acc_ref[:, sl] + pv[:, :HEAD_DIM]
            m_ref[h] = m_new

    @pl.when(ki == last)
    def _():
        for h in range(A_HEADS):
            sl = slice(h * HEAD_DIM, (h + 1) * HEAD_DIM)
            o_ref[:, sl] = (acc_ref[:, sl] / l_ref[h]).astype(o_ref.dtype)


def _attention(qk, v, bias):
    t, aw = v.shape
    bq = min(t, ATTN_BQ)
    bk = min(t, ATTN_BK)
    kv_map = lambda i, j: (jnp.minimum(j, _last_key_block(i, bq, bk)), 0)
    return pl.pallas_call(
        _attn_kernel,
        grid=(t // bq, t // bk),
        in_specs=[
            pl.BlockSpec((bq, aw), lambda i, j: (i, 0)),
            pl.BlockSpec((bk, aw), lambda i, j: (jnp.minimum(j, _last_key_block(i, bq, bk)), 1)),
            pl.BlockSpec((bk, aw), kv_map),
            pl.BlockSpec((bq, bk), lambda i, j: (i, jnp.minimum(j, _last_key_block(i, bq, bk)))),
        ],
        out_specs=pl.BlockSpec((bq, aw), lambda i, j: (i, 0)),
        out_shape=jax.ShapeDtypeStruct((t, aw), BF16),
        scratch_shapes=[pltpu.VMEM((bq, aw), F32), pltpu.VMEM((A_HEADS, bq, LANES), F32),
                        pltpu.VMEM((A_HEADS, bq, LANES), F32)],
        compiler_params=_cp("parallel", "arbitrary"),
    )(qk, qk, v, bias)


N_PAIRS = B_V_HEADS // 2


def _tri_inverse_minus_eye(a, row, col, lane_lo):
    n = len(a)
    lane_hi = jnp.logical_not(lane_lo)

    def mm(x, y):
        bd = jnp.concatenate([jnp.where(lane_lo, y, 0.0), jnp.where(lane_hi, y, 0.0)], axis=0)
        return _dot(x.astype(BF16), bd.astype(BF16))

    blk16 = (row // 16) == (col // 16)
    blk32 = (row // 32) == (col // 32)
    d1 = [jnp.where(blk16, a[p], 0.0) for p in range(n)]
    e = [-d1[p] for p in range(n)]
    dk = d1
    for _ in range(3):
        dk = [mm(dk[p], dk[p]) for p in range(n)]
        de = [mm(dk[p], e[p]) for p in range(n)]
        e = [e[p] + dk[p] + de[p] for p in range(n)]
    for sel in (jnp.logical_and(blk32, jnp.logical_not(blk16)), jnp.logical_not(blk32)):
        lo = [jnp.where(sel, a[p], 0.0) for p in range(n)]
        m1 = [lo[p] + mm(e[p], lo[p]) for p in range(n)]
        m2 = [m1[p] + mm(m1[p], e[p]) for p in range(n)]
        e = [e[p] - m2[p] for p in range(n)]
    return e


def _gdn_kernel(cur_ref, prev_ref, cw_ref, ps_ref, pst_ref, alr_ref, alc_ref, dtr_ref, dtc_ref, z_ref, gn_ref,
                o_ref, s_ref):
    c = CHUNK
    pairs = range(N_PAIRS)
    heads = range(B_V_HEADS)
    step = pl.program_id(0)

    @pl.when(step == 0)
    def _():
        s_ref[...] = jnp.zeros_like(s_ref)

    keep_prev = jnp.where(step > 0, 1.0, 0.0)
    halo = CONV_WIDTH - 1

    def conv_silu(col0):
        sl = slice(col0, col0 + HEAD_DIM)
        cur = cur_ref[:, sl]
        xcat = jnp.concatenate([prev_ref[:, sl] * keep_prev, cur], axis=0)
        y = cur * cw_ref[halo:halo + 1, sl]
        for tap in range(halo):
            y = y + xcat[8 - halo + tap:8 - halo + tap + c, :] * cw_ref[tap:tap + 1, sl]
        return _silu(y)

    def l2n(y, scale):
        return y * (lax.rsqrt(jnp.sum(y * y, axis=-1, keepdims=True) + EPS) * scale)

    q = [l2n(conv_silu(p * HEAD_DIM), HEAD_DIM ** -0.5) for p in pairs]
    k = [l2n(conv_silu((B_QK_HEADS + p) * HEAD_DIM), 1.0) for p in pairs]
    v = [conv_silu((2 * B_QK_HEADS + h) * HEAD_DIM) for h in heads]
    qbf = [q[p].astype(BF16) for p in pairs]
    kbf = [k[p].astype(BF16) for p in pairs]

    row = lax.broadcasted_iota(I32, (c, LANES), 0)
    lane = lax.broadcasted_iota(I32, (c, LANES), 1)
    col = lane & (c - 1)
    lane_lo = lane < c
    lane_hi = jnp.logical_not(lane_lo)
    incl = row >= col
    strict = row > col
    eye2 = jnp.where(row == col, 1.0, 0.0)
    r64 = lax.broadcasted_iota(I32, (c, c), 0)
    c64 = lax.broadcasted_iota(I32, (c, c), 1)
    lower = jnp.where(r64 >= c64, 1.0, 0.0)
    r128 = lax.broadcasted_iota(I32, (LANES, LANES), 0)
    c128 = lax.broadcasted_iota(I32, (LANES, LANES), 1)
    upper2 = jnp.where(jnp.logical_and(r128 // c == c128 // c, r128 <= c128), 1.0, 0.0)

    beta_col = _sigmoid(ps_ref[:, 2 * LANES:2 * LANES + B_V_HEADS])
    g_col = -jnp.exp(alr_ref[...]) * _softplus(ps_ref[:, 3 * LANES:3 * LANES + B_V_HEADS] + dtr_ref[...])
    g_row2 = -jnp.exp(alc_ref[...]) * _softplus(pst_ref[0] + dtc_ref[...])
    gc_col = _dot_hi(lower, g_col)
    gc_row2 = _dot_hi(g_row2, upper2)
    eg_col = jnp.exp(gc_col)
    eg_last = eg_col[c - 1:c, :]

    def col_pair(m, p):
        return jnp.where(lane_lo, m[:, 2 * p:2 * p + 1], m[:, 2 * p + 1:2 * p + 2])

    g = [_dot_nt(jnp.concatenate([qbf[p], kbf[p]], axis=0),
                 jnp.concatenate([kbf[p], kbf[p]], axis=0)) for p in pairs]
    decay = [jnp.exp(jnp.where(incl, col_pair(gc_col, p) - gc_row2[p:p + 1, :], NEG)) for p in pairs]
    a_mat = [jnp.where(strict, col_pair(beta_col, p) * g[p][c:] * decay[p], 0.0) for p in pairs]
    attn = [g[p][:c] * decay[p] for p in pairs]
    e = _tri_inverse_minus_eye(a_mat, row, col, lane_lo)

    def rhs_of(p, h):
        bc = beta_col[:, h:h + 1]
        return jnp.concatenate([v[h] * bc, k[p] * (bc * eg_col[:, h:h + 1])], axis=1)

    rhs = [jnp.concatenate([rhs_of(p, 2 * p), rhs_of(p, 2 * p + 1)], axis=0).astype(BF16) for p in pairs]
    t_mat = [eye2 + e[p] for p in pairs]
    uw = [_dot(jnp.where(sel, t_mat[h // 2], 0.0).astype(BF16), rhs[h // 2])
          for h in heads for sel in ((lane_lo, lane_hi)[h % 2],)]
    wq = [jnp.concatenate([uw[h][:, HEAD_DIM:], q[h // 2] * eg_col[:, h:h + 1]], axis=0).astype(BF16)
          for h in heads]
    kd = [k[h // 2] * jnp.exp(gc_col[c - 1:c, h:h + 1] - gc_col[:, h:h + 1]) for h in heads]
    ak = [jnp.concatenate([attn[p], jnp.concatenate([kd[2 * p], kd[2 * p + 1]], axis=0).T], axis=0).astype(BF16)
          for p in pairs]

    lane_lo_b = lax.broadcasted_iota(I32, (c + HEAD_DIM, LANES), 1) < c
    zero = jnp.zeros((), BF16)
    state = [s_ref[h] for h in heads]
    r1 = [_dot(wq[h], state[h].astype(BF16)) for h in heads]
    vn = [uw[h][:, :HEAD_DIM] - r1[h][:c] for h in heads]
    r2 = []
    for p in pairs:
        vst = jnp.concatenate([vn[2 * p], vn[2 * p + 1]], axis=0).astype(BF16)
        r2.append(_dot(jnp.where(lane_lo_b, ak[p], zero), vst))
        r2.append(_dot(jnp.where(lane_lo_b, zero, ak[p]), vst))
    gn = gn_ref[...]
    for h in heads:
        sl = slice(h * HEAD_DIM, (h + 1) * HEAD_DIM)
        o = r1[h][c:] + r2[h][:c]
        s_ref[h] = state[h] * eg_last[:, h:h + 1] + r2[h][c:]
        on = o * lax.rsqrt(jnp.mean(o * o, axis=-1, keepdims=True) + EPS) * gn
        z = z_ref[:, sl].astype(F32)
        o_ref[:, sl] = (on * _silu(z)).astype(o_ref.dtype)


def _gdn(p4, conv_w, ps, pst, a_log, dt_bias, z, out_norm):
    t, ch = p4.shape
    n = t // CHUNK
    hv = B_V_HEADS
    fix2 = lambda i: (0, 0)
    return pl.pallas_call(
        _gdn_kernel,
        grid=(n,),
        in_specs=[
            pl.BlockSpec((CHUNK, ch), lambda i: (i, 0)),
            pl.BlockSpec((8, ch), lambda i: (jnp.maximum(i * (CHUNK // 8) - 1, 0), 0)),
            pl.BlockSpec((CONV_WIDTH, ch), fix2),
            pl.BlockSpec((CHUNK, ps.shape[1]), lambda i: (i, 0)),
            pl.BlockSpec((1, N_PAIRS, LANES), lambda i: (i, 0, 0)),
            pl.BlockSpec((1, hv), fix2), pl.BlockSpec((N_PAIRS, LANES), fix2),
            pl.BlockSpec((1, hv), fix2), pl.BlockSpec((N_PAIRS, LANES), fix2),
            pl.BlockSpec((CHUNK, hv * HEAD_DIM), lambda i: (i, 0)),
            pl.BlockSpec((1, HEAD_DIM), fix2),
        ],
        out_specs=pl.BlockSpec((CHUNK, hv * HEAD_DIM), lambda i: (i, 0)),
        out_shape=jax.ShapeDtypeStruct((t, hv * HEAD_DIM), BF16),
        scratch_shapes=[pltpu.VMEM((hv, HEAD_DIM, HEAD_DIM), F32)],
        compiler_params=_cp("arbitrary"),
    )(p4, p4, conv_w, ps, pst, a_log.reshape(1, hv), _pair_lanes(a_log), dt_bias.reshape(1, hv),
      _pair_lanes(dt_bias), z, out_norm.reshape(1, HEAD_DIM))


def _pair_lanes(per_head):
    return jnp.repeat(per_head.reshape(N_PAIRS, 2), CHUNK, axis=1)


_A_W = A_HEADS * HEAD_DIM
_A_IQ = IDX_HEADS * IDX_DIM
_B_QK = B_QK_HEADS * HEAD_DIM
_B_V = B_V_HEADS * HEAD_DIM
_OFF_AIK = 3 * _A_W + _A_IQ
_OFF_AIW = _OFF_AIK + IDX_DIM
_OFF_BQ = _OFF_AIW + IDX_HEADS
_OFF_BZ = _OFF_BQ + 2 * _B_QK + _B_V
_OFF_BB = _OFF_BZ + _B_V
_OFF_BA = _OFF_BB + B_V_HEADS
_OFF_GATE = _OFF_BA + B_V_HEADS


_PK_A = 0
_PK_B = 3 * _A_W + _A_IQ
_PK_G = _PK_B + 2 * _B_QK + 2 * _B_V
_PK_S = _PK_G + 2 * _A_W
_PK_S_COLS = 4 * LANES


def _pack_in_weights(w_in):
    pad = lambda a: jnp.pad(a, ((0, 0), (0, LANES - a.shape[1])))
    return jnp.concatenate([
        w_in[:, 0:_OFF_AIK], w_in[:, _OFF_BQ:_OFF_BB], w_in[:, _OFF_GATE:_OFF_GATE + 2 * _A_W],
        pad(w_in[:, _OFF_AIK:_OFF_AIK + IDX_DIM]), pad(w_in[:, _OFF_AIW:_OFF_AIW + IDX_HEADS]),
        pad(w_in[:, _OFF_BB:_OFF_BB + B_V_HEADS]), pad(w_in[:, _OFF_BA:_OFF_BA + B_V_HEADS]),
    ], axis=1).astype(BF16)


def _ba_weights_t(w_in):
    w = w_in[:, _OFF_BA:_OFF_BA + B_V_HEADS]
    return jnp.concatenate([w[:, 0::2], w[:, 1::2]], axis=1).T.astype(BF16)


def _rope_freqs(rot, width):
    half = rot // 2
    inv = jnp.power(ROPE_THETA, -jnp.arange(half, dtype=F32) * 2.0 / rot)
    pat = jnp.concatenate([inv, inv, jnp.zeros((width - rot,), F32)])
    return jnp.tile(pat, LANES // width).reshape(1, LANES)


def _dsa_branch(h, w_pk, ps, positions, q_norm, k_norm, idx_k_norm):
    t = h.shape[0]
    tm = MM_TM
    ikn = jnp.pad(idx_k_norm.reshape(1, IDX_DIM), ((0, 0), (0, LANES - IDX_DIM)))
    cos_a, sin_a, cos_i, sin_i, k_idx, w_idx = _dsa_prep(
        ps, positions.reshape(t, 1), _rope_freqs(HEAD_DIM // 4, HEAD_DIM), _rope_freqs(IDX_DIM // 4, IDX_DIM), ikn)
    scale = HEAD_DIM ** -0.5 * LOG2_E
    gain = jnp.concatenate([jnp.tile(q_norm * scale, A_HEADS), jnp.tile(k_norm, A_HEADS)]).reshape(1, 2 * _A_W)
    (qk,) = _matmul(h, w_pk, [BF16], tm, MM_TN, _qk_epilogue, [(gain, "row"), (cos_a, "rows"), (sin_a, "rows")],
                    b_cols=(_PK_A, 2 * _A_W))
    (v,) = _matmul(h, w_pk, [BF16], tm, MM_TN, b_cols=(_PK_A + 2 * _A_W, _A_W))
    (q_idx,) = _matmul(h, w_pk, [BF16], tm, MM_TN, _qidx_epilogue, [(cos_i, "rows"), (sin_i, "rows")],
                       b_cols=(_PK_A + 3 * _A_W, _A_IQ))
    bias = _indexer(q_idx, w_idx, k_idx, min(TOPK_MAX, t // 4))
    return _attention(qk, v, bias)


def _gdn_branch(h, w_pk, ps, pst, conv_w, a_log, dt_bias, out_norm):
    tm = MM_TM
    (p4,) = _matmul(h, w_pk, [F32], tm, MM_TN, b_cols=(_PK_B, 2 * _B_QK + _B_V))
    (z,) = _matmul(h, w_pk, [BF16], tm, MM_TN, b_cols=(_PK_B + 2 * _B_QK + _B_V, _B_V))
    return _gdn(p4, conv_w, ps, pst, a_log, dt_bias, z, out_norm)


def _gate_a_epilogue(acc, g):
    return (_sigmoid(g.astype(F32)) * acc,)


def _gate_b_epilogue(acc, g, m):
    return (m + _sigmoid(g.astype(F32)) * acc,)


def _residual_norm_epilogue(acc, x, gain):
    x1 = x + acc
    h2 = x1 * lax.rsqrt(jnp.mean(x1 * x1, axis=-1, keepdims=True) + EPS) * gain
    return x1, h2


def _mix(x, h, w_pk, out_a, out_b, w_o_a, w_o_b, w_out, ffn_norm):
    d = x.shape[1]
    (ga,) = _matmul(h, w_pk, [BF16], MM_TM, MM_TN, b_cols=(_PK_G, d))
    (gb,) = _matmul(h, w_pk, [BF16], MM_TM, MM_TN, b_cols=(_PK_G + d, d))
    (mixa,) = _matmul(out_a, w_o_a.astype(BF16), [F32], MM_TM, MM_TN, _gate_a_epilogue, [(ga, "tile")])
    (mix,) = _matmul(out_b, w_o_b.astype(BF16), [BF16], MM_TM, MM_TN_DEEP, _gate_b_epilogue,
                     [(gb, "tile"), (mixa, "tile")])
    return _matmul(mix, w_out.astype(BF16), [F32, F32], ROW_TILE, d, _residual_norm_epilogue,
                   [(x, "tile"), (ffn_norm.reshape(1, d), "row")])


MOE_TM = 256
MOE_NF = 1
_R_E1, _R_E2, _R_RANK1, _R_RANK2, _R_W1, _R_W2 = 0, 1, 2, 3, 4, 5


def _router_kernel(h_ref, w_ref, b_ref, r_ref, cnt_ref, carry_ref):
    i = pl.program_id(0)

    @pl.when(i == 0)
    def _():
        carry_ref[...] = jnp.zeros_like(carry_ref)

    tb = h_ref.shape[0]
    logits = _dot(h_ref[...].astype(BF16), w_ref[...]) + b_ref[...]
    le = logits[:, :N_EXPERTS]
    lg = logits[:, N_EXPERTS:]
    lane = lax.broadcasted_iota(I32, (tb, LANES), 1)
    lanef = lane.astype(F32)
    far = float(4 * LANES)

    lgm = jnp.where(lane < N_GROUPS, lg, NEG)
    gmax = jnp.max(lgm, axis=-1, keepdims=True)
    g_idx = jnp.min(jnp.where(lgm == gmax, lanef, far), axis=-1, keepdims=True)
    g_top = 1.0 / jnp.sum(jnp.exp(lgm - gmax), axis=-1, keepdims=True)

    in_grp = (lane // EXPERTS_PER_GROUP).astype(F32) == g_idx
    lem = jnp.where(in_grp, le, NEG)
    e1 = jnp.max(lem, axis=-1, keepdims=True)
    i1 = jnp.min(jnp.where(lem == e1, lanef, far), axis=-1, keepdims=True)
    lem2 = jnp.where(lanef == i1, NEG, lem)
    e2 = jnp.max(lem2, axis=-1, keepdims=True)
    i2 = jnp.min(jnp.where(lem2 == e2, lanef, far), axis=-1, keepdims=True)
    se = jnp.sum(jnp.exp(lem - e1), axis=-1, keepdims=True)
    p1 = 1.0 / se
    p2 = jnp.exp(e2 - e1) / se
    w1 = g_top * p1 / (p1 + p2)
    w2 = g_top * p2 / (p1 + p2)

    o1 = jnp.where(lanef == i1, 1.0, 0.0)
    o2 = jnp.where(lanef == i2, 1.0, 0.0)
    osum = o1 + o2
    rr = lax.broadcasted_iota(I32, (tb, tb), 0)
    cc = lax.broadcasted_iota(I32, (tb, tb), 1)
    before = jnp.where(cc < rr, 1.0, 0.0).astype(BF16)
    prefix = _dot(before, osum.astype(BF16)) + carry_ref[0:1, :]
    rank1 = jnp.sum(prefix * o1, axis=-1, keepdims=True)
    rank2 = jnp.sum(prefix * o2, axis=-1, keepdims=True)
    carry_ref[...] = carry_ref[...] + jnp.sum(osum, axis=0, keepdims=True)
    cnt_ref[...] = carry_ref[...]

    rec = jnp.zeros((tb, LANES), F32)
    for idx, val in ((_R_E1, i1), (_R_E2, i2), (_R_RANK1, rank1), (_R_RANK2, rank2), (_R_W1, w1), (_R_W2, w2)):
        rec = jnp.where(lane == idx, val, rec)
    r_ref[...] = rec


def _router(h2, w_router, b_router):
    t, d = h2.shape
    tb = min(t, ROW_TILE)
    return pl.pallas_call(
        _router_kernel,
        grid=(t // tb,),
        in_specs=[pl.BlockSpec((tb, d), lambda i: (i, 0)), pl.BlockSpec((d, 2 * LANES), lambda i: (0, 0)),
                  pl.BlockSpec((1, 2 * LANES), lambda i: (0, 0))],
        out_specs=[pl.BlockSpec((tb, LANES), lambda i: (i, 0)), pl.BlockSpec((8, LANES), lambda i: (0, 0))],
        out_shape=[jax.ShapeDtypeStruct((t, LANES), F32), jax.ShapeDtypeStruct((8, LANES), F32)],
        scratch_shapes=[pltpu.VMEM((8, LANES), F32)],
        compiler_params=_cp("arbitrary"),
    )(h2, w_router, b_router)


def _positions_kernel(r_ref, cnt_ref, pos_ref, te_ref, nv_ref):
    t = r_ref.shape[0]
    nt = te_ref.shape[0]
    cnt = cnt_ref[...]
    tiles = jnp.floor((cnt + (MOE_TM - 1)) * (1.0 / MOE_TM))
    rr = lax.broadcasted_iota(I32, (LANES, LANES), 0)
    cc = lax.broadcasted_iota(I32, (LANES, LANES), 1)
    start_tiles = _dot(tiles.astype(BF16), jnp.where(rr < cc, 1.0, 0.0).astype(BF16))
    start = start_tiles[0:1, :] * float(MOE_TM)
    rec = r_ref[...]
    lane = lax.broadcasted_iota(I32, (t, LANES), 1)
    lanef = lane.astype(F32)
    pos1 = jnp.sum(jnp.where(lanef == rec[:, _R_E1:_R_E1 + 1], start, 0.0), axis=-1, keepdims=True) \
        + rec[:, _R_RANK1:_R_RANK1 + 1]
    pos2 = jnp.sum(jnp.where(lanef == rec[:, _R_E2:_R_E2 + 1], start, 0.0), axis=-1, keepdims=True) \
        + rec[:, _R_RANK2:_R_RANK2 + 1]
    pos_ref[...] = jnp.where(lane == 0, pos1, jnp.where(lane == 1, pos2, 0.0)).astype(I32)
    tile_id = lax.broadcasted_iota(I32, (nt, LANES), 0).astype(F32)
    tlane = lax.broadcasted_iota(I32, (nt, LANES), 1)
    owner = jnp.sum(jnp.where(start_tiles[0:1, :] <= tile_id, 1.0, 0.0), axis=-1, keepdims=True) - 1.0
    is_owner = tlane.astype(F32) == owner
    own_cnt = jnp.sum(jnp.where(is_owner, cnt[0:1, :], 0.0), axis=-1, keepdims=True)
    own_start = jnp.sum(jnp.where(is_owner, start_tiles[0:1, :], 0.0), axis=-1, keepdims=True)
    rows = jnp.clip(own_cnt - (tile_id[:, 0:1] - own_start) * float(MOE_TM), 0.0, float(MOE_TM))
    te_ref[...] = jnp.where(tlane == 0, owner, jnp.where(tlane == 1, rows, 0.0)).astype(I32)
    nv_ref[...] = jnp.broadcast_to(jnp.sum(tiles[0:1, :], axis=-1, keepdims=True), (8, LANES)).astype(I32)


def _positions(rec, cnt, n_tiles):
    t = rec.shape[0]
    full = lambda shape: pl.BlockSpec(shape, lambda i: (0, 0))
    return pl.pallas_call(
        _positions_kernel,
        grid=(1,),
        in_specs=[full((t, LANES)), full((8, LANES))],
        out_specs=[full((t, LANES)), full((n_tiles, LANES)), full((8, LANES))],
        out_shape=[jax.ShapeDtypeStruct((t, LANES), I32), jax.ShapeDtypeStruct((n_tiles, LANES), I32),
                   jax.ShapeDtypeStruct((8, LANES), I32)],
        compiler_params=_cp("arbitrary"),
    )(rec, cnt)


def _invert_kernel(pos_ref, asg_ref):
    def scatter(a, _):
        asg_ref[pos_ref[a]] = a
        return 0

    lax.fori_loop(0, pos_ref.shape[0], scatter, 0, unroll=8)


def _invert(pos, n_slots):
    return pl.pallas_call(
        _invert_kernel,
        in_specs=[pl.BlockSpec(memory_space=pltpu.SMEM)],
        out_specs=pl.BlockSpec(memory_space=pltpu.SMEM),
        out_shape=jax.ShapeDtypeStruct((n_slots,), I32),
    )(pos)


ROW_UNROLL = 8


def _for_each_row(n, body):
    full = lax.shift_right_logical(n, ROW_UNROLL.bit_length() - 1)

    def group(g, _):
        for u in range(ROW_UNROLL):
            body(g * ROW_UNROLL + u)
        return 0

    lax.fori_loop(0, full, group, 0)

    def single(r, _):
        body(r)
        return 0

    lax.fori_loop(full * ROW_UNROLL, n, single, 0)


def _ffn_kernel(te_ref, rows_ref, nv_ref, asg_ref, h_ref, wg_ref, wu_ref, wd_ref, y_ref,
                xbuf_ref, x16_ref, obuf_ref, gsem, ssem):
    i = pl.program_id(0)
    f = pl.program_id(1)
    tm, d = x16_ref.shape
    nv = nv_ref[0]
    last_f = pl.num_programs(1) - 1

    def gather_copy(tile, slot, r):
        token = lax.shift_right_logical(asg_ref[tile * tm + r], 1)
        return pltpu.make_async_copy(h_ref.at[pl.ds(token, 1)], xbuf_ref.at[slot, pl.ds(r, 1)], gsem.at[slot])

    def scatter_copy(tile, r):
        a = asg_ref[tile * tm + r]
        col = pl.multiple_of(jnp.bitwise_and(a, 1) * d, d)
        return pltpu.make_async_copy(obuf_ref.at[pl.ds(r, 1)],
                                     y_ref.at[pl.ds(lax.shift_right_logical(a, 1), 1), pl.ds(col, d)], ssem)

    def gather(tile, slot):
        _for_each_row(rows_ref[tile], lambda r: gather_copy(tile, slot, r).start())

    def scatter_wait(tile):
        _for_each_row(rows_ref[tile], lambda r: scatter_copy(tile, r).wait())

    @pl.when(jnp.logical_and(i == 0, f == 0))
    def _():
        xbuf_ref[...] = jnp.zeros_like(xbuf_ref)
        gather(0, 0)

    @pl.when(jnp.logical_and(f == 0, i + 1 < nv))
    def _():
        gather(i + 1, jnp.bitwise_and(i + 1, 1))

    @pl.when(jnp.logical_and(f == 0, i < nv))
    def _():
        slot = jnp.bitwise_and(i, 1)
        _for_each_row(rows_ref[i], lambda r: gather_copy(i, slot, r).wait())
        x16_ref[...] = xbuf_ref[slot].astype(x16_ref.dtype)

    @pl.when(i < nv)
    def _():
        x = x16_ref[...]
        g = _dot(x, wg_ref[0].astype(BF16))
        u = _dot(x, wu_ref[0].astype(BF16))
        act = _silu(g) * u
        y = _dot(act.astype(BF16), wd_ref[0].astype(BF16))

        @pl.when(f == 0)
        def _():
            @pl.when(i > 0)
            def _():
                scatter_wait(i - 1)

            obuf_ref[...] = y

        @pl.when(f > 0)
        def _():
            obuf_ref[...] = obuf_ref[...] + y

        @pl.when(f == last_f)
        def _():
            _for_each_row(rows_ref[i], lambda r: scatter_copy(i, r).start())

            @pl.when(i == nv - 1)
            def _():
                scatter_wait(i)


def _ffn(te, rows, nv, asg, h2, w_gate, w_up, w_down, n_tiles):
    t, d = h2.shape
    fb = D_FF // MOE_NF

    def tile(i, nv):
        return jnp.minimum(i, nv[0] - 1)

    def fblk(i, f, nv):
        return jnp.where(i < nv[0], f, MOE_NF - 1)

    any_spec = pl.BlockSpec(memory_space=pl.ANY)
    return pl.pallas_call(
        _ffn_kernel,
        grid_spec=pltpu.PrefetchScalarGridSpec(
            num_scalar_prefetch=4,
            grid=(n_tiles, MOE_NF),
            in_specs=[
                any_spec,
                pl.BlockSpec((1, d, fb), lambda i, f, te, rows, nv, asg: (te[tile(i, nv)], 0, fblk(i, f, nv))),
                pl.BlockSpec((1, d, fb), lambda i, f, te, rows, nv, asg: (te[tile(i, nv)], 0, fblk(i, f, nv))),
                pl.BlockSpec((1, fb, d), lambda i, f, te, rows, nv, asg: (te[tile(i, nv)], fblk(i, f, nv), 0)),
            ],
            out_specs=any_spec,
            scratch_shapes=[pltpu.VMEM((2, MOE_TM, d), F32), pltpu.VMEM((MOE_TM, d), BF16),
                            pltpu.VMEM((MOE_TM, d), F32), pltpu.SemaphoreType.DMA((2,)),
                            pltpu.SemaphoreType.DMA(())],
        ),
        out_shape=jax.ShapeDtypeStruct((t, 2 * d), F32),
        compiler_params=_cp("arbitrary", "arbitrary"),
    )(te, rows, nv, asg, h2, w_gate, w_up, w_down)


def _combine_kernel(y_ref, x_ref, r_ref, o_ref):
    d = x_ref.shape[1]
    rec = r_ref[...]
    o_ref[...] = x_ref[...] + rec[:, _R_W1:_R_W1 + 1] * y_ref[:, :d] + rec[:, _R_W2:_R_W2 + 1] * y_ref[:, d:]


def _combine(y, x1, rec):
    t, d = x1.shape
    tb = min(t, 256)
    return pl.pallas_call(
        _combine_kernel,
        grid=(t // tb,),
        in_specs=[pl.BlockSpec((tb, 2 * d), lambda i: (i, 0)), pl.BlockSpec((tb, d), lambda i: (i, 0)),
                  pl.BlockSpec((tb, LANES), lambda i: (i, 0))],
        out_specs=pl.BlockSpec((tb, d), lambda i: (i, 0)),
        out_shape=jax.ShapeDtypeStruct((t, d), F32),
        compiler_params=_cp("parallel"),
    )(y, x1, rec)


def _moe(x1, h2, w_rg, b_rg, w_re, b_re, w_gate, w_up, w_down):
    t, d = x1.shape
    pad_w = jnp.zeros((d, LANES - N_GROUPS), F32)
    w_router = jnp.concatenate([w_re, w_rg, pad_w], axis=1).astype(BF16)
    b_router = jnp.concatenate([b_re, b_rg, jnp.zeros((LANES - N_GROUPS,), F32)]).reshape(1, 2 * LANES)
    n_tiles = 2 * t // MOE_TM + N_EXPERTS
    rec, cnt = _router(h2, w_router, b_router)
    pos2d, te2d, nv2d = _positions(rec, cnt, n_tiles)
    pos = pos2d[:, :2].reshape(2 * t)
    te = te2d[:, 0]
    rows = te2d[:, 1]
    nv = nv2d[0, :1]
    asg = _invert(pos, n_tiles * MOE_TM)
    y = _ffn(te, rows, nv, asg, h2, w_gate, w_up, w_down, n_tiles)
    return _combine(y, x1, rec)


def kernel(x, positions, attn_norm, w_in, q_norm, k_norm, idx_k_norm, conv_w, a_log, dt_bias, gdn_out_norm,
           w_o_a, w_o_b, w_out, ffn_norm, w_router_group, b_router_group, w_router_expert, b_router_expert,
           w_gate, w_up, w_down):
    b, t, d = x.shape
    x2 = x.reshape(t, d)
    layer = 0
    h = _rmsnorm(x2, attn_norm[layer])
    w_pk = _pack_in_weights(w_in[layer])
    (ps,) = _matmul(h, w_pk, [F32], MM_TM, _PK_S_COLS, b_cols=(_PK_S, _PK_S_COLS))
    pst = _matmul_nt(_ba_weights_t(w_in[layer]), h, MM_TM)
    out_a = _dsa_branch(h, w_pk, ps, positions, q_norm[layer], k_norm[layer], idx_k_norm[layer])
    out_b = _gdn_branch(h, w_pk, ps, pst, conv_w[layer], a_log[layer], dt_bias[layer], gdn_out_norm[layer])
    x1, h2 = _mix(x2, h, w_pk, out_a, out_b, w_o_a[layer], w_o_b[layer], w_out[layer], ffn_norm[layer])
    out = _moe(x1, h2, w_router_group[layer], b_router_group[layer], w_router_expert[layer], b_router_expert[layer],
               w_gate[layer], w_up[layer], w_down[layer])
    return out.reshape(b, t, d)
```
